```python
import math
import jax, jax.numpy as jnp
from jax import lax
import numpy as np

D_MODEL = 4096
BATCH = 1
SEQ = 8192
DEPTH = 2

N_MIXERS = 2
N_POOL_LAYERS = (DEPTH + 1) // 2
N_SSD_LAYERS = DEPTH // 2

DEEPNORM_ALPHA = (2 * DEPTH) ** 0.25
DEEPNORM_BETA = (8 * DEPTH) ** -0.25
LN_EPS = 1e-5

POOL_WINDOWS = (2, 4, 8, 16)
POOL_GROUPS = len(POOL_WINDOWS)
POOL_GC = D_MODEL // POOL_GROUPS

SSD_EXPAND = 2
D_INNER = SSD_EXPAND * D_MODEL
SSD_HEAD_DIM = 64
SSD_HEADS = D_INNER // SSD_HEAD_DIM
SSD_GROUPS = 8
SSD_HPG = SSD_HEADS // SSD_GROUPS
SSD_STATE = 128
SSD_CONV = 4
SSD_CHUNK = 128
SSD_CONV_DIM = D_INNER + 2 * SSD_GROUPS * SSD_STATE
SSD_IN_DIM = D_INNER + SSD_CONV_DIM + SSD_HEADS
RMS_EPS = 1e-5

N_EXPERTS = 32
N_EXPERT_GROUPS = 8
EXPERTS_PER_GROUP = N_EXPERTS // N_EXPERT_GROUPS
TOP_K = 2
D_FF = 768
BLOCK_ROWS = 128

kernel_name = 'hybrid_pool_ssd_moe'


def layer_norm(x, g, b):
    xf = x.astype(jnp.float32)
    mu = jnp.mean(xf, axis=-1, keepdims=True)
    xc = xf - mu
    var = jnp.mean(xc * xc, axis=-1, keepdims=True)
    y = xc * lax.rsqrt(var + LN_EPS) * g.astype(jnp.float32) + b.astype(jnp.float32)
    return y.astype(x.dtype)


def pool_mixer(x, w_in, w_group, scale, w_out):
    b, l, d = x.shape
    u = (x @ w_in).astype(jnp.float32)
    csum = jnp.cumsum(u, axis=1)
    pos = jnp.arange(1, l + 1, dtype=jnp.float32)
    parts = []
    for g, w in enumerate(POOL_WINDOWS):
        c = csum[..., g * POOL_GC:(g + 1) * POOL_GC]
        lag = jnp.pad(c, ((0, 0), (w, 0), (0, 0)))[:, :l]
        mean = (c - lag) / jnp.minimum(pos, float(w))[None, :, None]
        parts.append(mean - u[..., g * POOL_GC:(g + 1) * POOL_GC])
    p = jnp.stack(parts, axis=2).astype(x.dtype)
    mixed = jnp.einsum('blgc,gcd->blgd', p, w_group).reshape(b, l, d) * scale
    return mixed @ w_out


def causal_depthwise_conv(u, w, bias):
    c = u.shape[-1]
    out = lax.conv_general_dilated(
        u, w[:, None, :], window_strides=(1,), padding=[(SSD_CONV - 1, 0)],
        dimension_numbers=('NWC', 'WIO', 'NWC'), feature_group_count=c)
    return out + bias


def ssd_chunked(xs, dt, a, bmat, cmat):
    b, l = xs.shape[:2]
    nc = l // SSD_CHUNK
    q = SSD_CHUNK
    xs = xs.reshape(b, nc, q, SSD_GROUPS, SSD_HPG, SSD_HEAD_DIM)
    dt = dt.reshape(b, nc, q, SSD_GROUPS, SSD_HPG)
    bmat = bmat.reshape(b, nc, q, SSD_GROUPS, SSD_STATE)
    cmat = cmat.reshape(b, nc, q, SSD_GROUPS, SSD_STATE)
    a_cum = jnp.cumsum(dt * a, axis=2)
    a_t = jnp.moveaxis(a_cum, 2, -1)
    dt_t = jnp.moveaxis(dt, 2, -1)
    mask = jnp.tril(jnp.ones((q, q), dtype=bool))
    seg = a_t[..., :, None] - a_t[..., None, :]
    cb = jnp.einsum('bclgn,bcsgn->bcgls', cmat, bmat)
    m = cb[:, :, :, None] * jnp.exp(jnp.where(mask, seg, -jnp.inf)) * dt_t[..., None, :]
    y_diag = jnp.einsum('bcghls,bcsghp->bclghp', m, xs)
    decay_states = jnp.exp(a_cum[:, :, -1:] - a_cum)
    states = jnp.einsum('bcsgn,bcsgh,bcsghp->bcghpn', bmat, decay_states * dt, xs)
    chunk_decay = jnp.exp(a_cum[:, :, -1])

    def step(h, inp):
        s_c, d_c = inp
        return d_c[..., None, None] * h + s_c, h

    h0 = jnp.zeros((b, SSD_GROUPS, SSD_HPG, SSD_HEAD_DIM, SSD_STATE), jnp.float32)
    _, prev = lax.scan(step, h0, (jnp.moveaxis(states, 1, 0), jnp.moveaxis(chunk_decay, 1, 0)))
    prev = jnp.moveaxis(prev, 0, 1)
    y_off = jnp.einsum('bclgn,bcghpn,bclgh->bclghp', cmat, prev, jnp.exp(a_cum))
    return (y_diag + y_off).reshape(b, l, SSD_GROUPS, SSD_HPG, SSD_HEAD_DIM)


def ssd_mixer(x, w_in, conv_w, conv_b, dt_bias, a_log, d_skip, norm_w, w_out):
    b, l, _ = x.shape
    zxbcdt = x @ w_in
    z = zxbcdt[..., :D_INNER]
    xbc = zxbcdt[..., D_INNER:D_INNER + SSD_CONV_DIM]
    dt = zxbcdt[..., D_INNER + SSD_CONV_DIM:]
    xbc = jax.nn.silu(causal_depthwise_conv(xbc, conv_w, conv_b)).astype(jnp.float32)
    gn = SSD_GROUPS * SSD_STATE
    xs = xbc[..., :D_INNER].reshape(b, l, SSD_GROUPS, SSD_HPG, SSD_HEAD_DIM)
    bmat = xbc[..., D_INNER:D_INNER + gn].reshape(b, l, SSD_GROUPS, SSD_STATE)
    cmat = xbc[..., D_INNER + gn:].reshape(b, l, SSD_GROUPS, SSD_STATE)
    dt = jax.nn.softplus(dt.astype(jnp.float32) + dt_bias.astype(jnp.float32))
    dt = dt.reshape(b, l, SSD_GROUPS, SSD_HPG)
    a = -jnp.exp(a_log.astype(jnp.float32)).reshape(SSD_GROUPS, SSD_HPG)
    y = ssd_chunked(xs, dt, a, bmat, cmat)
    y = y + d_skip.astype(jnp.float32).reshape(SSD_GROUPS, SSD_HPG)[:, :, None] * xs
    y = y.reshape(b, l, D_INNER) * jax.nn.silu(z.astype(jnp.float32))
    yg = y.reshape(b, l, SSD_GROUPS, D_INNER // SSD_GROUPS)
    yg = yg * lax.rsqrt(jnp.mean(yg * yg, axis=-1, keepdims=True) + RMS_EPS)
    y = yg.reshape(b, l, D_INNER) * norm_w.astype(jnp.float32)
    return y.astype(x.dtype) @ w_out


def moe_ffn(x, w_router, router_bias, w_in, w_out):
    b, l, d = x.shape
    t = b * l
    tk = t * TOP_K
    xf = x.reshape(t, d)
    scores = jax.nn.sigmoid((xf @ w_router).astype(jnp.float32))
    sel = scores + router_bias.astype(jnp.float32)
    grp = sel.reshape(t, N_EXPERT_GROUPS, EXPERTS_PER_GROUP)
    group_score = jnp.sum(lax.top_k(grp, 2)[0], axis=-1)
    g_idx = jnp.argmax(group_score, axis=-1).astype(jnp.int32)
    in_grp = jnp.take_along_axis(grp, g_idx[:, None, None], axis=1)[:, 0]
    _, local = lax.top_k(in_grp, TOP_K)
    expert_idx = g_idx[:, None] * EXPERTS_PER_GROUP + local.astype(jnp.int32)
    gate = jnp.take_along_axis(scores, expert_idx, axis=1)
    gate = gate / jnp.sum(gate, axis=-1, keepdims=True)
    flat_e = expert_idx.reshape(tk)
    flat_tok = jnp.arange(tk, dtype=jnp.int32) // TOP_K
    counts = jnp.zeros((N_EXPERTS,), jnp.int32).at[flat_e].add(1)
    padded = (counts + BLOCK_ROWS - 1) // BLOCK_ROWS * BLOCK_ROWS
    pad_end = jnp.cumsum(padded)
    pad_start = pad_end - padded
    raw_start = jnp.cumsum(counts) - counts
    order = jnp.argsort(flat_e)
    sorted_e = flat_e[order]
    dest_sorted = pad_start[sorted_e] + jnp.arange(tk, dtype=jnp.int32) - raw_start[sorted_e]
    dest = jnp.zeros((tk,), jnp.int32).at[order].set(dest_sorted)
    n_blocks = (tk + N_EXPERTS * (BLOCK_ROWS - 1) + BLOCK_ROWS - 1) // BLOCK_ROWS
    rows = n_blocks * BLOCK_ROWS
    x_disp = jnp.zeros((rows, d), x.dtype).at[dest].set(xf[flat_tok])
    block_start = jnp.arange(n_blocks, dtype=jnp.int32) * BLOCK_ROWS
    block_expert = jnp.minimum(
        jnp.sum(block_start[:, None] >= pad_end[None, :], axis=1), N_EXPERTS - 1).astype(jnp.int32)

    def expert_block(args):
        xb, e = args
        h = xb @ w_in[e]
        return (jax.nn.silu(h[:, :D_FF]) * h[:, D_FF:]) @ w_out[e]

    y_blocks = lax.map(expert_block, (x_disp.reshape(n_blocks, BLOCK_ROWS, d), block_expert))
    y_assign = y_blocks.reshape(rows, d)[dest].reshape(t, TOP_K, d)
    y = jnp.einsum('tkd,tk->td', y_assign, gate.astype(x.dtype))
    return y.reshape(b, l, d)


def setup_inputs(seed: int = 0) -> dict:
    key = jax.random.key(seed)
    ks = jax.random.split(key, 24)
    f32 = jnp.float32
    nrm = lambda k, shape, s: jax.random.normal(k, shape, f32) * s
    x = jax.random.normal(ks[0], (BATCH, SEQ, D_MODEL), f32)
    pool_w_in = nrm(ks[1], (N_POOL_LAYERS, D_MODEL, D_MODEL), D_MODEL ** -0.5)
    pool_w_group = nrm(ks[2], (N_POOL_LAYERS, POOL_GROUPS, POOL_GC, POOL_GC), POOL_GC ** -0.5)
    pool_scale = 1.0 + nrm(ks[3], (N_POOL_LAYERS, D_MODEL), 0.1)
    pool_w_out = nrm(ks[4], (N_POOL_LAYERS, D_MODEL, D_MODEL), DEEPNORM_BETA * D_MODEL ** -0.5)
    ssd_w_in = nrm(ks[5], (N_SSD_LAYERS, D_MODEL, SSD_IN_DIM), D_MODEL ** -0.5)
    ssd_conv_w = nrm(ks[6], (N_SSD_LAYERS, SSD_CONV, SSD_CONV_DIM), SSD_CONV ** -0.5)
    ssd_conv_b = nrm(ks[7], (N_SSD_LAYERS, SSD_CONV_DIM), 0.02)
    dt0 = jnp.exp(jax.random.uniform(ks[8], (N_SSD_LAYERS, SSD_HEADS), f32,
                                     minval=math.log(1e-3), maxval=math.log(1e-1)))
    ssd_dt_bias = dt0 + jnp.log(-jnp.expm1(-dt0))
    ssd_a_log = jnp.log(jax.random.uniform(ks[9], (N_SSD_LAYERS, SSD_HEADS), f32, minval=1.0, maxval=16.0))
    ssd_d = 1.0 + nrm(ks[10], (N_SSD_LAYERS, SSD_HEADS), 0.1)
    ssd_norm_w = 1.0 + nrm(ks[11], (N_SSD_LAYERS, D_INNER), 0.1)
    ssd_w_out = nrm(ks[12], (N_SSD_LAYERS, D_INNER, D_MODEL), DEEPNORM_BETA * D_INNER ** -0.5)
    moe_w_router = nrm(ks[13], (D_MODEL, N_EXPERTS), D_MODEL ** -0.5)
    moe_router_bias = nrm(ks[14], (N_EXPERTS,), 0.01)
    moe_w_in = nrm(ks[15], (DEPTH, N_EXPERTS, D_MODEL, 2 * D_FF), D_MODEL ** -0.5)
    moe_w_out = nrm(ks[16], (DEPTH, N_EXPERTS, D_FF, D_MODEL), DEEPNORM_BETA * D_FF ** -0.5)
    ln_mix_g = 1.0 + nrm(ks[17], (DEPTH, D_MODEL), 0.1)
    ln_mix_b = nrm(ks[18], (DEPTH, D_MODEL), 0.02)
    ln_ffn_g = 1.0 + nrm(ks[19], (DEPTH, D_MODEL), 0.1)
    ln_ffn_b = nrm(ks[20], (DEPTH, D_MODEL), 0.02)
    return {'x': x, 'pool_w_in': pool_w_in, 'pool_w_group': pool_w_group, 'pool_scale': pool_scale,
            'pool_w_out': pool_w_out, 'ssd_w_in': ssd_w_in, 'ssd_conv_w': ssd_conv_w,
            'ssd_conv_b': ssd_conv_b, 'ssd_dt_bias': ssd_dt_bias, 'ssd_a_log': ssd_a_log,
            'ssd_d': ssd_d, 'ssd_norm_w': ssd_norm_w, 'ssd_w_out': ssd_w_out,
            'moe_w_router': moe_w_router, 'moe_router_bias': moe_router_bias,
            'moe_w_in': moe_w_in, 'moe_w_out': moe_w_out, 'ln_mix_g': ln_mix_g,
            'ln_mix_b': ln_mix_b, 'ln_ffn_g': ln_ffn_g, 'ln_ffn_b': ln_ffn_b}


def reference(x, pool_w_in, pool_w_group, pool_scale, pool_w_out, ssd_w_in, ssd_conv_w,
              ssd_conv_b, ssd_dt_bias, ssd_a_log, ssd_d, ssd_norm_w, ssd_w_out,
              moe_w_router, moe_router_bias, moe_w_in, moe_w_out,
              ln_mix_g, ln_mix_b, ln_ffn_g, ln_ffn_b):
    for i in range(DEPTH):
        j = i // N_MIXERS
        if i % N_MIXERS == 0:
            h = pool_mixer(x, pool_w_in[j], pool_w_group[j], pool_scale[j], pool_w_out[j])
        else:
            h = ssd_mixer(x, ssd_w_in[j], ssd_conv_w[j], ssd_conv_b[j], ssd_dt_bias[j],
                          ssd_a_log[j], ssd_d[j], ssd_norm_w[j], ssd_w_out[j])
        x = layer_norm(DEEPNORM_ALPHA * x + h, ln_mix_g[i], ln_mix_b[i])
        h = moe_ffn(x, moe_w_router, moe_router_bias, moe_w_in[i], moe_w_out[i])
        x = layer_norm(DEEPNORM_ALPHA * x + h, ln_ffn_g[i], ln_ffn_b[i])
    return x
```

```python
import functools

import jax
import jax.numpy as jnp
from jax import lax
from jax.experimental import pallas as pl
from jax.experimental.pallas import tpu as pltpu

F32 = jnp.float32
BF16 = jnp.bfloat16
I32 = jnp.int32

D_MODEL = 4096
DEPTH = 2
N_MIXERS = 2
DEEPNORM_ALPHA = (2 * DEPTH) ** 0.25
LN_EPS = 1e-5
POOL_WINDOWS = (2, 4, 8, 16)
POOL_GROUPS = len(POOL_WINDOWS)
POOL_GC = D_MODEL // POOL_GROUPS
D_INNER = 2 * D_MODEL
SSD_HEAD_DIM = 64
SSD_HEADS = D_INNER // SSD_HEAD_DIM
SSD_GROUPS = 8
SSD_HPG = SSD_HEADS // SSD_GROUPS
SSD_STATE = 128
SSD_CONV = 4
SSD_CHUNK = 128
SSD_GW = D_INNER // SSD_GROUPS
SSD_CONV_DIM = D_INNER + 2 * SSD_GROUPS * SSD_STATE
SSD_IN_DIM = D_INNER + SSD_CONV_DIM + SSD_HEADS
RMS_EPS = 1e-5
N_EXPERTS = 32
N_EXPERT_GROUPS = 8
EXPERTS_PER_GROUP = N_EXPERTS // N_EXPERT_GROUPS
TOP_K = 2
D_FF = 768
BLOCK_ROWS = 128

LANES = 128
SUBLANES = 8
VMEM_BYTES_V7X = 64 * 1024 * 1024
VMEM_CAP = VMEM_BYTES_V7X * 7 // 8

POOL_HALO = 16
CONV_HALO = SUBLANES
FF_TILE = 256
DISPATCH_WINDOW = 128


def _nbytes(shape, dtype):
    n = 1
    for s in shape:
        n *= s
    return n * jnp.dtype(dtype).itemsize


def _vmem_limit(pipelined, resident):
    est = 2 * sum(pipelined) + sum(resident)
    return int(min(VMEM_CAP, est + max(est // 4, 8 * 1024 * 1024)))


def _params(semantics, pipelined, resident=()):
    return pltpu.CompilerParams(dimension_semantics=semantics,
                                vmem_limit_bytes=_vmem_limit(pipelined, resident))


def _mm_kernel(a_ref, w_ref, o_ref, wbf_ref):
    @pl.when(pl.program_id(1) == 0)
    def _():
        wbf_ref[...] = w_ref[...].astype(BF16)

    o_ref[...] = jnp.dot(a_ref[...], wbf_ref[...], preferred_element_type=F32).astype(o_ref.dtype)


def _mm_scale_kernel(a_ref, w_ref, s_ref, o_ref, wbf_ref):
    @pl.when(pl.program_id(1) == 0)
    def _():
        wbf_ref[...] = w_ref[...].astype(BF16)

    acc = jnp.dot(a_ref[...], wbf_ref[...], preferred_element_type=F32)
    o_ref[...] = (acc * s_ref[...]).astype(o_ref.dtype)


def matmul_cols(a, w, col_off, n_cols, *, tm, tn, out_dtype, name):
    m_rows, k = a.shape
    assert w.shape[0] == k and m_rows % tm == 0 and n_cols % tn == 0 and col_off % tn == 0
    off = col_off // tn
    return pl.pallas_call(
        _mm_kernel,
        grid=(n_cols // tn, m_rows // tm),
        in_specs=[pl.BlockSpec((tm, k), lambda n, m: (m, 0)),
                  pl.BlockSpec((k, tn), lambda n, m: (0, n + off))],
        out_specs=pl.BlockSpec((tm, tn), lambda n, m: (m, n)),
        out_shape=jax.ShapeDtypeStruct((m_rows, n_cols), out_dtype),
        scratch_shapes=[pltpu.VMEM((k, tn), BF16)],
        compiler_params=_params(
            ("arbitrary", "arbitrary"),
            [_nbytes((tm, k), BF16), _nbytes((k, tn), F32), _nbytes((tm, tn), out_dtype)],
            [_nbytes((k, tn), BF16), _nbytes((tm, tn), F32)]),
        name=name,
    )(a, w)


def pool_group_matmul(p, w_group, scale, *, tm):
    m_rows = p.shape[0]
    gc = POOL_GC
    assert m_rows % tm == 0
    return pl.pallas_call(
        _mm_scale_kernel,
        grid=(POOL_GROUPS, m_rows // tm),
        in_specs=[pl.BlockSpec((tm, gc), lambda g, m: (m, g)),
                  pl.BlockSpec((None, gc, gc), lambda g, m: (g, 0, 0)),
                  pl.BlockSpec((1, gc), lambda g, m: (0, g))],
        out_specs=pl.BlockSpec((tm, gc), lambda g, m: (m, g)),
        out_shape=jax.ShapeDtypeStruct((m_rows, POOL_GROUPS * gc), BF16),
        scratch_shapes=[pltpu.VMEM((gc, gc), BF16)],
        compiler_params=_params(
            ("arbitrary", "arbitrary"),
            [_nbytes((tm, gc), BF16), _nbytes((gc, gc), F32), _nbytes((tm, gc), BF16)],
            [_nbytes((gc, gc), BF16), _nbytes((tm, gc), F32)]),
        name="pool_group_matmul",
    )(p, w_group, scale.reshape(1, -1))


def _pool_in_kernel(x_ref, w_ref, p_ref, wbf_ref, ubuf_ref, *, tm, tn):
    n = pl.program_id(0)
    m = pl.program_id(1)

    @pl.when(m == 0)
    def _():
        wbf_ref[...] = w_ref[...].astype(BF16)
        ubuf_ref[0:POOL_HALO, :] = jnp.zeros((POOL_HALO, tn), F32)

    u = jnp.dot(x_ref[...], wbf_ref[...], preferred_element_type=F32)
    ubuf_ref[POOL_HALO:POOL_HALO + tm, :] = u
    pos = (m * tm + 1 + lax.broadcasted_iota(I32, (tm, tn), 0)).astype(F32)
    group = (n * tn) // POOL_GC
    for gi, window in enumerate(POOL_WINDOWS):
        @pl.when(group == gi)
        def _(window=window):
            acc = u
            for k in range(1, window):
                acc = acc + ubuf_ref[POOL_HALO - k:POOL_HALO - k + tm, :]
            mean = acc / jnp.minimum(pos, float(window))
            p_ref[...] = (mean - u).astype(p_ref.dtype)

    ubuf_ref[0:POOL_HALO, :] = ubuf_ref[tm:tm + POOL_HALO, :]


def pool_in(xb, w_in, *, tm, tn):
    m_rows, k = xb.shape
    n_cols = w_in.shape[1]
    assert m_rows % tm == 0 and n_cols % tn == 0 and POOL_GC % tn == 0 and tm >= POOL_HALO
    return pl.pallas_call(
        functools.partial(_pool_in_kernel, tm=tm, tn=tn),
        grid=(n_cols // tn, m_rows // tm),
        in_specs=[pl.BlockSpec((tm, k), lambda n, m: (m, 0)),
                  pl.BlockSpec((k, tn), lambda n, m: (0, n))],
        out_specs=pl.BlockSpec((tm, tn), lambda n, m: (m, n)),
        out_shape=jax.ShapeDtypeStruct((m_rows, n_cols), BF16),
        scratch_shapes=[pltpu.VMEM((k, tn), BF16), pltpu.VMEM((tm + POOL_HALO, tn), F32)],
        compiler_params=_params(
            ("arbitrary", "arbitrary"),
            [_nbytes((tm, k), BF16), _nbytes((k, tn), F32), _nbytes((tm, tn), BF16)],
            [_nbytes((k, tn), BF16), 4 * _nbytes((tm + POOL_HALO, tn), F32)]),
        name="pool_in",
    )(xb, w_in)


def _layer_norm_rows(v, g, b):
    mu = jnp.mean(v, axis=-1, keepdims=True)
    vc = v - mu
    var = jnp.mean(vc * vc, axis=-1, keepdims=True)
    return vc * lax.rsqrt(var + LN_EPS) * g + b


def _ln_router_kernel(x_ref, h_ref, g_ref, b_ref, wr_ref, rb_ref,
                      xo_ref, xb_ref, e_ref, gate_ref, rank_ref, cnt_ref, base_ref, *, tm):
    i = pl.program_id(0)

    @pl.when(i == 0)
    def _():
        base_ref[...] = jnp.zeros_like(base_ref)

    y = _layer_norm_rows(DEEPNORM_ALPHA * x_ref[...] + h_ref[...], g_ref[...], b_ref[...])
    xo_ref[...] = y
    yb = y.astype(BF16)
    xb_ref[...] = yb

    ng, epg = N_EXPERT_GROUPS, EXPERTS_PER_GROUP
    logits = lax.dot_general(wr_ref[...], yb, (((1,), (1,)), ((), ())),
                             preferred_element_type=F32)
    scores = jax.nn.sigmoid(logits)
    sel = scores + rb_ref[...]
    sel_j = [sel[ng * j:ng * (j + 1), :] for j in range(epg)]
    sc_j = [scores[ng * j:ng * (j + 1), :] for j in range(epg)]
    gscore = None
    for j1 in range(epg):
        for j2 in range(j1 + 1, epg):
            s = sel_j[j1] + sel_j[j2]
            gscore = s if gscore is None else jnp.maximum(gscore, s)
    gmax = jnp.max(gscore, axis=0, keepdims=True)
    giota = lax.broadcasted_iota(I32, (ng, tm), 0)
    g_idx = jnp.min(jnp.where(gscore == gmax, giota, ng), axis=0, keepdims=True)
    in_g = giota == g_idx
    v = [jnp.sum(jnp.where(in_g, sel_j[j], 0.0), axis=0, keepdims=True) for j in range(epg)]
    s = [jnp.sum(jnp.where(in_g, sc_j[j], 0.0), axis=0, keepdims=True) for j in range(epg)]
    order = []
    for j in range(epg):
        r = jnp.zeros((1, tm), I32)
        for k in range(epg):
            if k == j:
                continue
            beats = (v[k] >= v[j]) if k < j else (v[k] > v[j])
            r = r + beats.astype(I32)
        order.append(r)
    eiota = lax.broadcasted_iota(I32, (N_EXPERTS, tm), 0)
    onehot = jnp.zeros((N_EXPERTS, tm), F32)
    loc, raw, rows = [], [], []
    for slot in range(TOP_K):
        lj = jnp.zeros((1, tm), I32)
        gs = jnp.zeros((1, tm), F32)
        for j in range(epg):
            hit = order[j] == slot
            lj = lj + jnp.where(hit, j, 0)
            gs = gs + jnp.where(hit, s[j], 0.0)
        row = lj * ng + g_idx
        onehot = onehot + (eiota == row).astype(F32)
        loc.append(lj)
        raw.append(gs)
        rows.append(row)
    denom = raw[0] + raw[1]
    srow = lax.broadcasted_iota(I32, (tm, tm), 0)
    scol = lax.broadcasted_iota(I32, (tm, tm), 1)
    before = (srow < scol).astype(BF16)
    prefix = jnp.dot(onehot.astype(BF16), before, preferred_element_type=F32) + base_ref[...]
    for slot in range(TOP_K):
        e_ref[slot:slot + 1, :] = g_idx * epg + loc[slot]
        gate_ref[slot:slot + 1, :] = raw[slot] / denom
        rk = jnp.sum(jnp.where(eiota == rows[slot], prefix, 0.0), axis=0, keepdims=True)
        rank_ref[slot:slot + 1, :] = rk.astype(I32)
    base_ref[...] = base_ref[...] + jnp.sum(onehot, axis=1, keepdims=True)
    cnt_ref[...] = jnp.broadcast_to(base_ref[...], cnt_ref.shape)


def ln_router(x, h, g, b, w_router, router_bias, *, tm):
    t, d = x.shape
    assert t % tm == 0
    ng, epg = N_EXPERT_GROUPS, EXPERTS_PER_GROUP
    wr = w_router.T.reshape(ng, epg, d).transpose(1, 0, 2).reshape(N_EXPERTS, d).astype(BF16)
    rb = router_bias.astype(F32).reshape(ng, epg).T.reshape(N_EXPERTS, 1)
    row = pl.BlockSpec((tm, d), lambda i: (i, 0))
    vec = pl.BlockSpec((1, d), lambda i: (0, 0))
    tok = pl.BlockSpec((TOP_K, tm), lambda i: (0, i))
    outs = pl.pallas_call(
        functools.partial(_ln_router_kernel, tm=tm),
        grid=(t // tm,),
        in_specs=[row, row, vec, vec,
                  pl.BlockSpec((N_EXPERTS, d), lambda i: (0, 0)),
                  pl.BlockSpec((N_EXPERTS, 1), lambda i: (0, 0))],
        out_specs=[row, row, tok, tok, tok, pl.BlockSpec((N_EXPERTS, LANES), lambda i: (0, 0))],
        out_shape=[jax.ShapeDtypeStruct((t, d), F32), jax.ShapeDtypeStruct((t, d), BF16),
                   jax.ShapeDtypeStruct((TOP_K, t), I32), jax.ShapeDtypeStruct((TOP_K, t), F32),
                   jax.ShapeDtypeStruct((TOP_K, t), I32),
                   jax.ShapeDtypeStruct((N_EXPERTS, LANES), F32)],
        scratch_shapes=[pltpu.VMEM((N_EXPERTS, 1), F32)],
        compiler_params=_params(
            ("arbitrary",),
            [3 * _nbytes((tm, d), F32), _nbytes((tm, d), BF16), _nbytes((N_EXPERTS, d), BF16)],
            [4 * _nbytes((tm, d), F32)]),
        name="ln_router",
    )(x, h, g.reshape(1, d), b.reshape(1, d), wr, rb)
    x_new, xb, e_idx, gate, rank, cnt = outs
    counts = cnt[:, 0].astype(I32).reshape(epg, ng).T.reshape(N_EXPERTS)
    return x_new, xb, e_idx, gate, rank, counts


def _dispatch_kernel(e_ref, rank_ref, pstart_ref, cnt_ref, x_hbm, xd_hbm, zbuf, sem, zsem, *, t):
    zbuf[...] = jnp.zeros_like(zbuf)

    def pad_copies(e, act):
        c = cnt_ref[e]
        pad = (BLOCK_ROWS - c % BLOCK_ROWS) % BLOCK_ROWS
        off = pstart_ref[e] + c

        def body(r, carry):
            cp = pltpu.make_async_copy(zbuf.at[0:1], xd_hbm.at[pl.ds(off + r, 1)], zsem)
            cp.start() if act == "start" else cp.wait()
            return carry
        lax.fori_loop(0, pad, body, 0)

    def row_copy(tok, slot):
        dst = pstart_ref[e_ref[slot * t + tok]] + rank_ref[slot * t + tok]
        return pltpu.make_async_copy(x_hbm.at[pl.ds(tok, 1)], xd_hbm.at[pl.ds(dst, 1)], sem)

    def each_token(w, act):
        def body(r, carry):
            for slot in range(TOP_K):
                cp = row_copy(w * DISPATCH_WINDOW + r, slot)
                cp.start() if act == "start" else cp.wait()
            return carry
        lax.fori_loop(0, DISPATCH_WINDOW, body, 0)

    def tail_copies(act):
        last = N_EXPERTS - 1
        end = pstart_ref[last] + (cnt_ref[last] + BLOCK_ROWS - 1) // BLOCK_ROWS * BLOCK_ROWS

        def body(blk, carry):
            off = pl.multiple_of(blk * BLOCK_ROWS, BLOCK_ROWS)
            cp = pltpu.make_async_copy(zbuf, xd_hbm.at[pl.ds(off, BLOCK_ROWS)], zsem)
            cp.start() if act == "start" else cp.wait()
            return carry
        lax.fori_loop(end // BLOCK_ROWS, xd_hbm.shape[0] // BLOCK_ROWS, body, 0)

    lax.fori_loop(0, N_EXPERTS, lambda e, c: (pad_copies(e, "start"), c)[1], 0)
    tail_copies("start")
    n_win = t // DISPATCH_WINDOW

    def window(w, carry):
        each_token(w, "start")

        @pl.when(w > 0)
        def _():
            each_token(w - 1, "wait")
        return carry

    lax.fori_loop(0, n_win, window, 0)
    each_token(n_win - 1, "wait")
    lax.fori_loop(0, N_EXPERTS, lambda e, c: (pad_copies(e, "wait"), c)[1], 0)
    tail_copies("wait")


def dispatch(x, e_flat, rank_flat, pad_start, counts, rows):
    t, d = x.shape
    assert t % DISPATCH_WINDOW == 0
    return pl.pallas_call(
        functools.partial(_dispatch_kernel, t=t),
        grid_spec=pltpu.PrefetchScalarGridSpec(
            num_scalar_prefetch=4,
            grid=(1,),
            in_specs=[pl.BlockSpec(memory_space=pl.ANY)],
            out_specs=pl.BlockSpec(memory_space=pl.ANY),
            scratch_shapes=[pltpu.VMEM((BLOCK_ROWS, d), x.dtype),
                            pltpu.SemaphoreType.DMA(()), pltpu.SemaphoreType.DMA(())]),
        out_shape=jax.ShapeDtypeStruct((rows, d), x.dtype),
        compiler_params=pltpu.CompilerParams(dimension_semantics=("arbitrary",),
                                             has_side_effects=True),
        name="moe_dispatch",
    )(e_flat, rank_flat, pad_start, counts, x)


def _fresh_expert(be_ref, b):
    return (b == 0) | (be_ref[b] != be_ref[jnp.maximum(b - 1, 0)])


def _expert_in_kernel(be_ref, nu_ref, x_ref, w1_ref, w2_ref, o_ref, wbf_ref):
    b = pl.program_id(1)

    @pl.when(_fresh_expert(be_ref, b))
    def _():
        wbf_ref[:, 0:FF_TILE] = w1_ref[...].astype(BF16)
        wbf_ref[:, FF_TILE:2 * FF_TILE] = w2_ref[...].astype(BF16)

    @pl.when(b < nu_ref[0])
    def _():
        h = jnp.dot(x_ref[...].astype(BF16), wbf_ref[...], preferred_element_type=F32)
        h1 = h[:, 0:FF_TILE]
        h2 = h[:, FF_TILE:2 * FF_TILE]
        o_ref[...] = (h1 * jax.nn.sigmoid(h1) * h2).astype(o_ref.dtype)

    @pl.when(b >= nu_ref[0])
    def _():
        o_ref[...] = jnp.zeros_like(o_ref)


def _expert_out_kernel(be_ref, nu_ref, a_ref, w_ref, o_ref, wbf_ref):
    b = pl.program_id(0)

    @pl.when(_fresh_expert(be_ref, b))
    def _():
        wbf_ref[...] = w_ref[...].astype(BF16)

    @pl.when(b < nu_ref[0])
    def _():
        o_ref[...] = jnp.dot(a_ref[...], wbf_ref[...], preferred_element_type=F32)

    @pl.when(b >= nu_ref[0])
    def _():
        o_ref[...] = jnp.zeros_like(o_ref)


def expert_ffn(x_disp, block_expert, n_used, moe_w_in, moe_w_out, layer):
    rows, d = x_disp.shape
    n_blocks = rows // BLOCK_ROWS
    n_ff = D_FF // FF_TILE
    act = pl.pallas_call(
        _expert_in_kernel,
        grid_spec=pltpu.PrefetchScalarGridSpec(
            num_scalar_prefetch=2,
            grid=(n_ff, n_blocks),
            in_specs=[pl.BlockSpec((BLOCK_ROWS, d), lambda j, b, be, nu: (b, 0)),
                      pl.BlockSpec((None, None, d, FF_TILE), lambda j, b, be, nu: (layer, be[b], 0, j)),
                      pl.BlockSpec((None, None, d, FF_TILE),
                                   lambda j, b, be, nu: (layer, be[b], 0, n_ff + j))],
            out_specs=pl.BlockSpec((BLOCK_ROWS, FF_TILE), lambda j, b, be, nu: (b, j)),
            scratch_shapes=[pltpu.VMEM((d, 2 * FF_TILE), BF16)]),
        out_shape=jax.ShapeDtypeStruct((rows, D_FF), BF16),
        compiler_params=_params(
            ("arbitrary", "arbitrary"),
            [_nbytes((BLOCK_ROWS, d), F32), 2 * _nbytes((d, FF_TILE), F32),
             _nbytes((BLOCK_ROWS, FF_TILE), BF16)],
            [_nbytes((d, 2 * FF_TILE), BF16), _nbytes((BLOCK_ROWS, d), BF16)]),
        name="expert_in",
    )(block_expert, n_used, x_disp, moe_w_in, moe_w_in)
    return pl.pallas_call(
        _expert_out_kernel,
        grid_spec=pltpu.PrefetchScalarGridSpec(
            num_scalar_prefetch=2,
            grid=(n_blocks,),
            in_specs=[pl.BlockSpec((BLOCK_ROWS, D_FF), lambda b, be, nu: (b, 0)),
                      pl.BlockSpec((None, None, D_FF, d), lambda b, be, nu: (layer, be[b], 0, 0))],
            out_specs=pl.BlockSpec((BLOCK_ROWS, d), lambda b, be, nu: (b, 0)),
            scratch_shapes=[pltpu.VMEM((D_FF, d), BF16)]),
        out_shape=jax.ShapeDtypeStruct((rows, d), F32),
        compiler_params=_params(
            ("arbitrary",),
            [_nbytes((BLOCK_ROWS, D_FF), BF16), _nbytes((D_FF, d), F32), _nbytes((BLOCK_ROWS, d), F32)],
            [_nbytes((D_FF, d), BF16)]),
        name="expert_out",
    )(block_expert, n_used, act, moe_w_out)


def _combine_ln_kernel(dest_ref, x_ref, gate_ref, g_ref, b_ref, y_hbm, *rest, tm, t, with_bf16):
    if with_bf16:
        xo_ref, xb_ref, ybuf, sem = rest
    else:
        xo_ref, ybuf, sem = rest
        xb_ref = None
    i = pl.program_id(0)
    n = pl.num_programs(0)

    def row_copy(step, r, k):
        d = dest_ref[k * t + step * tm + r]
        buf = step % 2
        return pltpu.make_async_copy(y_hbm.at[pl.ds(d, 1)], ybuf.at[buf, k, pl.ds(r, 1)], sem.at[buf])

    def each_row(step, act):
        def body(r, carry):
            for k in range(TOP_K):
                cp = row_copy(step, r, k)
                cp.start() if act == "start" else cp.wait()
            return carry
        lax.fori_loop(0, tm, body, 0)

    @pl.when(i == 0)
    def _():
        each_row(i, "start")

    @pl.when(i + 1 < n)
    def _():
        each_row(i + 1, "start")

    each_row(i, "wait")
    buf = i % 2
    gate = gate_ref[...]
    h = gate[:, 0:1] * ybuf[buf, 0] + gate[:, 1:2] * ybuf[buf, 1]
    y = _layer_norm_rows(DEEPNORM_ALPHA * x_ref[...] + h, g_ref[...], b_ref[...])
    xo_ref[...] = y
    if with_bf16:
        xb_ref[...] = y.astype(BF16)


def combine_ln(x, y_disp, dest_flat, gate_t, g, b, *, tm, with_bf16):
    t, d = x.shape
    assert t % tm == 0
    row = pl.BlockSpec((tm, d), lambda i, dest: (i, 0))
    vec = pl.BlockSpec((1, d), lambda i, dest: (0, 0))
    out_specs = [row, row] if with_bf16 else [row]
    out_shape = [jax.ShapeDtypeStruct((t, d), F32)]
    if with_bf16:
        out_shape.append(jax.ShapeDtypeStruct((t, d), BF16))
    outs = pl.pallas_call(
        functools.partial(_combine_ln_kernel, tm=tm, t=t, with_bf16=with_bf16),
        grid_spec=pltpu.PrefetchScalarGridSpec(
            num_scalar_prefetch=1,
            grid=(t // tm,),
            in_specs=[row, pl.BlockSpec((tm, TOP_K), lambda i, dest: (i, 0)), vec, vec,
                      pl.BlockSpec(memory_space=pl.ANY)],
            out_specs=out_specs,
            scratch_shapes=[pltpu.VMEM((2, TOP_K, tm, d), F32), pltpu.SemaphoreType.DMA((2,))]),
        out_shape=out_shape,
        compiler_params=_params(
            ("arbitrary",),
            [2 * _nbytes((tm, d), F32), _nbytes((tm, d), BF16)],
            [_nbytes((2, TOP_K, tm, d), F32), 4 * _nbytes((tm, d), F32)]),
        name="moe_combine_ln",
    )(dest_flat, x, gate_t, g.reshape(1, d), b.reshape(1, d), y_disp)
    return (outs[0], outs[1]) if with_bf16 else (outs[0], None)


def moe_ffn_ln(x, xb_unused, e_idx, gate, rank, counts, moe_w_in, moe_w_out, layer, g, b, *,
               with_bf16, tm):
    del xb_unused
    t, d = x.shape
    tk = t * TOP_K
    padded = (counts + BLOCK_ROWS - 1) // BLOCK_ROWS * BLOCK_ROWS
    pad_end = jnp.cumsum(padded).astype(I32)
    pad_start = pad_end - padded
    n_blocks = (tk + N_EXPERTS * (BLOCK_ROWS - 1) + BLOCK_ROWS - 1) // BLOCK_ROWS
    rows = n_blocks * BLOCK_ROWS
    block_start = jnp.arange(n_blocks, dtype=I32) * BLOCK_ROWS
    block_expert = jnp.minimum(
        jnp.sum(block_start[:, None] >= pad_end[None, :], axis=1), N_EXPERTS - 1).astype(I32)
    n_used = (pad_end[-1:] // BLOCK_ROWS).astype(I32)
    e_flat = e_idx.reshape(tk)
    rank_flat = rank.reshape(tk)
    x_disp = dispatch(x, e_flat, rank_flat, pad_start, counts, rows)
    y_disp = expert_ffn(x_disp, block_expert, n_used, moe_w_in, moe_w_out, layer)
    dest_flat = pad_start[e_flat] + rank_flat
    return combine_ln(x, y_disp, dest_flat, gate.T, g, b, tm=tm, with_bf16=with_bf16)


def _softplus(x):
    return jnp.maximum(x, 0.0) + jnp.log1p(jnp.exp(-jnp.abs(x)))


def _silu(x):
    return x * jax.nn.sigmoid(x)


def _ssd_kernel(xs_ref, b_ref, c_ref, z_ref, dt_ref, dtT_ref,
                cwx_ref, cwb_ref, cwc_ref, cbx_ref, cbb_ref, cbc_ref,
                dtb_ref, dtbT_ref, alog_ref, alogT_ref, dsk_ref, nw_ref,
                y_ref, state_ref, hx_ref, hb_ref, hc_ref):
    q = SSD_CHUNK
    hi = lax.Precision.HIGHEST

    @pl.when(pl.program_id(1) == 0)
    def _():
        state_ref[...] = jnp.zeros_like(state_ref)
        hx_ref[...] = jnp.zeros_like(hx_ref)
        hb_ref[...] = jnp.zeros_like(hb_ref)
        hc_ref[...] = jnp.zeros_like(hc_ref)

    def conv_silu(raw_ref, halo_ref, w_ref, bias_ref):
        raw = raw_ref[...]
        ext = jnp.concatenate([halo_ref[...], raw], axis=0)
        w = w_ref[...]
        acc = bias_ref[...] + w[SSD_CONV - 1:SSD_CONV, :] * raw
        for k in range(SSD_CONV - 1):
            lo = CONV_HALO - (SSD_CONV - 1) + k
            acc = acc + w[k:k + 1, :] * ext[lo:lo + q, :]
        halo_ref[...] = raw[q - CONV_HALO:q, :]
        return _silu(acc)

    xs = conv_silu(xs_ref, hx_ref, cwx_ref, cbx_ref)
    bm = conv_silu(b_ref, hb_ref, cwb_ref, cbb_ref)
    cm = conv_silu(c_ref, hc_ref, cwc_ref, cbc_ref)

    dt = _softplus(dt_ref[...] + dtb_ref[...])
    dt_t = _softplus(dtT_ref[...] + dtbT_ref[...])
    a = -jnp.exp(alog_ref[...])
    a_t = -jnp.exp(alogT_ref[...])
    row = lax.broadcasted_iota(I32, (q, q), 0)
    col = lax.broadcasted_iota(I32, (q, q), 1)
    causal = row >= col
    a_cum = jnp.dot(causal.astype(F32), dt * a, precision=hi, preferred_element_type=F32)
    a_cum_t = jnp.dot(dt_t * a_t, (row <= col).astype(F32), precision=hi,
                      preferred_element_type=F32)
    a_last = a_cum[q - 1:q, :]
    hrow = lax.broadcasted_iota(I32, (SSD_HPG, SSD_GW), 0)
    hcol = lax.broadcasted_iota(I32, (SSD_HPG, SSD_GW), 1)
    expand = (hcol // SSD_HEAD_DIM == hrow).astype(F32)
    decay_in = jnp.dot(jnp.exp(a_cum), expand, precision=hi, preferred_element_type=F32)
    w_state = jnp.dot(jnp.exp(a_last - a_cum) * dt, expand, precision=hi,
                      preferred_element_type=F32)
    chunk_decay = decay_in[q - 1:q, :]

    cmb = cm.astype(BF16)
    bmb = bm.astype(BF16)
    cb = lax.dot_general(cmb, bmb, (((1,), (1,)), ((), ())), preferred_element_type=F32)
    prev = state_ref[...]
    y_off = jnp.dot(cmb, prev.astype(BF16), preferred_element_type=F32) * decay_in
    xw = (w_state * xs).astype(BF16)
    state_ref[...] = chunk_decay * prev + jnp.dot(bm.T.astype(BF16), xw, preferred_element_type=F32)

    lane = lax.broadcasted_iota(I32, (q, LANES), 1)
    heads_per_tile = LANES // SSD_HEAD_DIM
    ys = []
    for tile in range(SSD_GW // LANES):
        ms = []
        for hh in range(heads_per_tile):
            h = tile * heads_per_tile + hh
            seg = a_cum[:, h:h + 1] - a_cum_t[h:h + 1, :]
            decay = jnp.where(causal, jnp.exp(seg), 0.0)
            ms.append((cb * decay * dt_t[h:h + 1, :]).astype(BF16))
        x_tile = xs[:, tile * LANES:(tile + 1) * LANES]
        rhs = jnp.concatenate(
            [jnp.where(lane // SSD_HEAD_DIM == hh, x_tile, 0.0).astype(BF16)
             for hh in range(heads_per_tile)], axis=0)
        ys.append(jnp.dot(jnp.concatenate(ms, axis=1), rhs, preferred_element_type=F32))
    y = jnp.concatenate(ys, axis=1) + y_off + dsk_ref[...] * xs
    y = y * _silu(z_ref[...])
    y = y * lax.rsqrt(jnp.mean(y * y, axis=-1, keepdims=True) + RMS_EPS) * nw_ref[...]
    y_ref[...] = y.astype(y_ref.dtype)


def ssd_core(z, xbc, dt_raw, conv_w, conv_b, dt_bias, a_log, d_skip, norm_w):
    t = z.shape[0]
    q, gw, ns, hpg, ng = SSD_CHUNK, SSD_GW, SSD_STATE, SSD_HPG, SSD_GROUPS
    assert t % q == 0
    b_blk = D_INNER // ns
    c_blk = b_blk + ng
    dt_g = dt_raw.reshape(t, ng, hpg).transpose(1, 0, 2)
    dt_gt = dt_raw.T.reshape(ng, hpg, t)
    dtb = dt_bias.astype(F32).reshape(ng, 1, hpg)
    dtb_t = dt_bias.astype(F32).reshape(ng, hpg, 1)
    alog = a_log.astype(F32).reshape(ng, 1, hpg)
    alog_t = a_log.astype(F32).reshape(ng, hpg, 1)
    dsk = jnp.repeat(d_skip.astype(F32), SSD_HEAD_DIM).reshape(1, D_INNER)
    cw = conv_w.astype(F32)
    cbias = conv_b.astype(F32).reshape(1, SSD_CONV_DIM)
    nw = norm_w.astype(F32).reshape(1, D_INNER)

    def cspec(rows, width, blk):
        return pl.BlockSpec((rows, width), lambda g, c: (0, blk(g)))

    in_specs = [
        pl.BlockSpec((q, gw), lambda g, c: (c, g)),
        pl.BlockSpec((q, ns), lambda g, c: (c, b_blk + g)),
        pl.BlockSpec((q, ns), lambda g, c: (c, c_blk + g)),
        pl.BlockSpec((q, gw), lambda g, c: (c, g)),
        pl.BlockSpec((None, q, hpg), lambda g, c: (g, c, 0)),
        pl.BlockSpec((None, hpg, q), lambda g, c: (g, 0, c)),
        cspec(SSD_CONV, gw, lambda g: g), cspec(SSD_CONV, ns, lambda g: b_blk + g),
        cspec(SSD_CONV, ns, lambda g: c_blk + g),
        cspec(1, gw, lambda g: g), cspec(1, ns, lambda g: b_blk + g), cspec(1, ns, lambda g: c_blk + g),
        pl.BlockSpec((None, 1, hpg), lambda g, c: (g, 0, 0)),
        pl.BlockSpec((None, hpg, 1), lambda g, c: (g, 0, 0)),
        pl.BlockSpec((None, 1, hpg), lambda g, c: (g, 0, 0)),
        pl.BlockSpec((None, hpg, 1), lambda g, c: (g, 0, 0)),
        cspec(1, gw, lambda g: g), cspec(1, gw, lambda g: g),
    ]
    return pl.pallas_call(
        _ssd_kernel,
        grid=(ng, t // q),
        in_specs=in_specs,
        out_specs=pl.BlockSpec((q, gw), lambda g, c: (c, g)),
        out_shape=jax.ShapeDtypeStruct((t, D_INNER), BF16),
        scratch_shapes=[pltpu.VMEM((ns, gw), F32), pltpu.VMEM((CONV_HALO, gw), F32),
                        pltpu.VMEM((CONV_HALO, ns), F32), pltpu.VMEM((CONV_HALO, ns), F32)],
        compiler_params=_params(
            ("arbitrary", "arbitrary"),
            [2 * _nbytes((q, gw), F32), 2 * _nbytes((q, ns), F32), _nbytes((q, gw), BF16)],
            [_nbytes((ns, gw), F32), 16 * _nbytes((q, gw), F32)]),
        name="ssd_core",
    )(xbc, xbc, xbc, z, dt_g, dt_gt, cw, cw, cw, cbias, cbias, cbias,
      dtb, dtb_t, alog, alog_t, dsk, nw)


MM_TM = 512
MM_TN = 512
LN_TM = 256
COMBINE_TM = 128


def pool_mixer(xb, w_in, w_group, scale, w_out):
    p = pool_in(xb, w_in, tm=MM_TM, tn=MM_TN)
    mixed = pool_group_matmul(p, w_group, scale, tm=MM_TM)
    return matmul_cols(mixed, w_out, 0, D_MODEL, tm=MM_TM, tn=MM_TN, out_dtype=F32, name="pool_out")


def ssd_mixer(xb, w_in, conv_w, conv_b, dt_bias, a_log, d_skip, norm_w, w_out):
    z = matmul_cols(xb, w_in, 0, D_INNER, tm=MM_TM, tn=MM_TN, out_dtype=F32, name="ssd_in_z")
    xbc = matmul_cols(xb, w_in, D_INNER, SSD_CONV_DIM, tm=MM_TM, tn=MM_TN, out_dtype=F32,
                      name="ssd_in_xbc")
    dt_raw = matmul_cols(xb, w_in, D_INNER + SSD_CONV_DIM, SSD_HEADS, tm=MM_TM, tn=SSD_HEADS,
                         out_dtype=F32, name="ssd_in_dt")
    y = ssd_core(z, xbc, dt_raw, conv_w, conv_b, dt_bias, a_log, d_skip, norm_w)
    return matmul_cols(y, w_out, 0, D_MODEL, tm=MM_TM, tn=MM_TN // 2, out_dtype=F32, name="ssd_out")


def kernel(x, pool_w_in, pool_w_group, pool_scale, pool_w_out, ssd_w_in, ssd_conv_w, ssd_conv_b,
           ssd_dt_bias, ssd_a_log, ssd_d, ssd_norm_w, ssd_w_out, moe_w_router, moe_router_bias,
           moe_w_in, moe_w_out, ln_mix_g, ln_mix_b, ln_ffn_g, ln_ffn_b):
    bsz, seq, d = x.shape
    x = x.reshape(bsz * seq, d)
    assert bsz == 1
    xb = x.astype(BF16)
    for i in range(DEPTH):
        j = i // N_MIXERS
        if i % N_MIXERS == 0:
            h = pool_mixer(xb, pool_w_in[j], pool_w_group[j], pool_scale[j], pool_w_out[j])
        else:
            h = ssd_mixer(xb, ssd_w_in[j], ssd_conv_w[j], ssd_conv_b[j], ssd_dt_bias[j],
                          ssd_a_log[j], ssd_d[j], ssd_norm_w[j], ssd_w_out[j])
        x, xb, e_idx, gate, rank, counts = ln_router(
            x, h, ln_mix_g[i], ln_mix_b[i], moe_w_router, moe_router_bias, tm=LN_TM)
        x, xb = moe_ffn_ln(x, xb, e_idx, gate, rank, counts, moe_w_in, moe_w_out, i,
                           ln_ffn_g[i], ln_ffn_b[i], with_bf16=(i + 1 < DEPTH), tm=COMBINE_TM)
    return x.reshape(bsz, seq, d)
```

```python
import functools

import jax
import jax.numpy as jnp
from jax import lax
from jax.experimental import pallas as pl
from jax.experimental.pallas import tpu as pltpu

F32 = jnp.float32
BF16 = jnp.bfloat16
I32 = jnp.int32

D_MODEL = 4096
DEPTH = 2
N_MIXERS = 2
DEEPNORM_ALPHA = (2 * DEPTH) ** 0.25
LN_EPS = 1e-5
POOL_WINDOWS = (2, 4, 8, 16)
POOL_GROUPS = len(POOL_WINDOWS)
POOL_GC = D_MODEL // POOL_GROUPS
D_INNER = 2 * D_MODEL
SSD_HEAD_DIM = 64
SSD_HEADS = D_INNER // SSD_HEAD_DIM
SSD_GROUPS = 8
SSD_HPG = SSD_HEADS // SSD_GROUPS
SSD_STATE = 128
SSD_CONV = 4
SSD_CHUNK = 128
SSD_GW = D_INNER // SSD_GROUPS
SSD_CONV_DIM = D_INNER + 2 * SSD_GROUPS * SSD_STATE
SSD_IN_DIM = D_INNER + SSD_CONV_DIM + SSD_HEADS
RMS_EPS = 1e-5
N_EXPERTS = 32
N_EXPERT_GROUPS = 8
EXPERTS_PER_GROUP = N_EXPERTS // N_EXPERT_GROUPS
TOP_K = 2
D_FF = 768
BLOCK_ROWS = 128

LANES = 128
SUBLANES = 8
VMEM_BYTES_V7X = 64 * 1024 * 1024
VMEM_CAP = VMEM_BYTES_V7X * 7 // 8

POOL_HALO = 16
CONV_HALO = SUBLANES
EXPERT_CHUNK = 2 * BLOCK_ROWS


def _nbytes(shape, dtype):
    n = 1
    for s in shape:
        n *= s
    return n * jnp.dtype(dtype).itemsize


def _vmem_limit(pipelined, resident):
    est = 2 * sum(pipelined) + sum(resident)
    return int(min(VMEM_CAP, est + max(est // 4, 8 * 1024 * 1024)))


def _params(semantics, pipelined, resident=()):
    return pltpu.CompilerParams(dimension_semantics=semantics,
                                vmem_limit_bytes=_vmem_limit(pipelined, resident))


def _mm_kernel(a_ref, w_ref, o_ref, wbf_ref):
    @pl.when(pl.program_id(1) == 0)
    def _():
        wbf_ref[...] = w_ref[...].astype(BF16)

    o_ref[...] = jnp.dot(a_ref[...], wbf_ref[...], preferred_element_type=F32).astype(o_ref.dtype)


def _mm_scale_kernel(a_ref, w_ref, s_ref, o_ref, wbf_ref):
    @pl.when(pl.program_id(1) == 0)
    def _():
        wbf_ref[...] = w_ref[...].astype(BF16)

    acc = jnp.dot(a_ref[...], wbf_ref[...], preferred_element_type=F32)
    o_ref[...] = (acc * s_ref[...]).astype(o_ref.dtype)


def matmul_cols(a, w, col_off, n_cols, *, tm, tn, out_dtype, name):
    m_rows, k = a.shape
    assert w.shape[0] == k and m_rows % tm == 0 and n_cols % tn == 0 and col_off % tn == 0
    off = col_off // tn
    return pl.pallas_call(
        _mm_kernel,
        grid=(n_cols // tn, m_rows // tm),
        in_specs=[pl.BlockSpec((tm, k), lambda n, m: (m, 0)),
                  pl.BlockSpec((k, tn), lambda n, m: (0, n + off))],
        out_specs=pl.BlockSpec((tm, tn), lambda n, m: (m, n)),
        out_shape=jax.ShapeDtypeStruct((m_rows, n_cols), out_dtype),
        scratch_shapes=[pltpu.VMEM((k, tn), BF16)],
        compiler_params=_params(
            ("arbitrary", "arbitrary"),
            [_nbytes((tm, k), BF16), _nbytes((k, tn), F32), _nbytes((tm, tn), out_dtype)],
            [_nbytes((k, tn), BF16), _nbytes((tm, tn), F32)]),
        name=name,
    )(a, w)


def pool_group_matmul(p, w_group, scale, *, tm):
    m_rows = p.shape[0]
    gc = POOL_GC
    assert m_rows % tm == 0
    return pl.pallas_call(
        _mm_scale_kernel,
        grid=(POOL_GROUPS, m_rows // tm),
        in_specs=[pl.BlockSpec((tm, gc), lambda g, m: (m, g)),
                  pl.BlockSpec((None, gc, gc), lambda g, m: (g, 0, 0)),
                  pl.BlockSpec((1, gc), lambda g, m: (0, g))],
        out_specs=pl.BlockSpec((tm, gc), lambda g, m: (m, g)),
        out_shape=jax.ShapeDtypeStruct((m_rows, POOL_GROUPS * gc), BF16),
        scratch_shapes=[pltpu.VMEM((gc, gc), BF16)],
        compiler_params=_params(
            ("arbitrary", "arbitrary"),
            [_nbytes((tm, gc), BF16), _nbytes((gc, gc), F32), _nbytes((tm, gc), BF16)],
            [_nbytes((gc, gc), BF16), _nbytes((tm, gc), F32)]),
        name="pool_group_matmul",
    )(p, w_group, scale.reshape(1, -1))


def _pool_in_kernel(x_ref, w_ref, p_ref, wbf_ref, ubuf_ref, *, tm, tn):
    n = pl.program_id(0)
    m = pl.program_id(1)

    @pl.when(m == 0)
    def _():
        wbf_ref[...] = w_ref[...].astype(BF16)
        ubuf_ref[0:POOL_HALO, :] = jnp.zeros((POOL_HALO, tn), F32)

    u = jnp.dot(x_ref[...], wbf_ref[...], preferred_element_type=F32)
    ubuf_ref[POOL_HALO:POOL_HALO + tm, :] = u
    pos = (m * tm + 1 + lax.broadcasted_iota(I32, (tm, tn), 0)).astype(F32)
    group = (n * tn) // POOL_GC
    for gi, window in enumerate(POOL_WINDOWS):
        @pl.when(group == gi)
        def _(window=window):
            acc = u
            for k in range(1, window):
                acc = acc + ubuf_ref[POOL_HALO - k:POOL_HALO - k + tm, :]
            mean = acc / jnp.minimum(pos, float(window))
            p_ref[...] = (mean - u).astype(p_ref.dtype)

    ubuf_ref[0:POOL_HALO, :] = ubuf_ref[tm:tm + POOL_HALO, :]


def pool_in(xb, w_in, *, tm, tn):
    m_rows, k = xb.shape
    n_cols = w_in.shape[1]
    assert m_rows % tm == 0 and n_cols % tn == 0 and POOL_GC % tn == 0 and tm >= POOL_HALO
    return pl.pallas_call(
        functools.partial(_pool_in_kernel, tm=tm, tn=tn),
        grid=(n_cols // tn, m_rows // tm),
        in_specs=[pl.BlockSpec((tm, k), lambda n, m: (m, 0)),
                  pl.BlockSpec((k, tn), lambda n, m: (0, n))],
        out_specs=pl.BlockSpec((tm, tn), lambda n, m: (m, n)),
        out_shape=jax.ShapeDtypeStruct((m_rows, n_cols), BF16),
        scratch_shapes=[pltpu.VMEM((k, tn), BF16), pltpu.VMEM((tm + POOL_HALO, tn), F32)],
        compiler_params=_params(
            ("arbitrary", "arbitrary"),
            [_nbytes((tm, k), BF16), _nbytes((k, tn), F32), _nbytes((tm, tn), BF16)],
            [_nbytes((k, tn), BF16), 4 * _nbytes((tm + POOL_HALO, tn), F32)]),
        name="pool_in",
    )(xb, w_in)


def _layer_norm_rows(v, g, b):
    mu = jnp.mean(v, axis=-1, keepdims=True)
    vc = v - mu
    var = jnp.mean(vc * vc, axis=-1, keepdims=True)
    return vc * lax.rsqrt(var + LN_EPS) * g + b


def _ln_router_kernel(x_ref, h_ref, g_ref, b_ref, wr_ref, rb_ref,
                      xo_ref, xb_ref, e_ref, gate_ref, rank_ref, cnt_ref, base_ref, *, tm):
    i = pl.program_id(0)

    @pl.when(i == 0)
    def _():
        base_ref[...] = jnp.zeros_like(base_ref)

    y = _layer_norm_rows(DEEPNORM_ALPHA * x_ref[...] + h_ref[...], g_ref[...], b_ref[...])
    xo_ref[...] = y
    yb = y.astype(BF16)
    xb_ref[...] = yb

    ng, epg = N_EXPERT_GROUPS, EXPERTS_PER_GROUP
    logits = lax.dot_general(wr_ref[...], yb, (((1,), (1,)), ((), ())),
                             preferred_element_type=F32)
    scores = jax.nn.sigmoid(logits)
    sel = scores + rb_ref[...]
    sel_j = [sel[ng * j:ng * (j + 1), :] for j in range(epg)]
    sc_j = [scores[ng * j:ng * (j + 1), :] for j in range(epg)]
    gscore = None
    for j1 in range(epg):
        for j2 in range(j1 + 1, epg):
            s = sel_j[j1] + sel_j[j2]
            gscore = s if gscore is None else jnp.maximum(gscore, s)
    gmax = jnp.max(gscore, axis=0, keepdims=True)
    giota = lax.broadcasted_iota(I32, (ng, tm), 0)
    g_idx = jnp.min(jnp.where(gscore == gmax, giota, ng), axis=0, keepdims=True)
    in_g = giota == g_idx
    v = [jnp.sum(jnp.where(in_g, sel_j[j], 0.0), axis=0, keepdims=True) for j in range(epg)]
    s = [jnp.sum(jnp.where(in_g, sc_j[j], 0.0), axis=0, keepdims=True) for j in range(epg)]
    order = []
    for j in range(epg):
        r = jnp.zeros((1, tm), I32)
        for k in range(epg):
            if k == j:
                continue
            beats = (v[k] >= v[j]) if k < j else (v[k] > v[j])
            r = r + beats.astype(I32)
        order.append(r)
    eiota = lax.broadcasted_iota(I32, (N_EXPERTS, tm), 0)
    onehot = jnp.zeros((N_EXPERTS, tm), F32)
    loc, raw, rows = [], [], []
    for slot in range(TOP_K):
        lj = jnp.zeros((1, tm), I32)
        gs = jnp.zeros((1, tm), F32)
        for j in range(epg):
            hit = order[j] == slot
            lj = lj + jnp.where(hit, j, 0)
            gs = gs + jnp.where(hit, s[j], 0.0)
        row = lj * ng + g_idx
        onehot = onehot + (eiota == row).astype(F32)
        loc.append(lj)
        raw.append(gs)
        rows.append(row)
    denom = raw[0] + raw[1]
    srow = lax.broadcasted_iota(I32, (tm, tm), 0)
    scol = lax.broadcasted_iota(I32, (tm, tm), 1)
    before = (srow < scol).astype(BF16)
    prefix = jnp.dot(onehot.astype(BF16), before, preferred_element_type=F32) + base_ref[...]
    for slot in range(TOP_K):
        e_ref[slot:slot + 1, :] = g_idx * epg + loc[slot]
        gate_ref[slot:slot + 1, :] = raw[slot] / denom
        rk = jnp.sum(jnp.where(eiota == rows[slot], prefix, 0.0), axis=0, keepdims=True)
        rank_ref[slot:slot + 1, :] = rk.astype(I32)
    base_ref[...] = base_ref[...] + jnp.sum(onehot, axis=1, keepdims=True)
    cnt_ref[...] = jnp.broadcast_to(base_ref[...], cnt_ref.shape)


def ln_router(x, h, g, b, w_router, router_bias, *, tm):
    t, d = x.shape
    assert t % tm == 0
    ng, epg = N_EXPERT_GROUPS, EXPERTS_PER_GROUP
    wr = w_router.T.reshape(ng, epg, d).transpose(1, 0, 2).reshape(N_EXPERTS, d).astype(BF16)
    rb = router_bias.astype(F32).reshape(ng, epg).T.reshape(N_EXPERTS, 1)
    row = pl.BlockSpec((tm, d), lambda i: (i, 0))
    vec = pl.BlockSpec((1, d), lambda i: (0, 0))
    tok = pl.BlockSpec((TOP_K, tm), lambda i: (0, i))
    outs = pl.pallas_call(
        functools.partial(_ln_router_kernel, tm=tm),
        grid=(t // tm,),
        in_specs=[row, row, vec, vec,
                  pl.BlockSpec((N_EXPERTS, d), lambda i: (0, 0)),
                  pl.BlockSpec((N_EXPERTS, 1), lambda i: (0, 0))],
        out_specs=[row, row, tok, tok, tok, pl.BlockSpec((N_EXPERTS, LANES), lambda i: (0, 0))],
        out_shape=[jax.ShapeDtypeStruct((t, d), F32), jax.ShapeDtypeStruct((t, d), BF16),
                   jax.ShapeDtypeStruct((TOP_K, t), I32), jax.ShapeDtypeStruct((TOP_K, t), F32),
                   jax.ShapeDtypeStruct((TOP_K, t), I32),
                   jax.ShapeDtypeStruct((N_EXPERTS, LANES), F32)],
        scratch_shapes=[pltpu.VMEM((N_EXPERTS, 1), F32)],
        compiler_params=_params(
            ("arbitrary",),
            [3 * _nbytes((tm, d), F32), _nbytes((tm, d), BF16), _nbytes((N_EXPERTS, d), BF16)],
            [4 * _nbytes((tm, d), F32)]),
        name="ln_router",
    )(x, h, g.reshape(1, d), b.reshape(1, d), wr, rb)
    x_new, xb, e_idx, gate, rank, cnt = outs
    counts = cnt[:, 0].astype(I32).reshape(epg, ng).T.reshape(N_EXPERTS)
    return x_new, xb, e_idx, gate, rank, counts


def _dispatch_kernel(src_ref, nvalid_ref, x_hbm, o_ref, xbuf, sem):
    b = pl.program_id(0)
    n = pl.num_programs(0)

    def row_copy(blk, r):
        tok = src_ref[blk * BLOCK_ROWS + r]
        return pltpu.make_async_copy(x_hbm.at[pl.ds(tok, 1)], xbuf.at[blk % 2, pl.ds(r, 1)],
                                     sem.at[blk % 2])

    def each_row(blk, act):
        def body(r, carry):
            cp = row_copy(blk, r)
            cp.start() if act == "start" else cp.wait()
            return carry
        lax.fori_loop(0, BLOCK_ROWS, body, 0)

    @pl.when(b == 0)
    def _():
        each_row(b, "start")

    @pl.when(b + 1 < n)
    def _():
        each_row(b + 1, "start")

    each_row(b, "wait")
    rows = lax.broadcasted_iota(I32, o_ref.shape, 0)
    o_ref[...] = jnp.where(rows < nvalid_ref[b], xbuf[b % 2], 0.0).astype(o_ref.dtype)


def dispatch(x, src, nvalid):
    d = x.shape[1]
    n_blocks = nvalid.shape[0]
    return pl.pallas_call(
        _dispatch_kernel,
        grid_spec=pltpu.PrefetchScalarGridSpec(
            num_scalar_prefetch=2,
            grid=(n_blocks,),
            in_specs=[pl.BlockSpec(memory_space=pl.ANY)],
            out_specs=pl.BlockSpec((BLOCK_ROWS, d), lambda b, src, nv: (b, 0)),
            scratch_shapes=[pltpu.VMEM((2, BLOCK_ROWS, d), x.dtype), pltpu.SemaphoreType.DMA((2,))]),
        out_shape=jax.ShapeDtypeStruct((n_blocks * BLOCK_ROWS, d), BF16),
        compiler_params=_params(("arbitrary",), [_nbytes((BLOCK_ROWS, d), BF16)],
                                [3 * _nbytes((2, BLOCK_ROWS, d), x.dtype)]),
        name="moe_dispatch",
    )(src, nvalid, x)


def _expert_kernel(pstart_ref, padded_ref, in_hbm, w_hbm, out_hbm,
                   stage, wbf, ibuf, obuf, wsem, isem, osem, *, layer, gated):
    e = pl.program_id(0)

    def w_copy(ex):
        return pltpu.make_async_copy(w_hbm.at[layer, ex], stage, wsem)

    @pl.when(e == 0)
    def _():
        w_copy(e).start()

    w_copy(e).wait()
    wbf[...] = stage[...].astype(BF16)

    @pl.when(e + 1 < pl.num_programs(0))
    def _():
        w_copy(e + 1).start()

    start = pstart_ref[e]
    n_chunks = (padded_ref[e] + EXPERT_CHUNK - 1) // EXPERT_CHUNK

    def chunk_rows(i):
        return pl.ds(pl.multiple_of(start + i * EXPERT_CHUNK, BLOCK_ROWS), EXPERT_CHUNK)

    def in_copy(i):
        return pltpu.make_async_copy(in_hbm.at[chunk_rows(i)], ibuf.at[i % 2], isem.at[i % 2])

    def out_copy(i):
        return pltpu.make_async_copy(obuf.at[i % 2], out_hbm.at[chunk_rows(i)], osem.at[i % 2])

    @pl.when(n_chunks > 0)
    def _():
        in_copy(0).start()

    def body(i, carry):
        @pl.when(i + 1 < n_chunks)
        def _():
            in_copy(i + 1).start()

        in_copy(i).wait()
        h = jnp.dot(ibuf[i % 2], wbf[...], preferred_element_type=F32)
        if gated:
            half = h.shape[1] // 2
            h1 = h[:, :half]
            h = h1 * jax.nn.sigmoid(h1) * h[:, half:]
        obuf[i % 2] = h.astype(obuf.dtype)
        out_copy(i).start()

        @pl.when(i >= 1)
        def _():
            out_copy(i - 1).wait()
        return carry

    lax.fori_loop(0, n_chunks, body, 0)

    @pl.when(n_chunks > 0)
    def _():
        out_copy(n_chunks - 1).wait()

    @pl.when(e + 1 == pl.num_programs(0))
    def _():
        end = start + n_chunks * EXPERT_CHUNK
        obuf[0] = jnp.zeros(obuf.shape[1:], obuf.dtype)

        def tail_copy(blk):
            rows = pl.ds(pl.multiple_of(end + blk * BLOCK_ROWS, BLOCK_ROWS), BLOCK_ROWS)
            return pltpu.make_async_copy(obuf.at[0, 0:BLOCK_ROWS], out_hbm.at[rows], osem.at[0])

        n_tail = (out_hbm.shape[0] - end) // BLOCK_ROWS
        lax.fori_loop(0, n_tail, lambda blk, c: (tail_copy(blk).start(), c)[1], 0)
        lax.fori_loop(0, n_tail, lambda blk, c: (tail_copy(blk).wait(), c)[1], 0)


def _expert_call(rows_in, w, pad_start, padded, layer, *, gated, out_dtype, name):
    rows, k = rows_in.shape
    n_out = w.shape[-1] // 2 if gated else w.shape[-1]
    wshape = w.shape[-2:]
    resident = [_nbytes(wshape, F32), _nbytes(wshape, BF16),
                2 * _nbytes((EXPERT_CHUNK, k), rows_in.dtype),
                2 * _nbytes((EXPERT_CHUNK, n_out), out_dtype),
                2 * _nbytes((EXPERT_CHUNK, wshape[1]), F32)]
    return pl.pallas_call(
        functools.partial(_expert_kernel, layer=layer, gated=gated),
        grid_spec=pltpu.PrefetchScalarGridSpec(
            num_scalar_prefetch=2,
            grid=(N_EXPERTS,),
            in_specs=[pl.BlockSpec(memory_space=pl.ANY), pl.BlockSpec(memory_space=pl.ANY)],
            out_specs=pl.BlockSpec(memory_space=pl.ANY),
            scratch_shapes=[pltpu.VMEM(wshape, F32), pltpu.VMEM(wshape, BF16),
                            pltpu.VMEM((2, EXPERT_CHUNK, k), rows_in.dtype),
                            pltpu.VMEM((2, EXPERT_CHUNK, n_out), out_dtype),
                            pltpu.SemaphoreType.DMA(()), pltpu.SemaphoreType.DMA((2,)),
                            pltpu.SemaphoreType.DMA((2,))]),
        out_shape=jax.ShapeDtypeStruct((rows, n_out), out_dtype),
        compiler_params=pltpu.CompilerParams(
            dimension_semantics=("arbitrary",), has_side_effects=True,
            vmem_limit_bytes=int(min(VMEM_CAP, sum(resident) + 4 * 1024 * 1024))),
        name=name,
    )(pad_start, padded, rows_in, w)


def expert_ffn(x_disp, pad_start, padded, moe_w_in, moe_w_out, layer):
    act = _expert_call(x_disp, moe_w_in, pad_start, padded, layer, gated=True, out_dtype=BF16,
                       name="expert_in")
    return _expert_call(act, moe_w_out, pad_start, padded, layer, gated=False, out_dtype=F32,
                        name="expert_out")


def _combine_ln_kernel(dest_ref, x_ref, gate_ref, g_ref, b_ref, y_hbm, *rest, tm, t, with_bf16):
    if with_bf16:
        xo_ref, xb_ref, ybuf, sem = rest
    else:
        xo_ref, ybuf, sem = rest
        xb_ref = None
    i = pl.program_id(0)
    n = pl.num_programs(0)

    def row_copy(step, r, k):
        d = dest_ref[k * t + step * tm + r]
        buf = step % 2
        return pltpu.make_async_copy(y_hbm.at[pl.ds(d, 1)], ybuf.at[buf, k, pl.ds(r, 1)], sem.at[buf])

    def each_row(step, act):
        def body(r, carry):
            for k in range(TOP_K):
                cp = row_copy(step, r, k)
                cp.start() if act == "start" else cp.wait()
            return carry
        lax.fori_loop(0, tm, body, 0)

    @pl.when(i == 0)
    def _():
        each_row(i, "start")

    @pl.when(i + 1 < n)
    def _():
        each_row(i + 1, "start")

    each_row(i, "wait")
    buf = i % 2
    gate = gate_ref[...]
    h = gate[:, 0:1] * ybuf[buf, 0] + gate[:, 1:2] * ybuf[buf, 1]
    y = _layer_norm_rows(DEEPNORM_ALPHA * x_ref[...] + h, g_ref[...], b_ref[...])
    xo_ref[...] = y
    if with_bf16:
        xb_ref[...] = y.astype(BF16)


def combine_ln(x, y_disp, dest_flat, gate_t, g, b, *, tm, with_bf16):
    t, d = x.shape
    assert t % tm == 0
    row = pl.BlockSpec((tm, d), lambda i, dest: (i, 0))
    vec = pl.BlockSpec((1, d), lambda i, dest: (0, 0))
    out_specs = [row, row] if with_bf16 else [row]
    out_shape = [jax.ShapeDtypeStruct((t, d), F32)]
    if with_bf16:
        out_shape.append(jax.ShapeDtypeStruct((t, d), BF16))
    outs = pl.pallas_call(
        functools.partial(_combine_ln_kernel, tm=tm, t=t, with_bf16=with_bf16),
        grid_spec=pltpu.PrefetchScalarGridSpec(
            num_scalar_prefetch=1,
            grid=(t // tm,),
            in_specs=[row, pl.BlockSpec((tm, TOP_K), lambda i, dest: (i, 0)), vec, vec,
                      pl.BlockSpec(memory_space=pl.ANY)],
            out_specs=out_specs,
            scratch_shapes=[pltpu.VMEM((2, TOP_K, tm, d), F32), pltpu.SemaphoreType.DMA((2,))]),
        out_shape=out_shape,
        compiler_params=_params(
            ("arbitrary",),
            [2 * _nbytes((tm, d), F32), _nbytes((tm, d), BF16)],
            [_nbytes((2, TOP_K, tm, d), F32), 4 * _nbytes((tm, d), F32)]),
        name="moe_combine_ln",
    )(dest_flat, x, gate_t, g.reshape(1, d), b.reshape(1, d), y_disp)
    return (outs[0], outs[1]) if with_bf16 else (outs[0], None)


def moe_ffn_ln(x, xb_unused, e_idx, gate, rank, counts, moe_w_in, moe_w_out, layer, g, b, *,
               with_bf16, tm):
    del xb_unused
    t, d = x.shape
    tk = t * TOP_K
    padded = (counts + BLOCK_ROWS - 1) // BLOCK_ROWS * BLOCK_ROWS
    pad_end = jnp.cumsum(padded).astype(I32)
    pad_start = pad_end - padded
    n_blocks = (tk + N_EXPERTS * (BLOCK_ROWS - 1) + BLOCK_ROWS - 1) // BLOCK_ROWS + 1
    block_start = jnp.arange(n_blocks, dtype=I32) * BLOCK_ROWS
    block_expert = jnp.minimum(
        jnp.sum(block_start[:, None] >= pad_end[None, :], axis=1), N_EXPERTS - 1).astype(I32)
    nvalid = jnp.clip(counts[block_expert] - (block_start - pad_start[block_expert]), 0, BLOCK_ROWS)
    e_flat = e_idx.reshape(tk)
    dest_flat = pad_start[e_flat] + rank.reshape(tk)
    tok_flat = jnp.arange(tk, dtype=I32) % t
    src = jnp.zeros((n_blocks * BLOCK_ROWS,), I32).at[dest_flat].set(tok_flat)
    x_disp = dispatch(x, src, nvalid.astype(I32))
    y_disp = expert_ffn(x_disp, pad_start, padded.astype(I32), moe_w_in, moe_w_out, layer)
    return combine_ln(x, y_disp, dest_flat, gate.T, g, b, tm=tm, with_bf16=with_bf16)


def _softplus(x):
    return jnp.maximum(x, 0.0) + jnp.log1p(jnp.exp(-jnp.abs(x)))


def _silu(x):
    return x * jax.nn.sigmoid(x)


def _ssd_kernel(xs_ref, b_ref, c_ref, z_ref, dt_ref, dtT_ref,
                cwx_ref, cwb_ref, cwc_ref, cbx_ref, cbb_ref, cbc_ref,
                dtb_ref, dtbT_ref, alog_ref, alogT_ref, dsk_ref, nw_ref,
                y_ref, state_ref, hx_ref, hb_ref, hc_ref):
    q = SSD_CHUNK
    hi = lax.Precision.HIGHEST

    @pl.when(pl.program_id(1) == 0)
    def _():
        state_ref[...] = jnp.zeros_like(state_ref)
        hx_ref[...] = jnp.zeros_like(hx_ref)
        hb_ref[...] = jnp.zeros_like(hb_ref)
        hc_ref[...] = jnp.zeros_like(hc_ref)

    def conv_silu(raw_ref, halo_ref, w_ref, bias_ref):
        raw = raw_ref[...]
        ext = jnp.concatenate([halo_ref[...], raw], axis=0)
        w = w_ref[...]
        acc = bias_ref[...] + w[SSD_CONV - 1:SSD_CONV, :] * raw
        for k in range(SSD_CONV - 1):
            lo = CONV_HALO - (SSD_CONV - 1) + k
            acc = acc + w[k:k + 1, :] * ext[lo:lo + q, :]
        halo_ref[...] = raw[q - CONV_HALO:q, :]
        return _silu(acc)

    xs = conv_silu(xs_ref, hx_ref, cwx_ref, cbx_ref)
    bm = conv_silu(b_ref, hb_ref, cwb_ref, cbb_ref)
    cm = conv_silu(c_ref, hc_ref, cwc_ref, cbc_ref)

    dt = _softplus(dt_ref[...] + dtb_ref[...])
    dt_t = _softplus(dtT_ref[...] + dtbT_ref[...])
    a = -jnp.exp(alog_ref[...])
    a_t = -jnp.exp(alogT_ref[...])
    row = lax.broadcasted_iota(I32, (q, q), 0)
    col = lax.broadcasted_iota(I32, (q, q), 1)
    causal = row >= col
    a_cum = jnp.dot(causal.astype(F32), dt * a, precision=hi, preferred_element_type=F32)
    a_cum_t = jnp.dot(dt_t * a_t, (row <= col).astype(F32), precision=hi,
                      preferred_element_type=F32)
    a_last = a_cum[q - 1:q, :]
    hrow = lax.broadcasted_iota(I32, (SSD_HPG, SSD_GW), 0)
    hcol = lax.broadcasted_iota(I32, (SSD_HPG, SSD_GW), 1)
    expand = (hcol // SSD_HEAD_DIM == hrow).astype(F32)
    decay_in = jnp.dot(jnp.exp(a_cum), expand, precision=hi, preferred_element_type=F32)
    w_state = jnp.dot(jnp.exp(a_last - a_cum) * dt, expand, precision=hi,
                      preferred_element_type=F32)
    chunk_decay = decay_in[q - 1:q, :]

    cmb = cm.astype(BF16)
    bmb = bm.astype(BF16)
    cb = lax.dot_general(cmb, bmb, (((1,), (1,)), ((), ())), preferred_element_type=F32)
    prev = state_ref[...]
    y_off = jnp.dot(cmb, prev.astype(BF16), preferred_element_type=F32) * decay_in
    xw = (w_state * xs).astype(BF16)
    state_ref[...] = chunk_decay * prev + jnp.dot(bm.T.astype(BF16), xw, preferred_element_type=F32)

    lane = lax.broadcasted_iota(I32, (q, LANES), 1)
    heads_per_tile = LANES // SSD_HEAD_DIM
    ys = []
    for tile in range(SSD_GW // LANES):
        ms = []
        for hh in range(heads_per_tile):
            h = tile * heads_per_tile + hh
            seg = a_cum[:, h:h + 1] - a_cum_t[h:h + 1, :]
            decay = jnp.where(causal, jnp.exp(seg), 0.0)
            ms.append((cb * decay * dt_t[h:h + 1, :]).astype(BF16))
        x_tile = xs[:, tile * LANES:(tile + 1) * LANES]
        rhs = jnp.concatenate(
            [jnp.where(lane // SSD_HEAD_DIM == hh, x_tile, 0.0).astype(BF16)
             for hh in range(heads_per_tile)], axis=0)
        ys.append(jnp.dot(jnp.concatenate(ms, axis=1), rhs, preferred_element_type=F32))
    y = jnp.concatenate(ys, axis=1) + y_off + dsk_ref[...] * xs
    y = y * _silu(z_ref[...])
    y = y * lax.rsqrt(jnp.mean(y * y, axis=-1, keepdims=True) + RMS_EPS) * nw_ref[...]
    y_ref[...] = y.astype(y_ref.dtype)


def ssd_core(z, xbc, dt_raw, conv_w, conv_b, dt_bias, a_log, d_skip, norm_w):
    t = z.shape[0]
    q, gw, ns, hpg, ng = SSD_CHUNK, SSD_GW, SSD_STATE, SSD_HPG, SSD_GROUPS
    assert t % q == 0
    b_blk = D_INNER // ns
    c_blk = b_blk + ng
    dt_g = dt_raw.reshape(t, ng, hpg).transpose(1, 0, 2)
    dt_gt = dt_raw.T.reshape(ng, hpg, t)
    dtb = dt_bias.astype(F32).reshape(ng, 1, hpg)
    dtb_t = dt_bias.astype(F32).reshape(ng, hpg, 1)
    alog = a_log.astype(F32).reshape(ng, 1, hpg)
    alog_t = a_log.astype(F32).reshape(ng, hpg, 1)
    dsk = jnp.repeat(d_skip.astype(F32), SSD_HEAD_DIM).reshape(1, D_INNER)
    cw = conv_w.astype(F32)
    cbias = conv_b.astype(F32).reshape(1, SSD_CONV_DIM)
    nw = norm_w.astype(F32).reshape(1, D_INNER)

    def cspec(rows, width, blk):
        return pl.BlockSpec((rows, width), lambda g, c: (0, blk(g)))

    in_specs = [
        pl.BlockSpec((q, gw), lambda g, c: (c, g)),
        pl.BlockSpec((q, ns), lambda g, c: (c, b_blk + g)),
        pl.BlockSpec((q, ns), lambda g, c: (c, c_blk + g)),
        pl.BlockSpec((q, gw), lambda g, c: (c, g)),
        pl.BlockSpec((None, q, hpg), lambda g, c: (g, c, 0)),
        pl.BlockSpec((None, hpg, q), lambda g, c: (g, 0, c)),
        cspec(SSD_CONV, gw, lambda g: g), cspec(SSD_CONV, ns, lambda g: b_blk + g),
        cspec(SSD_CONV, ns, lambda g: c_blk + g),
        cspec(1, gw, lambda g: g), cspec(1, ns, lambda g: b_blk + g), cspec(1, ns, lambda g: c_blk + g),
        pl.BlockSpec((None, 1, hpg), lambda g, c: (g, 0, 0)),
        pl.BlockSpec((None, hpg, 1), lambda g, c: (g, 0, 0)),
        pl.BlockSpec((None, 1, hpg), lambda g, c: (g, 0, 0)),
        pl.BlockSpec((None, hpg, 1), lambda g, c: (g, 0, 0)),
        cspec(1, gw, lambda g: g), cspec(1, gw, lambda g: g),
    ]
    return pl.pallas_call(
        _ssd_kernel,
        grid=(ng, t // q),
        in_specs=in_specs,
        out_specs=pl.BlockSpec((q, gw), lambda g, c: (c, g)),
        out_shape=jax.ShapeDtypeStruct((t, D_INNER), BF16),
        scratch_shapes=[pltpu.VMEM((ns, gw), F32), pltpu.VMEM((CONV_HALO, gw), F32),
                        pltpu.VMEM((CONV_HALO, ns), F32), pltpu.VMEM((CONV_HALO, ns), F32)],
        compiler_params=_params(
            ("arbitrary", "arbitrary"),
            [2 * _nbytes((q, gw), F32), 2 * _nbytes((q, ns), F32), _nbytes((q, gw), BF16)],
            [_nbytes((ns, gw), F32), 16 * _nbytes((q, gw), F32)]),
        name="ssd_core",
    )(xbc, xbc, xbc, z, dt_g, dt_gt, cw, cw, cw, cbias, cbias, cbias,
      dtb, dtb_t, alog, alog_t, dsk, nw)


MM_TM = 512
MM_TN = 512
LN_TM = 256
COMBINE_TM = 128


def pool_mixer(xb, w_in, w_group, scale, w_out):
    p = pool_in(xb, w_in, tm=MM_TM, tn=MM_TN)
    mixed = pool_group_matmul(p, w_group, scale, tm=MM_TM)
    return matmul_cols(mixed, w_out, 0, D_MODEL, tm=MM_TM, tn=MM_TN, out_dtype=F32, name="pool_out")


def ssd_mixer(xb, w_in, conv_w, conv_b, dt_bias, a_log, d_skip, norm_w, w_out):
    z = matmul_cols(xb, w_in, 0, D_INNER, tm=MM_TM, tn=MM_TN, out_dtype=F32, name="ssd_in_z")
    xbc = matmul_cols(xb, w_in, D_INNER, SSD_CONV_DIM, tm=MM_TM, tn=MM_TN, out_dtype=F32,
                      name="ssd_in_xbc")
    dt_raw = matmul_cols(xb, w_in, D_INNER + SSD_CONV_DIM, SSD_HEADS, tm=MM_TM, tn=SSD_HEADS,
                         out_dtype=F32, name="ssd_in_dt")
    y = ssd_core(z, xbc, dt_raw, conv_w, conv_b, dt_bias, a_log, d_skip, norm_w)
    return matmul_cols(y, w_out, 0, D_MODEL, tm=MM_TM, tn=MM_TN // 2, out_dtype=F32, name="ssd_out")


def kernel(x, pool_w_in, pool_w_group, pool_scale, pool_w_out, ssd_w_in, ssd_conv_w, ssd_conv_b,
           ssd_dt_bias, ssd_a_log, ssd_d, ssd_norm_w, ssd_w_out, moe_w_router, moe_router_bias,
           moe_w_in, moe_w_out, ln_mix_g, ln_mix_b, ln_ffn_g, ln_ffn_b):
    bsz, seq, d = x.shape
    x = x.reshape(bsz * seq, d)
    assert bsz == 1
    xb = x.astype(BF16)
    for i in range(DEPTH):
        j = i // N_MIXERS
        if i % N_MIXERS == 0:
            h = pool_mixer(xb, pool_w_in[j], pool_w_group[j], pool_scale[j], pool_w_out[j])
        else:
            h = ssd_mixer(xb, ssd_w_in[j], ssd_conv_w[j], ssd_conv_b[j], ssd_dt_bias[j],
                          ssd_a_log[j], ssd_d[j], ssd_norm_w[j], ssd_w_out[j])
        x, xb, e_idx, gate, rank, counts = ln_router(
            x, h, ln_mix_g[i], ln_mix_b[i], moe_w_router, moe_router_bias, tm=LN_TM)
        x, xb = moe_ffn_ln(x, xb, e_idx, gate, rank, counts, moe_w_in, moe_w_out, i,
                           ln_ffn_g[i], ln_ffn_b[i], with_bf16=(i + 1 < DEPTH), tm=COMBINE_TM)
    return x.reshape(bsz, seq, d)
```

```python
import functools

import jax
import jax.numpy as jnp
from jax import lax
from jax.experimental import pallas as pl
from jax.experimental.pallas import tpu as pltpu

F32 = jnp.float32
BF16 = jnp.bfloat16
I32 = jnp.int32

D_MODEL = 4096
DEPTH = 2
N_MIXERS = 2
DEEPNORM_ALPHA = (2 * DEPTH) ** 0.25
LN_EPS = 1e-5
POOL_WINDOWS = (2, 4, 8, 16)
POOL_GROUPS = len(POOL_WINDOWS)
POOL_GC = D_MODEL // POOL_GROUPS
D_INNER = 2 * D_MODEL
SSD_HEAD_DIM = 64
SSD_HEADS = D_INNER // SSD_HEAD_DIM
SSD_GROUPS = 8
SSD_HPG = SSD_HEADS // SSD_GROUPS
SSD_STATE = 128
SSD_CONV = 4
SSD_CHUNK = 128
SSD_GW = D_INNER // SSD_GROUPS
SSD_CONV_DIM = D_INNER + 2 * SSD_GROUPS * SSD_STATE
SSD_IN_DIM = D_INNER + SSD_CONV_DIM + SSD_HEADS
RMS_EPS = 1e-5
N_EXPERTS = 32
N_EXPERT_GROUPS = 8
EXPERTS_PER_GROUP = N_EXPERTS // N_EXPERT_GROUPS
TOP_K = 2
D_FF = 768
BLOCK_ROWS = 128

LANES = 128
SUBLANES = 8
VMEM_BYTES_V7X = 64 * 1024 * 1024
VMEM_CAP = VMEM_BYTES_V7X * 7 // 8

POOL_HALO = 16
CONV_HALO = SUBLANES
EXPERT_CHUNK = 2 * BLOCK_ROWS
ROW_UNROLL = 8


def _nbytes(shape, dtype):
    n = 1
    for s in shape:
        n *= s
    return n * jnp.dtype(dtype).itemsize


def _vmem_limit(pipelined, resident):
    est = 2 * sum(pipelined) + sum(resident)
    return int(min(VMEM_CAP, est + max(est // 4, 8 * 1024 * 1024)))


def _params(semantics, pipelined, resident=()):
    return pltpu.CompilerParams(dimension_semantics=semantics,
                                vmem_limit_bytes=_vmem_limit(pipelined, resident))


def _mm_kernel(a_ref, w_ref, o_ref, wbf_ref):
    @pl.when(pl.program_id(1) == 0)
    def _():
        wbf_ref[...] = w_ref[...].astype(BF16)

    o_ref[...] = jnp.dot(a_ref[...], wbf_ref[...], preferred_element_type=F32).astype(o_ref.dtype)


def _mm_scale_kernel(a_ref, w_ref, s_ref, o_ref, wbf_ref):
    @pl.when(pl.program_id(1) == 0)
    def _():
        wbf_ref[...] = w_ref[...].astype(BF16)

    acc = jnp.dot(a_ref[...], wbf_ref[...], preferred_element_type=F32)
    o_ref[...] = (acc * s_ref[...]).astype(o_ref.dtype)


def matmul_cols(a, w, col_off, n_cols, *, tm, tn, out_dtype, name):
    m_rows, k = a.shape
    assert w.shape[0] == k and m_rows % tm == 0 and n_cols % tn == 0 and col_off % tn == 0
    off = col_off // tn
    return pl.pallas_call(
        _mm_kernel,
        grid=(n_cols // tn, m_rows // tm),
        in_specs=[pl.BlockSpec((tm, k), lambda n, m: (m, 0)),
                  pl.BlockSpec((k, tn), lambda n, m: (0, n + off))],
        out_specs=pl.BlockSpec((tm, tn), lambda n, m: (m, n)),
        out_shape=jax.ShapeDtypeStruct((m_rows, n_cols), out_dtype),
        scratch_shapes=[pltpu.VMEM((k, tn), BF16)],
        compiler_params=_params(
            ("arbitrary", "arbitrary"),
            [_nbytes((tm, k), BF16), _nbytes((k, tn), F32), _nbytes((tm, tn), out_dtype)],
            [_nbytes((k, tn), BF16), _nbytes((tm, tn), F32)]),
        name=name,
    )(a, w)


def _mm_bf16w_kernel(a_ref, w_ref, o_ref):
    o_ref[...] = jnp.dot(a_ref[...], w_ref[...], preferred_element_type=F32).astype(o_ref.dtype)


def matmul_bf16w(a, w, *, tm, tn, out_dtype, name):
    m_rows, k = a.shape
    n_cols = w.shape[1]
    assert w.shape[0] == k and m_rows % tm == 0 and n_cols % tn == 0
    return pl.pallas_call(
        _mm_bf16w_kernel,
        grid=(n_cols // tn, m_rows // tm),
        in_specs=[pl.BlockSpec((tm, k), lambda n, m: (m, 0)),
                  pl.BlockSpec((k, tn), lambda n, m: (0, n))],
        out_specs=pl.BlockSpec((tm, tn), lambda n, m: (m, n)),
        out_shape=jax.ShapeDtypeStruct((m_rows, n_cols), out_dtype),
        compiler_params=_params(
            ("arbitrary", "arbitrary"),
            [_nbytes((tm, k), BF16), _nbytes((k, tn), BF16), _nbytes((tm, tn), out_dtype)],
            [_nbytes((tm, tn), F32)]),
        name=name,
    )(a, w)


def pool_group_matmul(p, w_group, scale, *, tm):
    m_rows = p.shape[0]
    gc = POOL_GC
    assert m_rows % tm == 0
    return pl.pallas_call(
        _mm_scale_kernel,
        grid=(POOL_GROUPS, m_rows // tm),
        in_specs=[pl.BlockSpec((tm, gc), lambda g, m: (m, g)),
                  pl.BlockSpec((None, gc, gc), lambda g, m: (g, 0, 0)),
                  pl.BlockSpec((1, gc), lambda g, m: (0, g))],
        out_specs=pl.BlockSpec((tm, gc), lambda g, m: (m, g)),
        out_shape=jax.ShapeDtypeStruct((m_rows, POOL_GROUPS * gc), BF16),
        scratch_shapes=[pltpu.VMEM((gc, gc), BF16)],
        compiler_params=_params(
            ("arbitrary", "arbitrary"),
            [_nbytes((tm, gc), BF16), _nbytes((gc, gc), F32), _nbytes((tm, gc), BF16)],
            [_nbytes((gc, gc), BF16), _nbytes((tm, gc), F32)]),
        name="pool_group_matmul",
    )(p, w_group, scale.reshape(1, -1))


def _pool_in_kernel(x_ref, w_ref, p_ref, wbf_ref, ubuf_ref, *, tm, tn):
    n = pl.program_id(0)
    m = pl.program_id(1)

    @pl.when(m == 0)
    def _():
        wbf_ref[...] = w_ref[...].astype(BF16)
        ubuf_ref[0:POOL_HALO, :] = jnp.zeros((POOL_HALO, tn), F32)

    u = jnp.dot(x_ref[...], wbf_ref[...], preferred_element_type=F32)
    ubuf_ref[POOL_HALO:POOL_HALO + tm, :] = u
    pos = (m * tm + 1 + lax.broadcasted_iota(I32, (tm, tn), 0)).astype(F32)
    group = (n * tn) // POOL_GC
    for gi, window in enumerate(POOL_WINDOWS):
        @pl.when(group == gi)
        def _(window=window):
            acc = u
            for k in range(1, window):
                acc = acc + ubuf_ref[POOL_HALO - k:POOL_HALO - k + tm, :]
            mean = acc / jnp.minimum(pos, float(window))
            p_ref[...] = (mean - u).astype(p_ref.dtype)

    ubuf_ref[0:POOL_HALO, :] = ubuf_ref[tm:tm + POOL_HALO, :]


def pool_in(xb, w_in, *, tm, tn):
    m_rows, k = xb.shape
    n_cols = w_in.shape[1]
    assert m_rows % tm == 0 and n_cols % tn == 0 and POOL_GC % tn == 0 and tm >= POOL_HALO
    return pl.pallas_call(
        functools.partial(_pool_in_kernel, tm=tm, tn=tn),
        grid=(n_cols // tn, m_rows // tm),
        in_specs=[pl.BlockSpec((tm, k), lambda n, m: (m, 0)),
                  pl.BlockSpec((k, tn), lambda n, m: (0, n))],
        out_specs=pl.BlockSpec((tm, tn), lambda n, m: (m, n)),
        out_shape=jax.ShapeDtypeStruct((m_rows, n_cols), BF16),
        scratch_shapes=[pltpu.VMEM((k, tn), BF16), pltpu.VMEM((tm + POOL_HALO, tn), F32)],
        compiler_params=_params(
            ("arbitrary", "arbitrary"),
            [_nbytes((tm, k), BF16), _nbytes((k, tn), F32), _nbytes((tm, tn), BF16)],
            [_nbytes((k, tn), BF16), 4 * _nbytes((tm + POOL_HALO, tn), F32)]),
        name="pool_in",
    )(xb, w_in)


def _layer_norm_rows(v, g, b):
    mu = jnp.mean(v, axis=-1, keepdims=True)
    vc = v - mu
    var = jnp.mean(vc * vc, axis=-1, keepdims=True)
    return vc * lax.rsqrt(var + LN_EPS) * g + b


def _ln_router_kernel(x_ref, h_ref, g_ref, b_ref, wr_ref, rb_ref,
                      xo_ref, xb_ref, e_ref, gate_ref, rank_ref, cnt_ref, base_ref, *, tm):
    i = pl.program_id(0)

    @pl.when(i == 0)
    def _():
        base_ref[...] = jnp.zeros_like(base_ref)

    y = _layer_norm_rows(DEEPNORM_ALPHA * x_ref[...] + h_ref[...], g_ref[...], b_ref[...])
    xo_ref[...] = y
    yb = y.astype(BF16)
    xb_ref[...] = yb

    ng, epg = N_EXPERT_GROUPS, EXPERTS_PER_GROUP
    logits = lax.dot_general(wr_ref[...], yb, (((1,), (1,)), ((), ())),
                             preferred_element_type=F32)
    scores = jax.nn.sigmoid(logits)
    sel = scores + rb_ref[...]
    sel_j = [sel[ng * j:ng * (j + 1), :] for j in range(epg)]
    sc_j = [scores[ng * j:ng * (j + 1), :] for j in range(epg)]
    gscore = None
    for j1 in range(epg):
        for j2 in range(j1 + 1, epg):
            s = sel_j[j1] + sel_j[j2]
            gscore = s if gscore is None else jnp.maximum(gscore, s)
    gmax = jnp.max(gscore, axis=0, keepdims=True)
    giota = lax.broadcasted_iota(I32, (ng, tm), 0)
    g_idx = jnp.min(jnp.where(gscore == gmax, giota, ng), axis=0, keepdims=True)
    in_g = giota == g_idx
    v = [jnp.sum(jnp.where(in_g, sel_j[j], 0.0), axis=0, keepdims=True) for j in range(epg)]
    s = [jnp.sum(jnp.where(in_g, sc_j[j], 0.0), axis=0, keepdims=True) for j in range(epg)]
    order = []
    for j in range(epg):
        r = jnp.zeros((1, tm), I32)
        for k in range(epg):
            if k == j:
                continue
            beats = (v[k] >= v[j]) if k < j else (v[k] > v[j])
            r = r + beats.astype(I32)
        order.append(r)
    eiota = lax.broadcasted_iota(I32, (N_EXPERTS, tm), 0)
    onehot = jnp.zeros((N_EXPERTS, tm), F32)
    loc, raw, rows = [], [], []
    for slot in range(TOP_K):
        lj = jnp.zeros((1, tm), I32)
        gs = jnp.zeros((1, tm), F32)
        for j in range(epg):
            hit = order[j] == slot
            lj = lj + jnp.where(hit, j, 0)
            gs = gs + jnp.where(hit, s[j], 0.0)
        row = lj * ng + g_idx
        onehot = onehot + (eiota == row).astype(F32)
        loc.append(lj)
        raw.append(gs)
        rows.append(row)
    denom = raw[0] + raw[1]
    srow = lax.broadcasted_iota(I32, (tm, tm), 0)
    scol = lax.broadcasted_iota(I32, (tm, tm), 1)
    before = (srow < scol).astype(BF16)
    prefix = jnp.dot(onehot.astype(BF16), before, preferred_element_type=F32) + base_ref[...]
    for slot in range(TOP_K):
        e_ref[slot:slot + 1, :] = g_idx * epg + loc[slot]
        gate_ref[slot:slot + 1, :] = raw[slot] / denom
        rk = jnp.sum(jnp.where(eiota == rows[slot], prefix, 0.0), axis=0, keepdims=True)
        rank_ref[slot:slot + 1, :] = rk.astype(I32)
    base_ref[...] = base_ref[...] + jnp.sum(onehot, axis=1, keepdims=True)
    cnt_ref[...] = jnp.broadcast_to(base_ref[...], cnt_ref.shape)


def ln_router(x, h, g, b, w_router, router_bias, *, tm):
    t, d = x.shape
    assert t % tm == 0
    ng, epg = N_EXPERT_GROUPS, EXPERTS_PER_GROUP
    wr = w_router.T.reshape(ng, epg, d).transpose(1, 0, 2).reshape(N_EXPERTS, d).astype(BF16)
    rb = router_bias.astype(F32).reshape(ng, epg).T.reshape(N_EXPERTS, 1)
    row = pl.BlockSpec((tm, d), lambda i: (i, 0))
    vec = pl.BlockSpec((1, d), lambda i: (0, 0))
    tok = pl.BlockSpec((TOP_K, tm), lambda i: (0, i))
    outs = pl.pallas_call(
        functools.partial(_ln_router_kernel, tm=tm),
        grid=(t // tm,),
        in_specs=[row, row, vec, vec,
                  pl.BlockSpec((N_EXPERTS, d), lambda i: (0, 0)),
                  pl.BlockSpec((N_EXPERTS, 1), lambda i: (0, 0))],
        out_specs=[row, row, tok, tok, tok, pl.BlockSpec((N_EXPERTS, LANES), lambda i: (0, 0))],
        out_shape=[jax.ShapeDtypeStruct((t, d), F32), jax.ShapeDtypeStruct((t, d), BF16),
                   jax.ShapeDtypeStruct((TOP_K, t), I32), jax.ShapeDtypeStruct((TOP_K, t), F32),
                   jax.ShapeDtypeStruct((TOP_K, t), I32),
                   jax.ShapeDtypeStruct((N_EXPERTS, LANES), F32)],
        scratch_shapes=[pltpu.VMEM((N_EXPERTS, 1), F32)],
        compiler_params=_params(
            ("arbitrary",),
            [3 * _nbytes((tm, d), F32), _nbytes((tm, d), BF16), _nbytes((N_EXPERTS, d), BF16)],
            [4 * _nbytes((tm, d), F32)]),
        name="ln_router",
    )(x, h, g.reshape(1, d), b.reshape(1, d), wr, rb)
    x_new, xb, e_idx, gate, rank, cnt = outs
    counts = cnt[:, 0].astype(I32).reshape(epg, ng).T.reshape(N_EXPERTS)
    return x_new, xb, e_idx, gate, rank, counts


def _dispatch_kernel(src_ref, nvalid_ref, x_hbm, o_ref, xbuf, sem):
    b = pl.program_id(0)
    n = pl.num_programs(0)

    def start_rows(blk):
        def body(r8, carry):
            for u in range(ROW_UNROLL):
                r = r8 * ROW_UNROLL + u
                tok = src_ref[blk * BLOCK_ROWS + r]
                pltpu.make_async_copy(x_hbm.at[pl.ds(tok, 1)], xbuf.at[blk % 2, pl.ds(r, 1)],
                                      sem.at[blk % 2]).start(priority=u % 2)
            return carry
        lax.fori_loop(0, BLOCK_ROWS // ROW_UNROLL, body, 0)

    def wait_rows(blk):
        pltpu.make_async_copy(x_hbm.at[pl.ds(0, BLOCK_ROWS)], xbuf.at[blk % 2], sem.at[blk % 2]).wait()

    @pl.when(b == 0)
    def _():
        start_rows(b)

    @pl.when(b + 1 < n)
    def _():
        start_rows(b + 1)

    wait_rows(b)
    rows = lax.broadcasted_iota(I32, o_ref.shape, 0)
    o_ref[...] = jnp.where(rows < nvalid_ref[b], xbuf[b % 2], 0.0).astype(o_ref.dtype)


def dispatch(x, src, nvalid):
    d = x.shape[1]
    n_blocks = nvalid.shape[0]
    return pl.pallas_call(
        _dispatch_kernel,
        grid_spec=pltpu.PrefetchScalarGridSpec(
            num_scalar_prefetch=2,
            grid=(n_blocks,),
            in_specs=[pl.BlockSpec(memory_space=pl.ANY)],
            out_specs=pl.BlockSpec((BLOCK_ROWS, d), lambda b, src, nv: (b, 0)),
            scratch_shapes=[pltpu.VMEM((2, BLOCK_ROWS, d), x.dtype), pltpu.SemaphoreType.DMA((2,))]),
        out_shape=jax.ShapeDtypeStruct((n_blocks * BLOCK_ROWS, d), BF16),
        compiler_params=_params(("arbitrary",), [_nbytes((BLOCK_ROWS, d), BF16)],
                                [3 * _nbytes((2, BLOCK_ROWS, d), x.dtype)]),
        name="moe_dispatch",
    )(src, nvalid, x)


def _expert_kernel(pstart_ref, padded_ref, in_hbm, w_hbm, out_hbm,
                   stage, wbf, ibuf, obuf, wsem, isem, osem, *, layer, gated):
    e = pl.program_id(0)

    def w_copy(ex):
        return pltpu.make_async_copy(w_hbm.at[layer, ex], stage, wsem)

    @pl.when(e == 0)
    def _():
        w_copy(e).start()

    w_copy(e).wait()
    wbf[...] = stage[...].astype(BF16)

    @pl.when(e + 1 < pl.num_programs(0))
    def _():
        w_copy(e + 1).start()

    start = pstart_ref[e]
    n_chunks = (padded_ref[e] + EXPERT_CHUNK - 1) // EXPERT_CHUNK

    def chunk_rows(i):
        return pl.ds(pl.multiple_of(start + i * EXPERT_CHUNK, BLOCK_ROWS), EXPERT_CHUNK)

    def in_copy(i):
        return pltpu.make_async_copy(in_hbm.at[chunk_rows(i)], ibuf.at[i % 2], isem.at[i % 2])

    def out_copy(i):
        return pltpu.make_async_copy(obuf.at[i % 2], out_hbm.at[chunk_rows(i)], osem.at[i % 2])

    @pl.when(n_chunks > 0)
    def _():
        in_copy(0).start()

    def body(i, carry):
        @pl.when(i + 1 < n_chunks)
        def _():
            in_copy(i + 1).start()

        in_copy(i).wait()
        h = jnp.dot(ibuf[i % 2], wbf[...], preferred_element_type=F32)
        if gated:
            half = h.shape[1] // 2
            h1 = h[:, :half]
            h = h1 * jax.nn.sigmoid(h1) * h[:, half:]
        obuf[i % 2] = h.astype(obuf.dtype)
        out_copy(i).start()

        @pl.when(i >= 1)
        def _():
            out_copy(i - 1).wait()
        return carry

    lax.fori_loop(0, n_chunks, body, 0)

    @pl.when(n_chunks > 0)
    def _():
        out_copy(n_chunks - 1).wait()

    @pl.when(e + 1 == pl.num_programs(0))
    def _():
        end = start + n_chunks * EXPERT_CHUNK
        obuf[0] = jnp.zeros(obuf.shape[1:], obuf.dtype)

        def tail_copy(blk):
            rows = pl.ds(pl.multiple_of(end + blk * BLOCK_ROWS, BLOCK_ROWS), BLOCK_ROWS)
            return pltpu.make_async_copy(obuf.at[0, 0:BLOCK_ROWS], out_hbm.at[rows], osem.at[0])

        n_tail = (out_hbm.shape[0] - end) // BLOCK_ROWS
        lax.fori_loop(0, n_tail, lambda blk, c: (tail_copy(blk).start(), c)[1], 0)
        lax.fori_loop(0, n_tail, lambda blk, c: (tail_copy(blk).wait(), c)[1], 0)


def _expert_call(rows_in, w, pad_start, padded, layer, *, gated, out_dtype, name):
    rows, k = rows_in.shape
    n_out = w.shape[-1] // 2 if gated else w.shape[-1]
    wshape = w.shape[-2:]
    resident = [_nbytes(wshape, F32), _nbytes(wshape, BF16),
                2 * _nbytes((EXPERT_CHUNK, k), rows_in.dtype),
                2 * _nbytes((EXPERT_CHUNK, n_out), out_dtype),
                2 * _nbytes((EXPERT_CHUNK, wshape[1]), F32)]
    return pl.pallas_call(
        functools.partial(_expert_kernel, layer=layer, gated=gated),
        grid_spec=pltpu.PrefetchScalarGridSpec(
            num_scalar_prefetch=2,
            grid=(N_EXPERTS,),
            in_specs=[pl.BlockSpec(memory_space=pl.ANY), pl.BlockSpec(memory_space=pl.ANY)],
            out_specs=pl.BlockSpec(memory_space=pl.ANY),
            scratch_shapes=[pltpu.VMEM(wshape, F32), pltpu.VMEM(wshape, BF16),
                            pltpu.VMEM((2, EXPERT_CHUNK, k), rows_in.dtype),
                            pltpu.VMEM((2, EXPERT_CHUNK, n_out), out_dtype),
                            pltpu.SemaphoreType.DMA(()), pltpu.SemaphoreType.DMA((2,)),
                            pltpu.SemaphoreType.DMA((2,))]),
        out_shape=jax.ShapeDtypeStruct((rows, n_out), out_dtype),
        compiler_params=pltpu.CompilerParams(
            dimension_semantics=("arbitrary",), has_side_effects=True,
            vmem_limit_bytes=int(min(VMEM_CAP, sum(resident) + 4 * 1024 * 1024))),
        name=name,
    )(pad_start, padded, rows_in, w)


def expert_ffn(x_disp, pad_start, padded, moe_w_in, moe_w_out, layer):
    act = _expert_call(x_disp, moe_w_in, pad_start, padded, layer, gated=True, out_dtype=BF16,
                       name="expert_in")
    return _expert_call(act, moe_w_out, pad_start, padded, layer, gated=False, out_dtype=F32,
                        name="expert_out")


def _combine_ln_kernel(dest_ref, x_ref, gate_ref, g_ref, b_ref, y_hbm, *rest, tm, t, with_bf16):
    if with_bf16:
        xo_ref, xb_ref, ybuf, sem = rest
    else:
        xo_ref, ybuf, sem = rest
        xb_ref = None
    i = pl.program_id(0)
    n = pl.num_programs(0)

    def start_rows(step):
        buf = step % 2

        def body(r8, carry):
            for u in range(ROW_UNROLL):
                r = r8 * ROW_UNROLL + u
                for k in range(TOP_K):
                    d = dest_ref[k * t + step * tm + r]
                    pltpu.make_async_copy(y_hbm.at[pl.ds(d, 1)], ybuf.at[buf, k, pl.ds(r, 1)],
                                          sem.at[buf]).start(priority=(u + k) % 2)
            return carry
        lax.fori_loop(0, tm // ROW_UNROLL, body, 0)

    def wait_rows(step):
        buf = step % 2
        for k in range(TOP_K):
            pltpu.make_async_copy(y_hbm.at[pl.ds(0, tm)], ybuf.at[buf, k], sem.at[buf]).wait()

    @pl.when(i == 0)
    def _():
        start_rows(i)

    @pl.when(i + 1 < n)
    def _():
        start_rows(i + 1)

    wait_rows(i)
    buf = i % 2
    gate = gate_ref[...]
    h = gate[:, 0:1] * ybuf[buf, 0] + gate[:, 1:2] * ybuf[buf, 1]
    y = _layer_norm_rows(DEEPNORM_ALPHA * x_ref[...] + h, g_ref[...], b_ref[...])
    xo_ref[...] = y
    if with_bf16:
        xb_ref[...] = y.astype(BF16)


def combine_ln(x, y_disp, dest_flat, gate_t, g, b, *, tm, with_bf16):
    t, d = x.shape
    assert t % tm == 0
    row = pl.BlockSpec((tm, d), lambda i, dest: (i, 0))
    vec = pl.BlockSpec((1, d), lambda i, dest: (0, 0))
    out_specs = [row, row] if with_bf16 else [row]
    out_shape = [jax.ShapeDtypeStruct((t, d), F32)]
    if with_bf16:
        out_shape.append(jax.ShapeDtypeStruct((t, d), BF16))
    outs = pl.pallas_call(
        functools.partial(_combine_ln_kernel, tm=tm, t=t, with_bf16=with_bf16),
        grid_spec=pltpu.PrefetchScalarGridSpec(
            num_scalar_prefetch=1,
            grid=(t // tm,),
            in_specs=[row, pl.BlockSpec((tm, TOP_K), lambda i, dest: (i, 0)), vec, vec,
                      pl.BlockSpec(memory_space=pl.ANY)],
            out_specs=out_specs,
            scratch_shapes=[pltpu.VMEM((2, TOP_K, tm, d), F32), pltpu.SemaphoreType.DMA((2,))]),
        out_shape=out_shape,
        compiler_params=_params(
            ("arbitrary",),
            [2 * _nbytes((tm, d), F32), _nbytes((tm, d), BF16)],
            [_nbytes((2, TOP_K, tm, d), F32), 4 * _nbytes((tm, d), F32)]),
        name="moe_combine_ln",
    )(dest_flat, x, gate_t, g.reshape(1, d), b.reshape(1, d), y_disp)
    return (outs[0], outs[1]) if with_bf16 else (outs[0], None)


def moe_ffn_ln(x, xb_unused, e_idx, gate, rank, counts, moe_w_in, moe_w_out, layer, g, b, *,
               with_bf16, tm):
    del xb_unused
    t, d = x.shape
    tk = t * TOP_K
    padded = (counts + BLOCK_ROWS - 1) // BLOCK_ROWS * BLOCK_ROWS
    pad_end = jnp.cumsum(padded).astype(I32)
    pad_start = pad_end - padded
    n_blocks = (tk + N_EXPERTS * (BLOCK_ROWS - 1) + BLOCK_ROWS - 1) // BLOCK_ROWS + 1
    block_start = jnp.arange(n_blocks, dtype=I32) * BLOCK_ROWS
    block_expert = jnp.minimum(
        jnp.sum(block_start[:, None] >= pad_end[None, :], axis=1), N_EXPERTS - 1).astype(I32)
    nvalid = jnp.clip(counts[block_expert] - (block_start - pad_start[block_expert]), 0, BLOCK_ROWS)
    e_flat = e_idx.reshape(tk)
    dest_flat = pad_start[e_flat] + rank.reshape(tk)
    tok_flat = jnp.arange(tk, dtype=I32) % t
    src = jnp.zeros((n_blocks * BLOCK_ROWS,), I32).at[dest_flat].set(tok_flat)
    x_disp = dispatch(x, src, nvalid.astype(I32))
    y_disp = expert_ffn(x_disp, pad_start, padded.astype(I32), moe_w_in, moe_w_out, layer)
    return combine_ln(x, y_disp, dest_flat, gate.T, g, b, tm=tm, with_bf16=with_bf16)


def _softplus(x):
    return jnp.maximum(x, 0.0) + jnp.log1p(jnp.exp(-jnp.abs(x)))


def _silu(x):
    return x * jax.nn.sigmoid(x)


def _dot_01_f32(m01, v, *, ones_left):
    hi = v.astype(BF16)
    r = v - hi.astype(F32)
    mid = r.astype(BF16)
    lo = (r - mid.astype(F32)).astype(BF16)

    def dot(p):
        lhs, rhs = (m01, p) if ones_left else (p, m01)
        return jnp.dot(lhs, rhs, preferred_element_type=F32)
    return dot(hi) + dot(mid) + dot(lo)


def _ssd_kernel(xs_ref, b_ref, c_ref, z_ref, dt_ref, dtT_ref,
                cwx_ref, cwb_ref, cwc_ref, cbx_ref, cbb_ref, cbc_ref,
                dtb_ref, dtbT_ref, alog_ref, alogT_ref, dsk_ref, nw_ref,
                y_ref, state_ref, hx_ref, hb_ref, hc_ref):
    q = SSD_CHUNK

    @pl.when(pl.program_id(1) == 0)
    def _():
        state_ref[...] = jnp.zeros_like(state_ref)
        hx_ref[...] = jnp.zeros_like(hx_ref)
        hb_ref[...] = jnp.zeros_like(hb_ref)
        hc_ref[...] = jnp.zeros_like(hc_ref)

    def conv_silu(raw_ref, halo_ref, w_ref, bias_ref):
        raw = raw_ref[...]
        ext = jnp.concatenate([halo_ref[...], raw], axis=0)
        w = w_ref[...]
        acc = bias_ref[...] + w[SSD_CONV - 1:SSD_CONV, :] * raw
        for k in range(SSD_CONV - 1):
            lo = CONV_HALO - (SSD_CONV - 1) + k
            acc = acc + w[k:k + 1, :] * ext[lo:lo + q, :]
        halo_ref[...] = raw[q - CONV_HALO:q, :]
        return _silu(acc)

    xs = conv_silu(xs_ref, hx_ref, cwx_ref, cbx_ref)
    bm = conv_silu(b_ref, hb_ref, cwb_ref, cbb_ref)
    cm = conv_silu(c_ref, hc_ref, cwc_ref, cbc_ref)

    dt = _softplus(dt_ref[...] + dtb_ref[...])
    dt_t = _softplus(dtT_ref[...] + dtbT_ref[...])
    a = -jnp.exp(alog_ref[...])
    a_t = -jnp.exp(alogT_ref[...])
    row = lax.broadcasted_iota(I32, (q, q), 0)
    col = lax.broadcasted_iota(I32, (q, q), 1)
    causal = row >= col
    a_cum = _dot_01_f32(jnp.where(causal, 1.0, 0.0).astype(BF16), dt * a, ones_left=True)
    a_cum_t = _dot_01_f32(jnp.where(row <= col, 1.0, 0.0).astype(BF16), dt_t * a_t,
                          ones_left=False)
    a_last = a_cum[q - 1:q, :]
    hrow = lax.broadcasted_iota(I32, (SSD_HPG, SSD_GW), 0)
    hcol = lax.broadcasted_iota(I32, (SSD_HPG, SSD_GW), 1)
    expand = jnp.where(hcol // SSD_HEAD_DIM == hrow, 1.0, 0.0).astype(BF16)
    decay_in = _dot_01_f32(expand, jnp.exp(a_cum), ones_left=False)
    w_state = _dot_01_f32(expand, jnp.exp(a_last - a_cum) * dt, ones_left=False)
    chunk_decay = decay_in[q - 1:q, :]

    cmb = cm.astype(BF16)
    bmb = bm.astype(BF16)
    cb = lax.dot_general(cmb, bmb, (((1,), (1,)), ((), ())), preferred_element_type=F32)
    prev = state_ref[...]
    y_off = jnp.dot(cmb, prev.astype(BF16), preferred_element_type=F32) * decay_in
    xw = (w_state * xs).astype(BF16)
    state_ref[...] = chunk_decay * prev + jnp.dot(bm.T.astype(BF16), xw, preferred_element_type=F32)

    lane = lax.broadcasted_iota(I32, (q, LANES), 1)
    heads_per_tile = LANES // SSD_HEAD_DIM
    ys = []
    for tile in range(SSD_GW // LANES):
        ms = []
        for hh in range(heads_per_tile):
            h = tile * heads_per_tile + hh
            seg = a_cum[:, h:h + 1] - a_cum_t[h:h + 1, :]
            decay = jnp.where(causal, jnp.exp(seg), 0.0)
            ms.append((cb * decay * dt_t[h:h + 1, :]).astype(BF16))
        x_tile = xs[:, tile * LANES:(tile + 1) * LANES]
        rhs = jnp.concatenate(
            [jnp.where(lane // SSD_HEAD_DIM == hh, x_tile, 0.0).astype(BF16)
             for hh in range(heads_per_tile)], axis=0)
        ys.append(jnp.dot(jnp.concatenate(ms, axis=1), rhs, preferred_element_type=F32))
    y = jnp.concatenate(ys, axis=1) + y_off + dsk_ref[...] * xs
    y = y * _silu(z_ref[...])
    y = y * lax.rsqrt(jnp.mean(y * y, axis=-1, keepdims=True) + RMS_EPS) * nw_ref[...]
    y_ref[...] = y.astype(y_ref.dtype)


def ssd_core(z, xbc, dt_raw, conv_w, conv_b, dt_bias, a_log, d_skip, norm_w):
    t = z.shape[0]
    q, gw, ns, hpg, ng = SSD_CHUNK, SSD_GW, SSD_STATE, SSD_HPG, SSD_GROUPS
    assert t % q == 0
    b_blk = D_INNER // ns
    c_blk = b_blk + ng
    dt_g = dt_raw.reshape(t, ng, hpg).transpose(1, 0, 2)
    dt_gt = dt_raw.T.reshape(ng, hpg, t)
    dtb = dt_bias.astype(F32).reshape(ng, 1, hpg)
    dtb_t = dt_bias.astype(F32).reshape(ng, hpg, 1)
    alog = a_log.astype(F32).reshape(ng, 1, hpg)
    alog_t = a_log.astype(F32).reshape(ng, hpg, 1)
    dsk = jnp.repeat(d_skip.astype(F32), SSD_HEAD_DIM).reshape(1, D_INNER)
    cw = conv_w.astype(F32)
    cbias = conv_b.astype(F32).reshape(1, SSD_CONV_DIM)
    nw = norm_w.astype(F32).reshape(1, D_INNER)

    def cspec(rows, width, blk):
        return pl.BlockSpec((rows, width), lambda g, c: (0, blk(g)))

    in_specs = [
        pl.BlockSpec((q, gw), lambda g, c: (c, g)),
        pl.BlockSpec((q, ns), lambda g, c: (c, b_blk + g)),
        pl.BlockSpec((q, ns), lambda g, c: (c, c_blk + g)),
        pl.BlockSpec((q, gw), lambda g, c: (c, g)),
        pl.BlockSpec((None, q, hpg), lambda g, c: (g, c, 0)),
        pl.BlockSpec((None, hpg, q), lambda g, c: (g, 0, c)),
        cspec(SSD_CONV, gw, lambda g: g), cspec(SSD_CONV, ns, lambda g: b_blk + g),
        cspec(SSD_CONV, ns, lambda g: c_blk + g),
        cspec(1, gw, lambda g: g), cspec(1, ns, lambda g: b_blk + g), cspec(1, ns, lambda g: c_blk + g),
        pl.BlockSpec((None, 1, hpg), lambda g, c: (g, 0, 0)),
        pl.BlockSpec((None, hpg, 1), lambda g, c: (g, 0, 0)),
        pl.BlockSpec((None, 1, hpg), lambda g, c: (g, 0, 0)),
        pl.BlockSpec((None, hpg, 1), lambda g, c: (g, 0, 0)),
        cspec(1, gw, lambda g: g), cspec(1, gw, lambda g: g),
    ]
    return pl.pallas_call(
        _ssd_kernel,
        grid=(ng, t // q),
        in_specs=in_specs,
        out_specs=pl.BlockSpec((q, gw), lambda g, c: (c, g)),
        out_shape=jax.ShapeDtypeStruct((t, D_INNER), BF16),
        scratch_shapes=[pltpu.VMEM((ns, gw), F32), pltpu.VMEM((CONV_HALO, gw), F32),
                        pltpu.VMEM((CONV_HALO, ns), F32), pltpu.VMEM((CONV_HALO, ns), F32)],
        compiler_params=_params(
            ("arbitrary", "arbitrary"),
            [2 * _nbytes((q, gw), F32), 2 * _nbytes((q, ns), F32), _nbytes((q, gw), BF16)],
            [_nbytes((ns, gw), F32), 16 * _nbytes((q, gw), F32)]),
        name="ssd_core",
    )(xbc, xbc, xbc, z, dt_g, dt_gt, cw, cw, cw, cbias, cbias, cbias,
      dtb, dtb_t, alog, alog_t, dsk, nw)


MM_TM = 1024
MM_TN = 512
WIDE_K_TM = 512
LN_TM = 256
COMBINE_TM = 128


def pool_mixer(xb, w_in, w_group, scale, w_out):
    p = pool_in(xb, w_in, tm=MM_TM, tn=MM_TN)
    mixed = pool_group_matmul(p, w_group, scale, tm=MM_TM)
    return matmul_cols(mixed, w_out, 0, D_MODEL, tm=MM_TM, tn=MM_TN, out_dtype=F32, name="pool_out")


def ssd_mixer(xb, w_in, conv_w, conv_b, dt_bias, a_log, d_skip, norm_w, w_out):
    z = matmul_cols(xb, w_in, 0, D_INNER, tm=MM_TM, tn=MM_TN, out_dtype=F32, name="ssd_in_z")
    xbc = matmul_cols(xb, w_in, D_INNER, SSD_CONV_DIM, tm=MM_TM, tn=MM_TN, out_dtype=F32,
                      name="ssd_in_xbc")
    dt_raw = matmul_cols(xb, w_in, D_INNER + SSD_CONV_DIM, SSD_HEADS, tm=MM_TM, tn=SSD_HEADS,
                         out_dtype=F32, name="ssd_in_dt")
    y = ssd_core(z, xbc, dt_raw, conv_w, conv_b, dt_bias, a_log, d_skip, norm_w)
    return matmul_bf16w(y, w_out.astype(BF16), tm=WIDE_K_TM, tn=MM_TN, out_dtype=F32, name="ssd_out")


def kernel(x, pool_w_in, pool_w_group, pool_scale, pool_w_out, ssd_w_in, ssd_conv_w, ssd_conv_b,
           ssd_dt_bias, ssd_a_log, ssd_d, ssd_norm_w, ssd_w_out, moe_w_router, moe_router_bias,
           moe_w_in, moe_w_out, ln_mix_g, ln_mix_b, ln_ffn_g, ln_ffn_b):
    bsz, seq, d = x.shape
    x = x.reshape(bsz * seq, d)
    assert bsz == 1
    xb = x.astype(BF16)
    for i in range(DEPTH):
        j = i // N_MIXERS
        if i % N_MIXERS == 0:
            h = pool_mixer(xb, pool_w_in[j], pool_w_group[j], pool_scale[j], pool_w_out[j])
        else:
            h = ssd_mixer(xb, ssd_w_in[j], ssd_conv_w[j], ssd_conv_b[j], ssd_dt_bias[j],
                          ssd_a_log[j], ssd_d[j], ssd_norm_w[j], ssd_w_out[j])
        x, xb, e_idx, gate, rank, counts = ln_router(
            x, h, ln_mix_g[i], ln_mix_b[i], moe_w_router, moe_router_bias, tm=LN_TM)
        x, xb = moe_ffn_ln(x, xb, e_idx, gate, rank, counts, moe_w_in, moe_w_out, i,
                           ln_ffn_g[i], ln_ffn_b[i], with_bf16=(i + 1 < DEPTH), tm=COMBINE_TM)
    return x.reshape(bsz, seq, d)
```

```python
import functools

import jax
import jax.numpy as jnp
from jax import lax
from jax.experimental import pallas as pl
from jax.experimental.pallas import tpu as pltpu

F32 = jnp.float32
BF16 = jnp.bfloat16
I32 = jnp.int32
U32 = jnp.uint32

D_MODEL = 4096
DEPTH = 2
N_MIXERS = 2
DEEPNORM_ALPHA = (2 * DEPTH) ** 0.25
LN_EPS = 1e-5
POOL_WINDOWS = (2, 4, 8, 16)
POOL_GROUPS = len(POOL_WINDOWS)
POOL_GC = D_MODEL // POOL_GROUPS
D_INNER = 2 * D_MODEL
SSD_HEAD_DIM = 64
SSD_HEADS = D_INNER // SSD_HEAD_DIM
SSD_GROUPS = 8
SSD_HPG = SSD_HEADS // SSD_GROUPS
SSD_STATE = 128
SSD_CONV = 4
SSD_CHUNK = 128
SSD_GW = D_INNER // SSD_GROUPS
SSD_CONV_DIM = D_INNER + 2 * SSD_GROUPS * SSD_STATE
SSD_IN_DIM = D_INNER + SSD_CONV_DIM + SSD_HEADS
RMS_EPS = 1e-5
N_EXPERTS = 32
N_EXPERT_GROUPS = 8
EXPERTS_PER_GROUP = N_EXPERTS // N_EXPERT_GROUPS
TOP_K = 2
D_FF = 768
BLOCK_ROWS = 128

LANES = 128
SUBLANES = 8
VMEM_BYTES_V7X = 64 * 1024 * 1024
VMEM_CAP = VMEM_BYTES_V7X * 7 // 8

POOL_HALO = 16
CONV_HALO = SUBLANES
EXPERT_CHUNK = 2 * BLOCK_ROWS
ROW_UNROLL = 8


def _nbytes(shape, dtype):
    n = 1
    for s in shape:
        n *= s
    return n * jnp.dtype(dtype).itemsize


def _vmem_limit(pipelined, resident):
    est = 2 * sum(pipelined) + sum(resident)
    return int(min(VMEM_CAP, est + max(est // 4, 8 * 1024 * 1024)))


def _params(semantics, pipelined, resident=()):
    return pltpu.CompilerParams(dimension_semantics=semantics,
                                vmem_limit_bytes=_vmem_limit(pipelined, resident))


def _mm_kernel(a_ref, w_ref, o_ref, wbf_ref):
    @pl.when(pl.program_id(1) == 0)
    def _():
        wbf_ref[...] = w_ref[...].astype(BF16)

    o_ref[...] = jnp.dot(a_ref[...], wbf_ref[...], preferred_element_type=F32).astype(o_ref.dtype)


def _mm_scale_kernel(a_ref, w_ref, s_ref, o_ref, wbf_ref):
    @pl.when(pl.program_id(1) == 0)
    def _():
        wbf_ref[...] = w_ref[...].astype(BF16)

    acc = jnp.dot(a_ref[...], wbf_ref[...], preferred_element_type=F32)
    o_ref[...] = (acc * s_ref[...]).astype(o_ref.dtype)


def matmul_cols(a, w, col_off, n_cols, *, tm, tn, out_dtype, name):
    m_rows, k = a.shape
    assert w.shape[0] == k and m_rows % tm == 0 and n_cols % tn == 0 and col_off % tn == 0
    off = col_off // tn
    return pl.pallas_call(
        _mm_kernel,
        grid=(n_cols // tn, m_rows // tm),
        in_specs=[pl.BlockSpec((tm, k), lambda n, m: (m, 0)),
                  pl.BlockSpec((k, tn), lambda n, m: (0, n + off))],
        out_specs=pl.BlockSpec((tm, tn), lambda n, m: (m, n)),
        out_shape=jax.ShapeDtypeStruct((m_rows, n_cols), out_dtype),
        scratch_shapes=[pltpu.VMEM((k, tn), BF16)],
        compiler_params=_params(
            ("arbitrary", "arbitrary"),
            [_nbytes((tm, k), BF16), _nbytes((k, tn), F32), _nbytes((tm, tn), out_dtype)],
            [_nbytes((k, tn), BF16), _nbytes((tm, tn), F32)]),
        name=name,
    )(a, w)


def _mm_bf16w_kernel(a_ref, w_ref, o_ref):
    o_ref[...] = jnp.dot(a_ref[...], w_ref[...], preferred_element_type=F32).astype(o_ref.dtype)


def matmul_bf16w(a, w, *, tm, tn, out_dtype, name):
    m_rows, k = a.shape
    n_cols = w.shape[1]
    assert w.shape[0] == k and m_rows % tm == 0 and n_cols % tn == 0
    return pl.pallas_call(
        _mm_bf16w_kernel,
        grid=(n_cols // tn, m_rows // tm),
        in_specs=[pl.BlockSpec((tm, k), lambda n, m: (m, 0)),
                  pl.BlockSpec((k, tn), lambda n, m: (0, n))],
        out_specs=pl.BlockSpec((tm, tn), lambda n, m: (m, n)),
        out_shape=jax.ShapeDtypeStruct((m_rows, n_cols), out_dtype),
        compiler_params=_params(
            ("arbitrary", "arbitrary"),
            [_nbytes((tm, k), BF16), _nbytes((k, tn), BF16), _nbytes((tm, tn), out_dtype)],
            [_nbytes((tm, tn), F32)]),
        name=name,
    )(a, w)


def pool_group_matmul(p, w_group, scale, *, tm):
    m_rows = p.shape[0]
    gc = POOL_GC
    assert m_rows % tm == 0
    return pl.pallas_call(
        _mm_scale_kernel,
        grid=(POOL_GROUPS, m_rows // tm),
        in_specs=[pl.BlockSpec((tm, gc), lambda g, m: (m, g)),
                  pl.BlockSpec((None, gc, gc), lambda g, m: (g, 0, 0)),
                  pl.BlockSpec((1, gc), lambda g, m: (0, g))],
        out_specs=pl.BlockSpec((tm, gc), lambda g, m: (m, g)),
        out_shape=jax.ShapeDtypeStruct((m_rows, POOL_GROUPS * gc), BF16),
        scratch_shapes=[pltpu.VMEM((gc, gc), BF16)],
        compiler_params=_params(
            ("arbitrary", "arbitrary"),
            [_nbytes((tm, gc), BF16), _nbytes((gc, gc), F32), _nbytes((tm, gc), BF16)],
            [_nbytes((gc, gc), BF16), _nbytes((tm, gc), F32)]),
        name="pool_group_matmul",
    )(p, w_group, scale.reshape(1, -1))


def _pool_in_kernel(x_ref, w_ref, p_ref, wbf_ref, ubuf_ref, *, tm, tn):
    n = pl.program_id(0)
    m = pl.program_id(1)

    @pl.when(m == 0)
    def _():
        wbf_ref[...] = w_ref[...].astype(BF16)
        ubuf_ref[0:POOL_HALO, :] = jnp.zeros((POOL_HALO, tn), F32)

    u = jnp.dot(x_ref[...], wbf_ref[...], preferred_element_type=F32)
    ubuf_ref[POOL_HALO:POOL_HALO + tm, :] = u
    pos = (m * tm + 1 + lax.broadcasted_iota(I32, (tm, tn), 0)).astype(F32)
    group = (n * tn) // POOL_GC
    for gi, window in enumerate(POOL_WINDOWS):
        @pl.when(group == gi)
        def _(window=window):
            acc = u
            for k in range(1, window):
                acc = acc + ubuf_ref[POOL_HALO - k:POOL_HALO - k + tm, :]
            mean = acc / jnp.minimum(pos, float(window))
            p_ref[...] = (mean - u).astype(p_ref.dtype)

    ubuf_ref[0:POOL_HALO, :] = ubuf_ref[tm:tm + POOL_HALO, :]


def pool_in(xb, w_in, *, tm, tn):
    m_rows, k = xb.shape
    n_cols = w_in.shape[1]
    assert m_rows % tm == 0 and n_cols % tn == 0 and POOL_GC % tn == 0 and tm >= POOL_HALO
    return pl.pallas_call(
        functools.partial(_pool_in_kernel, tm=tm, tn=tn),
        grid=(n_cols // tn, m_rows // tm),
        in_specs=[pl.BlockSpec((tm, k), lambda n, m: (m, 0)),
                  pl.BlockSpec((k, tn), lambda n, m: (0, n))],
        out_specs=pl.BlockSpec((tm, tn), lambda n, m: (m, n)),
        out_shape=jax.ShapeDtypeStruct((m_rows, n_cols), BF16),
        scratch_shapes=[pltpu.VMEM((k, tn), BF16), pltpu.VMEM((tm + POOL_HALO, tn), F32)],
        compiler_params=_params(
            ("arbitrary", "arbitrary"),
            [_nbytes((tm, k), BF16), _nbytes((k, tn), F32), _nbytes((tm, tn), BF16)],
            [_nbytes((k, tn), BF16), 4 * _nbytes((tm + POOL_HALO, tn), F32)]),
        name="pool_in",
    )(xb, w_in)


def _layer_norm_rows(v, g, b):
    mu = jnp.mean(v, axis=-1, keepdims=True)
    vc = v - mu
    var = jnp.mean(vc * vc, axis=-1, keepdims=True)
    return vc * lax.rsqrt(var + LN_EPS) * g + b


def _ln_router_kernel(x_ref, h_ref, g_ref, b_ref, wr_ref, rb_ref,
                      xo_ref, xpk_ref, e_ref, gate_ref, rank_ref, cnt_ref, base_ref, *, tm):
    i = pl.program_id(0)

    @pl.when(i == 0)
    def _():
        base_ref[...] = jnp.zeros_like(base_ref)

    y = _layer_norm_rows(DEEPNORM_ALPHA * x_ref[...] + h_ref[...], g_ref[...], b_ref[...])
    xo_ref[...] = y
    yb = y.astype(BF16)
    half = y.shape[1] // 2
    lo_bits = pltpu.bitcast(yb[:, :half].astype(F32), U32) >> 16
    hi_bits = pltpu.bitcast(yb[:, half:].astype(F32), U32)
    xpk_ref[...] = hi_bits | lo_bits

    ng, epg = N_EXPERT_GROUPS, EXPERTS_PER_GROUP
    logits = lax.dot_general(wr_ref[...], yb, (((1,), (1,)), ((), ())),
                             preferred_element_type=F32)
    scores = jax.nn.sigmoid(logits)
    sel = scores + rb_ref[...]
    sel_j = [sel[ng * j:ng * (j + 1), :] for j in range(epg)]
    sc_j = [scores[ng * j:ng * (j + 1), :] for j in range(epg)]
    gscore = None
    for j1 in range(epg):
        for j2 in range(j1 + 1, epg):
            s = sel_j[j1] + sel_j[j2]
            gscore = s if gscore is None else jnp.maximum(gscore, s)
    gmax = jnp.max(gscore, axis=0, keepdims=True)
    giota = lax.broadcasted_iota(I32, (ng, tm), 0)
    g_idx = jnp.min(jnp.where(gscore == gmax, giota, ng), axis=0, keepdims=True)
    in_g = giota == g_idx
    v = [jnp.sum(jnp.where(in_g, sel_j[j], 0.0), axis=0, keepdims=True) for j in range(epg)]
    s = [jnp.sum(jnp.where(in_g, sc_j[j], 0.0), axis=0, keepdims=True) for j in range(epg)]
    order = []
    for j in range(epg):
        r = jnp.zeros((1, tm), I32)
        for k in range(epg):
            if k == j:
                continue
            beats = (v[k] >= v[j]) if k < j else (v[k] > v[j])
            r = r + beats.astype(I32)
        order.append(r)
    eiota = lax.broadcasted_iota(I32, (N_EXPERTS, tm), 0)
    onehot = jnp.zeros((N_EXPERTS, tm), F32)
    loc, raw, rows = [], [], []
    for slot in range(TOP_K):
        lj = jnp.zeros((1, tm), I32)
        gs = jnp.zeros((1, tm), F32)
        for j in range(epg):
            hit = order[j] == slot
            lj = lj + jnp.where(hit, j, 0)
            gs = gs + jnp.where(hit, s[j], 0.0)
        row = lj * ng + g_idx
        onehot = onehot + (eiota == row).astype(F32)
        loc.append(lj)
        raw.append(gs)
        rows.append(row)
    denom = raw[0] + raw[1]
    srow = lax.broadcasted_iota(I32, (tm, tm), 0)
    scol = lax.broadcasted_iota(I32, (tm, tm), 1)
    before = (srow < scol).astype(BF16)
    prefix = jnp.dot(onehot.astype(BF16), before, preferred_element_type=F32) + base_ref[...]
    for slot in range(TOP_K):
        e_ref[slot:slot + 1, :] = g_idx * epg + loc[slot]
        gate_ref[slot:slot + 1, :] = raw[slot] / denom
        rk = jnp.sum(jnp.where(eiota == rows[slot], prefix, 0.0), axis=0, keepdims=True)
        rank_ref[slot:slot + 1, :] = rk.astype(I32)
    base_ref[...] = base_ref[...] + jnp.sum(onehot, axis=1, keepdims=True)
    cnt_ref[...] = jnp.broadcast_to(base_ref[...], cnt_ref.shape)


def ln_router(x, h, g, b, w_router, router_bias, *, tm):
    t, d = x.shape
    assert t % tm == 0
    ng, epg = N_EXPERT_GROUPS, EXPERTS_PER_GROUP
    wr = w_router.T.reshape(ng, epg, d).transpose(1, 0, 2).reshape(N_EXPERTS, d).astype(BF16)
    rb = router_bias.astype(F32).reshape(ng, epg).T.reshape(N_EXPERTS, 1)
    row = pl.BlockSpec((tm, d), lambda i: (i, 0))
    vec = pl.BlockSpec((1, d), lambda i: (0, 0))
    tok = pl.BlockSpec((TOP_K, tm), lambda i: (0, i))
    outs = pl.pallas_call(
        functools.partial(_ln_router_kernel, tm=tm),
        grid=(t // tm,),
        in_specs=[row, row, vec, vec,
                  pl.BlockSpec((N_EXPERTS, d), lambda i: (0, 0)),
                  pl.BlockSpec((N_EXPERTS, 1), lambda i: (0, 0))],
        out_specs=[row, pl.BlockSpec((tm, d // 2), lambda i: (i, 0)), tok, tok, tok,
                   pl.BlockSpec((N_EXPERTS, LANES), lambda i: (0, 0))],
        out_shape=[jax.ShapeDtypeStruct((t, d), F32), jax.ShapeDtypeStruct((t, d // 2), U32),
                   jax.ShapeDtypeStruct((TOP_K, t), I32), jax.ShapeDtypeStruct((TOP_K, t), F32),
                   jax.ShapeDtypeStruct((TOP_K, t), I32),
                   jax.ShapeDtypeStruct((N_EXPERTS, LANES), F32)],
        scratch_shapes=[pltpu.VMEM((N_EXPERTS, 1), F32)],
        compiler_params=_params(
            ("arbitrary",),
            [3 * _nbytes((tm, d), F32), _nbytes((tm, d), BF16), _nbytes((N_EXPERTS, d), BF16)],
            [4 * _nbytes((tm, d), F32)]),
        name="ln_router",
    )(x, h, g.reshape(1, d), b.reshape(1, d), wr, rb)
    x_new, xpk, e_idx, gate, rank, cnt = outs
    counts = cnt[:, 0].astype(I32).reshape(epg, ng).T.reshape(N_EXPERTS)
    return x_new, xpk, e_idx, gate, rank, counts


def _expert_in_kernel(src_ref, crow_ref, cbase_ref, nch_ref, total_ref, xpk_hbm, w_hbm, act_hbm,
                      stage, wbf, ibuf, obuf, wsem, isem, osem, *, layer):
    e = pl.program_id(0)
    last = pl.num_programs(0) - 1
    total = total_ref[0]
    half = xpk_hbm.shape[1]

    def w_copy(ex):
        return pltpu.make_async_copy(w_hbm.at[layer, ex], stage, wsem)

    def start_gather(g):
        row0 = crow_ref[g]

        def body(r8, carry):
            for u in range(ROW_UNROLL):
                r = r8 * ROW_UNROLL + u
                tok = src_ref[row0 + r]
                pltpu.make_async_copy(xpk_hbm.at[pl.ds(tok, 1)], ibuf.at[g % 2, pl.ds(r, 1)],
                                      isem.at[g % 2]).start(priority=u % 2)
            return carry
        lax.fori_loop(0, EXPERT_CHUNK // ROW_UNROLL, body, 0)

    def wait_gather(g):
        pltpu.make_async_copy(xpk_hbm.at[pl.ds(0, EXPERT_CHUNK)], ibuf.at[g % 2], isem.at[g % 2]).wait()

    def out_copy(g):
        rows = pl.ds(pl.multiple_of(crow_ref[g], BLOCK_ROWS), EXPERT_CHUNK)
        return pltpu.make_async_copy(obuf.at[g % 2], act_hbm.at[rows], osem.at[g % 2])

    @pl.when(e == 0)
    def _():
        w_copy(e).start()

        @pl.when(total > 0)
        def _():
            start_gather(0)

    w_copy(e).wait()
    wbf[...] = stage[...].astype(BF16)

    @pl.when(e < last)
    def _():
        w_copy(e + 1).start()

    base = cbase_ref[e]

    def body(i, carry):
        g = base + i

        @pl.when(g + 1 < total)
        def _():
            start_gather(g + 1)

        wait_gather(g)
        packed = ibuf[g % 2]
        x_lo = pltpu.bitcast(packed << 16, F32).astype(BF16)
        x_hi = pltpu.bitcast(packed & jnp.uint32(0xFFFF0000), F32).astype(BF16)
        h = (jnp.dot(x_lo, wbf[0:half, :], preferred_element_type=F32)
             + jnp.dot(x_hi, wbf[half:2 * half, :], preferred_element_type=F32))
        h1 = h[:, :D_FF]
        obuf[g % 2] = (h1 * jax.nn.sigmoid(h1) * h[:, D_FF:]).astype(obuf.dtype)

        @pl.when(g >= 1)
        def _():
            out_copy(g - 1).wait()

        out_copy(g).start()
        return carry

    lax.fori_loop(0, nch_ref[e], body, 0)

    @pl.when(e == last)
    def _():
        @pl.when(total > 0)
        def _():
            out_copy(total - 1).wait()

        end = jnp.where(total > 0, crow_ref[jnp.maximum(total - 1, 0)] + EXPERT_CHUNK, 0)
        obuf[0] = jnp.zeros(obuf.shape[1:], obuf.dtype)

        def tail_copy(blk):
            rows = pl.ds(pl.multiple_of(end + blk * BLOCK_ROWS, BLOCK_ROWS), BLOCK_ROWS)
            return pltpu.make_async_copy(obuf.at[0, 0:BLOCK_ROWS], act_hbm.at[rows], osem.at[0])

        n_tail = (act_hbm.shape[0] - end) // BLOCK_ROWS
        lax.fori_loop(0, n_tail, lambda blk, c: (tail_copy(blk).start(), c)[1], 0)
        lax.fori_loop(0, n_tail, lambda blk, c: (tail_copy(blk).wait(), c)[1], 0)


def expert_in(xpk, src, chunk_row, chunk_base, n_chunks, total, moe_w_in, layer):
    rows = src.shape[0]
    half = xpk.shape[1]
    wshape = moe_w_in.shape[-2:]
    resident = [_nbytes(wshape, F32), _nbytes(wshape, BF16),
                2 * _nbytes((EXPERT_CHUNK, half), U32), 2 * _nbytes((EXPERT_CHUNK, D_FF), BF16),
                _nbytes((EXPERT_CHUNK, 2 * half), BF16), 2 * _nbytes((EXPERT_CHUNK, wshape[1]), F32)]
    return pl.pallas_call(
        functools.partial(_expert_in_kernel, layer=layer),
        grid_spec=pltpu.PrefetchScalarGridSpec(
            num_scalar_prefetch=5,
            grid=(N_EXPERTS,),
            in_specs=[pl.BlockSpec(memory_space=pl.ANY), pl.BlockSpec(memory_space=pl.ANY)],
            out_specs=pl.BlockSpec(memory_space=pl.ANY),
            scratch_shapes=[pltpu.VMEM(wshape, F32), pltpu.VMEM(wshape, BF16),
                            pltpu.VMEM((2, EXPERT_CHUNK, half), U32),
                            pltpu.VMEM((2, EXPERT_CHUNK, D_FF), BF16),
                            pltpu.SemaphoreType.DMA(()), pltpu.SemaphoreType.DMA((2,)),
                            pltpu.SemaphoreType.DMA((2,))]),
        out_shape=jax.ShapeDtypeStruct((rows, D_FF), BF16),
        compiler_params=pltpu.CompilerParams(
            dimension_semantics=("arbitrary",), has_side_effects=True,
            vmem_limit_bytes=int(min(VMEM_CAP, sum(resident) + 4 * 1024 * 1024))),
        name="expert_in",
    )(src, chunk_row, chunk_base, n_chunks, total, xpk, moe_w_in)


def _expert_kernel(pstart_ref, padded_ref, in_hbm, w_hbm, out_hbm,
                   stage, wbf, ibuf, obuf, wsem, isem, osem, *, layer, gated):
    e = pl.program_id(0)

    def w_copy(ex):
        return pltpu.make_async_copy(w_hbm.at[layer, ex], stage, wsem)

    @pl.when(e == 0)
    def _():
        w_copy(e).start()

    w_copy(e).wait()
    wbf[...] = stage[...].astype(BF16)

    @pl.when(e + 1 < pl.num_programs(0))
    def _():
        w_copy(e + 1).start()

    start = pstart_ref[e]
    n_chunks = (padded_ref[e] + EXPERT_CHUNK - 1) // EXPERT_CHUNK

    def chunk_rows(i):
        return pl.ds(pl.multiple_of(start + i * EXPERT_CHUNK, BLOCK_ROWS), EXPERT_CHUNK)

    def in_copy(i):
        return pltpu.make_async_copy(in_hbm.at[chunk_rows(i)], ibuf.at[i % 2], isem.at[i % 2])

    def out_copy(i):
        return pltpu.make_async_copy(obuf.at[i % 2], out_hbm.at[chunk_rows(i)], osem.at[i % 2])

    @pl.when(n_chunks > 0)
    def _():
        in_copy(0).start()

    def body(i, carry):
        @pl.when(i + 1 < n_chunks)
        def _():
            in_copy(i + 1).start()

        in_copy(i).wait()
        h = jnp.dot(ibuf[i % 2], wbf[...], preferred_element_type=F32)
        if gated:
            half = h.shape[1] // 2
            h1 = h[:, :half]
            h = h1 * jax.nn.sigmoid(h1) * h[:, half:]
        obuf[i % 2] = h.astype(obuf.dtype)
        out_copy(i).start()

        @pl.when(i >= 1)
        def _():
            out_copy(i - 1).wait()
        return carry

    lax.fori_loop(0, n_chunks, body, 0)

    @pl.when(n_chunks > 0)
    def _():
        out_copy(n_chunks - 1).wait()

    @pl.when(e + 1 == pl.num_programs(0))
    def _():
        end = start + n_chunks * EXPERT_CHUNK
        obuf[0] = jnp.zeros(obuf.shape[1:], obuf.dtype)

        def tail_copy(blk):
            rows = pl.ds(pl.multiple_of(end + blk * BLOCK_ROWS, BLOCK_ROWS), BLOCK_ROWS)
            return pltpu.make_async_copy(obuf.at[0, 0:BLOCK_ROWS], out_hbm.at[rows], osem.at[0])

        n_tail = (out_hbm.shape[0] - end) // BLOCK_ROWS
        lax.fori_loop(0, n_tail, lambda blk, c: (tail_copy(blk).start(), c)[1], 0)
        lax.fori_loop(0, n_tail, lambda blk, c: (tail_copy(blk).wait(), c)[1], 0)


def _expert_call(rows_in, w, pad_start, padded, layer, *, gated, out_dtype, name):
    rows, k = rows_in.shape
    n_out = w.shape[-1] // 2 if gated else w.shape[-1]
    wshape = w.shape[-2:]
    resident = [_nbytes(wshape, F32), _nbytes(wshape, BF16),
                2 * _nbytes((EXPERT_CHUNK, k), rows_in.dtype),
                2 * _nbytes((EXPERT_CHUNK, n_out), out_dtype),
                2 * _nbytes((EXPERT_CHUNK, wshape[1]), F32)]
    return pl.pallas_call(
        functools.partial(_expert_kernel, layer=layer, gated=gated),
        grid_spec=pltpu.PrefetchScalarGridSpec(
            num_scalar_prefetch=2,
            grid=(N_EXPERTS,),
            in_specs=[pl.BlockSpec(memory_space=pl.ANY), pl.BlockSpec(memory_space=pl.ANY)],
            out_specs=pl.BlockSpec(memory_space=pl.ANY),
            scratch_shapes=[pltpu.VMEM(wshape, F32), pltpu.VMEM(wshape, BF16),
                            pltpu.VMEM((2, EXPERT_CHUNK, k), rows_in.dtype),
                            pltpu.VMEM((2, EXPERT_CHUNK, n_out), out_dtype),
                            pltpu.SemaphoreType.DMA(()), pltpu.SemaphoreType.DMA((2,)),
                            pltpu.SemaphoreType.DMA((2,))]),
        out_shape=jax.ShapeDtypeStruct((rows, n_out), out_dtype),
        compiler_params=pltpu.CompilerParams(
            dimension_semantics=("arbitrary",), has_side_effects=True,
            vmem_limit_bytes=int(min(VMEM_CAP, sum(resident) + 4 * 1024 * 1024))),
        name=name,
    )(pad_start, padded, rows_in, w)


def expert_out(act, pad_start, padded, moe_w_out, layer):
    return _expert_call(act, moe_w_out, pad_start, padded, layer, gated=False, out_dtype=F32,
                        name="expert_out")


def _combine_ln_kernel(dest_ref, x_ref, gate_ref, g_ref, b_ref, y_hbm, *rest, tm, t, with_bf16):
    if with_bf16:
        xo_ref, xb_ref, ybuf, sem = rest
    else:
        xo_ref, ybuf, sem = rest
        xb_ref = None
    i = pl.program_id(0)
    n = pl.num_programs(0)

    def start_rows(step):
        buf = step % 2

        def body(r8, carry):
            for u in range(ROW_UNROLL):
                r = r8 * ROW_UNROLL + u
                for k in range(TOP_K):
                    d = dest_ref[k * t + step * tm + r]
                    pltpu.make_async_copy(y_hbm.at[pl.ds(d, 1)], ybuf.at[buf, k, pl.ds(r, 1)],
                                          sem.at[buf]).start(priority=(u + k) % 2)
            return carry
        lax.fori_loop(0, tm // ROW_UNROLL, body, 0)

    def wait_rows(step):
        buf = step % 2
        for k in range(TOP_K):
            pltpu.make_async_copy(y_hbm.at[pl.ds(0, tm)], ybuf.at[buf, k], sem.at[buf]).wait()

    @pl.when(i == 0)
    def _():
        start_rows(i)

    @pl.when(i + 1 < n)
    def _():
        start_rows(i + 1)

    wait_rows(i)
    buf = i % 2
    gate = gate_ref[...]
    h = gate[:, 0:1] * ybuf[buf, 0] + gate[:, 1:2] * ybuf[buf, 1]
    y = _layer_norm_rows(DEEPNORM_ALPHA * x_ref[...] + h, g_ref[...], b_ref[...])
    xo_ref[...] = y
    if with_bf16:
        xb_ref[...] = y.astype(BF16)


def combine_ln(x, y_disp, dest_flat, gate_t, g, b, *, tm, with_bf16):
    t, d = x.shape
    assert t % tm == 0
    row = pl.BlockSpec((tm, d), lambda i, dest: (i, 0))
    vec = pl.BlockSpec((1, d), lambda i, dest: (0, 0))
    out_specs = [row, row] if with_bf16 else [row]
    out_shape = [jax.ShapeDtypeStruct((t, d), F32)]
    if with_bf16:
        out_shape.append(jax.ShapeDtypeStruct((t, d), BF16))
    outs = pl.pallas_call(
        functools.partial(_combine_ln_kernel, tm=tm, t=t, with_bf16=with_bf16),
        grid_spec=pltpu.PrefetchScalarGridSpec(
            num_scalar_prefetch=1,
            grid=(t // tm,),
            in_specs=[row, pl.BlockSpec((tm, TOP_K), lambda i, dest: (i, 0)), vec, vec,
                      pl.BlockSpec(memory_space=pl.ANY)],
            out_specs=out_specs,
            scratch_shapes=[pltpu.VMEM((2, TOP_K, tm, d), F32), pltpu.SemaphoreType.DMA((2,))]),
        out_shape=out_shape,
        compiler_params=_params(
            ("arbitrary",),
            [2 * _nbytes((tm, d), F32), _nbytes((tm, d), BF16)],
            [_nbytes((2, TOP_K, tm, d), F32), 4 * _nbytes((tm, d), F32)]),
        name="moe_combine_ln",
    )(dest_flat, x, gate_t, g.reshape(1, d), b.reshape(1, d), y_disp)
    return (outs[0], outs[1]) if with_bf16 else (outs[0], None)


def moe_ffn_ln(x, xpk, e_idx, gate, rank, counts, moe_w_in, moe_w_out, layer, g, b, *,
               with_bf16, tm):
    t, d = x.shape
    tk = t * TOP_K
    padded = ((counts + BLOCK_ROWS - 1) // BLOCK_ROWS * BLOCK_ROWS).astype(I32)
    pad_end = jnp.cumsum(padded).astype(I32)
    pad_start = pad_end - padded
    n_blocks = (tk + N_EXPERTS * (BLOCK_ROWS - 1) + BLOCK_ROWS - 1) // BLOCK_ROWS + 1
    e_flat = e_idx.reshape(tk)
    dest_flat = pad_start[e_flat] + rank.reshape(tk)
    tok_flat = jnp.arange(tk, dtype=I32) % t
    src = jnp.zeros((n_blocks * BLOCK_ROWS,), I32).at[dest_flat].set(tok_flat)
    n_chunks = (padded + EXPERT_CHUNK - 1) // EXPERT_CHUNK
    chunk_end = jnp.cumsum(n_chunks).astype(I32)
    chunk_base = chunk_end - n_chunks
    max_chunks = n_blocks // (EXPERT_CHUNK // BLOCK_ROWS) + N_EXPERTS
    gidx = jnp.arange(max_chunks, dtype=I32)
    owner = jnp.minimum(jnp.sum(gidx[:, None] >= chunk_end[None, :], axis=1), N_EXPERTS - 1)
    chunk_row = (pad_start[owner] + (gidx - chunk_base[owner]) * EXPERT_CHUNK).astype(I32)
    act = expert_in(xpk, src, chunk_row, chunk_base, n_chunks.astype(I32), chunk_end[-1:],
                    moe_w_in, layer)
    y_disp = expert_out(act, pad_start, padded, moe_w_out, layer)
    return combine_ln(x, y_disp, dest_flat, gate.T, g, b, tm=tm, with_bf16=with_bf16)


def _softplus(x):
    return jnp.maximum(x, 0.0) + jnp.log1p(jnp.exp(-jnp.abs(x)))


def _silu(x):
    return x * jax.nn.sigmoid(x)


def _dot_01_f32(m01, v, *, ones_left):
    hi = v.astype(BF16)
    r = v - hi.astype(F32)
    mid = r.astype(BF16)
    lo = (r - mid.astype(F32)).astype(BF16)

    def dot(p):
        lhs, rhs = (m01, p) if ones_left else (p, m01)
        return jnp.dot(lhs, rhs, preferred_element_type=F32)
    return dot(hi) + dot(mid) + dot(lo)


def _ssd_kernel(xs_ref, b_ref, c_ref, z_ref, dt_ref, dtT_ref,
                cwx_ref, cwb_ref, cwc_ref, cbx_ref, cbb_ref, cbc_ref,
                dtb_ref, dtbT_ref, alog_ref, alogT_ref, dsk_ref, nw_ref,
                y_ref, state_ref, hx_ref, hb_ref, hc_ref):
    q = SSD_CHUNK

    @pl.when(pl.program_id(1) == 0)
    def _():
        state_ref[...] = jnp.zeros_like(state_ref)
        hx_ref[...] = jnp.zeros_like(hx_ref)
        hb_ref[...] = jnp.zeros_like(hb_ref)
        hc_ref[...] = jnp.zeros_like(hc_ref)

    def conv_silu(raw_ref, halo_ref, w_ref, bias_ref):
        raw = raw_ref[...]
        ext = jnp.concatenate([halo_ref[...], raw], axis=0)
        w = w_ref[...]
        acc = bias_ref[...] + w[SSD_CONV - 1:SSD_CONV, :] * raw
        for k in range(SSD_CONV - 1):
            lo = CONV_HALO - (SSD_CONV - 1) + k
            acc = acc + w[k:k + 1, :] * ext[lo:lo + q, :]
        halo_ref[...] = raw[q - CONV_HALO:q, :]
        return _silu(acc)

    xs = conv_silu(xs_ref, hx_ref, cwx_ref, cbx_ref)
    bm = conv_silu(b_ref, hb_ref, cwb_ref, cbb_ref)
    cm = conv_silu(c_ref, hc_ref, cwc_ref, cbc_ref)

    dt = _softplus(dt_ref[...] + dtb_ref[...])
    dt_t = _softplus(dtT_ref[...] + dtbT_ref[...])
    a = -jnp.exp(alog_ref[...])
    a_t = -jnp.exp(alogT_ref[...])
    row = lax.broadcasted_iota(I32, (q, q), 0)
    col = lax.broadcasted_iota(I32, (q, q), 1)
    causal = row >= col
    a_cum = _dot_01_f32(jnp.where(causal, 1.0, 0.0).astype(BF16), dt * a, ones_left=True)
    a_cum_t = _dot_01_f32(jnp.where(row <= col, 1.0, 0.0).astype(BF16), dt_t * a_t,
                          ones_left=False)
    a_last = a_cum[q - 1:q, :]
    hrow = lax.broadcasted_iota(I32, (SSD_HPG, SSD_GW), 0)
    hcol = lax.broadcasted_iota(I32, (SSD_HPG, SSD_GW), 1)
    expand = jnp.where(hcol // SSD_HEAD_DIM == hrow, 1.0, 0.0).astype(BF16)
    decay_in = _dot_01_f32(expand, jnp.exp(a_cum), ones_left=False)
    w_state = _dot_01_f32(expand, jnp.exp(a_last - a_cum) * dt, ones_left=False)
    chunk_decay = decay_in[q - 1:q, :]

    cmb = cm.astype(BF16)
    bmb = bm.astype(BF16)
    cb = lax.dot_general(cmb, bmb, (((1,), (1,)), ((), ())), preferred_element_type=F32)
    prev = state_ref[...]
    y_off = jnp.dot(cmb, prev.astype(BF16), preferred_element_type=F32) * decay_in
    xw = (w_state * xs).astype(BF16)
    state_ref[...] = chunk_decay * prev + jnp.dot(bm.T.astype(BF16), xw, preferred_element_type=F32)

    lane = lax.broadcasted_iota(I32, (q, LANES), 1)
    heads_per_tile = LANES // SSD_HEAD_DIM
    ys = []
    for tile in range(SSD_GW // LANES):
        ms = []
        for hh in range(heads_per_tile):
            h = tile * heads_per_tile + hh
            seg = a_cum[:, h:h + 1] - a_cum_t[h:h + 1, :]
            decay = jnp.where(causal, jnp.exp(seg), 0.0)
            ms.append((cb * decay * dt_t[h:h + 1, :]).astype(BF16))
        x_tile = xs[:, tile * LANES:(tile + 1) * LANES]
        rhs = jnp.concatenate(
            [jnp.where(lane // SSD_HEAD_DIM == hh, x_tile, 0.0).astype(BF16)
             for hh in range(heads_per_tile)], axis=0)
        ys.append(jnp.dot(jnp.concatenate(ms, axis=1), rhs, preferred_element_type=F32))
    y = jnp.concatenate(ys, axis=1) + y_off + dsk_ref[...] * xs
    y = y * _silu(z_ref[...])
    y = y * lax.rsqrt(jnp.mean(y * y, axis=-1, keepdims=True) + RMS_EPS) * nw_ref[...]
    y_ref[...] = y.astype(y_ref.dtype)


def ssd_core(z, xbc, dt_raw, conv_w, conv_b, dt_bias, a_log, d_skip, norm_w):
    t = z.shape[0]
    q, gw, ns, hpg, ng = SSD_CHUNK, SSD_GW, SSD_STATE, SSD_HPG, SSD_GROUPS
    assert t % q == 0
    b_blk = D_INNER // ns
    c_blk = b_blk + ng
    dt_g = dt_raw.reshape(t, ng, hpg).transpose(1, 0, 2)
    dt_gt = dt_raw.T.reshape(ng, hpg, t)
    dtb = dt_bias.astype(F32).reshape(ng, 1, hpg)
    dtb_t = dt_bias.astype(F32).reshape(ng, hpg, 1)
    alog = a_log.astype(F32).reshape(ng, 1, hpg)
    alog_t = a_log.astype(F32).reshape(ng, hpg, 1)
    dsk = jnp.repeat(d_skip.astype(F32), SSD_HEAD_DIM).reshape(1, D_INNER)
    cw = conv_w.astype(F32)
    cbias = conv_b.astype(F32).reshape(1, SSD_CONV_DIM)
    nw = norm_w.astype(F32).reshape(1, D_INNER)

    def cspec(rows, width, blk):
        return pl.BlockSpec((rows, width), lambda g, c: (0, blk(g)))

    in_specs = [
        pl.BlockSpec((q, gw), lambda g, c: (c, g)),
        pl.BlockSpec((q, ns), lambda g, c: (c, b_blk + g)),
        pl.BlockSpec((q, ns), lambda g, c: (c, c_blk + g)),
        pl.BlockSpec((q, gw), lambda g, c: (c, g)),
        pl.BlockSpec((None, q, hpg), lambda g, c: (g, c, 0)),
        pl.BlockSpec((None, hpg, q), lambda g, c: (g, 0, c)),
        cspec(SSD_CONV, gw, lambda g: g), cspec(SSD_CONV, ns, lambda g: b_blk + g),
        cspec(SSD_CONV, ns, lambda g: c_blk + g),
        cspec(1, gw, lambda g: g), cspec(1, ns, lambda g: b_blk + g), cspec(1, ns, lambda g: c_blk + g),
        pl.BlockSpec((None, 1, hpg), lambda g, c: (g, 0, 0)),
        pl.BlockSpec((None, hpg, 1), lambda g, c: (g, 0, 0)),
        pl.BlockSpec((None, 1, hpg), lambda g, c: (g, 0, 0)),
        pl.BlockSpec((None, hpg, 1), lambda g, c: (g, 0, 0)),
        cspec(1, gw, lambda g: g), cspec(1, gw, lambda g: g),
    ]
    return pl.pallas_call(
        _ssd_kernel,
        grid=(ng, t // q),
        in_specs=in_specs,
        out_specs=pl.BlockSpec((q, gw), lambda g, c: (c, g)),
        out_shape=jax.ShapeDtypeStruct((t, D_INNER), BF16),
        scratch_shapes=[pltpu.VMEM((ns, gw), F32), pltpu.VMEM((CONV_HALO, gw), F32),
                        pltpu.VMEM((CONV_HALO, ns), F32), pltpu.VMEM((CONV_HALO, ns), F32)],
        compiler_params=_params(
            ("arbitrary", "arbitrary"),
            [2 * _nbytes((q, gw), F32), 2 * _nbytes((q, ns), F32), _nbytes((q, gw), BF16)],
            [_nbytes((ns, gw), F32), 16 * _nbytes((q, gw), F32)]),
        name="ssd_core",
    )(xbc, xbc, xbc, z, dt_g, dt_gt, cw, cw, cw, cbias, cbias, cbias,
      dtb, dtb_t, alog, alog_t, dsk, nw)


MM_TM = 1024
MM_TN = 512
WIDE_K_TM = 512
LN_TM = 256
COMBINE_TM = 128


def pool_mixer(xb, w_in, w_group, scale, w_out):
    p = pool_in(xb, w_in, tm=MM_TM, tn=MM_TN)
    mixed = pool_group_matmul(p, w_group, scale, tm=MM_TM)
    return matmul_cols(mixed, w_out, 0, D_MODEL, tm=MM_TM, tn=MM_TN, out_dtype=F32, name="pool_out")


def ssd_mixer(xb, w_in, conv_w, conv_b, dt_bias, a_log, d_skip, norm_w, w_out):
    z = matmul_cols(xb, w_in, 0, D_INNER, tm=MM_TM, tn=MM_TN, out_dtype=F32, name="ssd_in_z")
    xbc = matmul_cols(xb, w_in, D_INNER, SSD_CONV_DIM, tm=MM_TM, tn=MM_TN, out_dtype=F32,
                      name="ssd_in_xbc")
    dt_raw = matmul_cols(xb, w_in, D_INNER + SSD_CONV_DIM, SSD_HEADS, tm=MM_TM, tn=SSD_HEADS,
                         out_dtype=F32, name="ssd_in_dt")
    y = ssd_core(z, xbc, dt_raw, conv_w, conv_b, dt_bias, a_log, d_skip, norm_w)
    return matmul_bf16w(y, w_out.astype(BF16), tm=WIDE_K_TM, tn=MM_TN, out_dtype=F32, name="ssd_out")


def kernel(x, pool_w_in, pool_w_group, pool_scale, pool_w_out, ssd_w_in, ssd_conv_w, ssd_conv_b,
           ssd_dt_bias, ssd_a_log, ssd_d, ssd_norm_w, ssd_w_out, moe_w_router, moe_router_bias,
           moe_w_in, moe_w_out, ln_mix_g, ln_mix_b, ln_ffn_g, ln_ffn_b):
    bsz, seq, d = x.shape
    x = x.reshape(bsz * seq, d)
    assert bsz == 1
    xb = x.astype(BF16)
    for i in range(DEPTH):
        j = i // N_MIXERS
        if i % N_MIXERS == 0:
            h = pool_mixer(xb, pool_w_in[j], pool_w_group[j], pool_scale[j], pool_w_out[j])
        else:
            h = ssd_mixer(xb, ssd_w_in[j], ssd_conv_w[j], ssd_conv_b[j], ssd_dt_bias[j],
                          ssd_a_log[j], ssd_d[j], ssd_norm_w[j], ssd_w_out[j])
        x, xpk, e_idx, gate, rank, counts = ln_router(
            x, h, ln_mix_g[i], ln_mix_b[i], moe_w_router, moe_router_bias, tm=LN_TM)
        x, xb = moe_ffn_ln(x, xpk, e_idx, gate, rank, counts, moe_w_in, moe_w_out, i,
                           ln_ffn_g[i], ln_ffn_b[i], with_bf16=(i + 1 < DEPTH), tm=COMBINE_TM)
    return x.reshape(bsz, seq, d)
```

```python
import functools

import jax
import jax.numpy as jnp
from jax import lax
from jax.experimental import pallas as pl
from jax.experimental.pallas import tpu as pltpu

F32 = jnp.float32
BF16 = jnp.bfloat16
I32 = jnp.int32
U32 = jnp.uint32

D_MODEL = 4096
DEPTH = 2
N_MIXERS = 2
DEEPNORM_ALPHA = (2 * DEPTH) ** 0.25
LN_EPS = 1e-5
POOL_WINDOWS = (2, 4, 8, 16)
POOL_GROUPS = len(POOL_WINDOWS)
POOL_GC = D_MODEL // POOL_GROUPS
D_INNER = 2 * D_MODEL
SSD_HEAD_DIM = 64
SSD_HEADS = D_INNER // SSD_HEAD_DIM
SSD_GROUPS = 8
SSD_HPG = SSD_HEADS // SSD_GROUPS
SSD_STATE = 128
SSD_CONV = 4
SSD_CHUNK = 128
SSD_GW = D_INNER // SSD_GROUPS
SSD_CONV_DIM = D_INNER + 2 * SSD_GROUPS * SSD_STATE
SSD_IN_DIM = D_INNER + SSD_CONV_DIM + SSD_HEADS
RMS_EPS = 1e-5
N_EXPERTS = 32
N_EXPERT_GROUPS = 8
EXPERTS_PER_GROUP = N_EXPERTS // N_EXPERT_GROUPS
TOP_K = 2
D_FF = 768
BLOCK_ROWS = 128

LANES = 128
SUBLANES = 8
VMEM_BYTES_V7X = 64 * 1024 * 1024
VMEM_CAP = VMEM_BYTES_V7X * 7 // 8

POOL_HALO = 16
CONV_HALO = SUBLANES
EXPERT_CHUNK = 2 * BLOCK_ROWS
ROW_UNROLL = 8
WEIGHT_DMA_PARTS = 4


def _nbytes(shape, dtype):
    n = 1
    for s in shape:
        n *= s
    return n * jnp.dtype(dtype).itemsize


def _vmem_limit(pipelined, resident):
    est = 2 * sum(pipelined) + sum(resident)
    return int(min(VMEM_CAP, est + max(est // 4, 8 * 1024 * 1024)))


def _params(semantics, pipelined, resident=()):
    return pltpu.CompilerParams(dimension_semantics=semantics,
                                vmem_limit_bytes=_vmem_limit(pipelined, resident))


def _mm_kernel(a_ref, w_ref, o_ref, wbf_ref):
    @pl.when(pl.program_id(1) == 0)
    def _():
        wbf_ref[...] = w_ref[...].astype(BF16)

    o_ref[...] = jnp.dot(a_ref[...], wbf_ref[...], preferred_element_type=F32).astype(o_ref.dtype)


def _mm_scale_kernel(a_ref, w_ref, s_ref, o_ref, wbf_ref):
    @pl.when(pl.program_id(1) == 0)
    def _():
        wbf_ref[...] = w_ref[...].astype(BF16)

    acc = jnp.dot(a_ref[...], wbf_ref[...], preferred_element_type=F32)
    o_ref[...] = (acc * s_ref[...]).astype(o_ref.dtype)


def matmul_cols(a, w, col_off, n_cols, *, tm, tn, out_dtype, name):
    m_rows, k = a.shape
    assert w.shape[0] == k and m_rows % tm == 0 and n_cols % tn == 0 and col_off % tn == 0
    off = col_off // tn
    return pl.pallas_call(
        _mm_kernel,
        grid=(n_cols // tn, m_rows // tm),
        in_specs=[pl.BlockSpec((tm, k), lambda n, m: (m, 0)),
                  pl.BlockSpec((k, tn), lambda n, m: (0, n + off))],
        out_specs=pl.BlockSpec((tm, tn), lambda n, m: (m, n)),
        out_shape=jax.ShapeDtypeStruct((m_rows, n_cols), out_dtype),
        scratch_shapes=[pltpu.VMEM((k, tn), BF16)],
        compiler_params=_params(
            ("arbitrary", "arbitrary"),
            [_nbytes((tm, k), BF16), _nbytes((k, tn), F32), _nbytes((tm, tn), out_dtype)],
            [_nbytes((k, tn), BF16), _nbytes((tm, tn), F32)]),
        name=name,
    )(a, w)


def _mm_bf16w_kernel(a_ref, w_ref, o_ref):
    o_ref[...] = jnp.dot(a_ref[...], w_ref[...], preferred_element_type=F32).astype(o_ref.dtype)


def matmul_bf16w(a, w, *, tm, tn, out_dtype, name):
    m_rows, k = a.shape
    n_cols = w.shape[1]
    assert w.shape[0] == k and m_rows % tm == 0 and n_cols % tn == 0
    return pl.pallas_call(
        _mm_bf16w_kernel,
        grid=(n_cols // tn, m_rows // tm),
        in_specs=[pl.BlockSpec((tm, k), lambda n, m: (m, 0)),
                  pl.BlockSpec((k, tn), lambda n, m: (0, n))],
        out_specs=pl.BlockSpec((tm, tn), lambda n, m: (m, n)),
        out_shape=jax.ShapeDtypeStruct((m_rows, n_cols), out_dtype),
        compiler_params=_params(
            ("arbitrary", "arbitrary"),
            [_nbytes((tm, k), BF16), _nbytes((k, tn), BF16), _nbytes((tm, tn), out_dtype)],
            [_nbytes((tm, tn), F32)]),
        name=name,
    )(a, w)


def pool_group_matmul(p, w_group, scale, *, tm):
    m_rows = p.shape[0]
    gc = POOL_GC
    assert m_rows % tm == 0
    return pl.pallas_call(
        _mm_scale_kernel,
        grid=(POOL_GROUPS, m_rows // tm),
        in_specs=[pl.BlockSpec((tm, gc), lambda g, m: (m, g)),
                  pl.BlockSpec((None, gc, gc), lambda g, m: (g, 0, 0)),
                  pl.BlockSpec((1, gc), lambda g, m: (0, g))],
        out_specs=pl.BlockSpec((tm, gc), lambda g, m: (m, g)),
        out_shape=jax.ShapeDtypeStruct((m_rows, POOL_GROUPS * gc), BF16),
        scratch_shapes=[pltpu.VMEM((gc, gc), BF16)],
        compiler_params=_params(
            ("arbitrary", "arbitrary"),
            [_nbytes((tm, gc), BF16), _nbytes((gc, gc), F32), _nbytes((tm, gc), BF16)],
            [_nbytes((gc, gc), BF16), _nbytes((tm, gc), F32)]),
        name="pool_group_matmul",
    )(p, w_group, scale.reshape(1, -1))


def _pool_in_kernel(x_ref, w_ref, p_ref, wbf_ref, ubuf_ref, *, tm, tn):
    n = pl.program_id(0)
    m = pl.program_id(1)

    @pl.when(m == 0)
    def _():
        wbf_ref[...] = w_ref[...].astype(BF16)
        ubuf_ref[0:POOL_HALO, :] = jnp.zeros((POOL_HALO, tn), F32)

    u = jnp.dot(x_ref[...], wbf_ref[...], preferred_element_type=F32)
    ubuf_ref[POOL_HALO:POOL_HALO + tm, :] = u
    pos = (m * tm + 1 + lax.broadcasted_iota(I32, (tm, tn), 0)).astype(F32)
    group = (n * tn) // POOL_GC
    for gi, window in enumerate(POOL_WINDOWS):
        @pl.when(group == gi)
        def _(window=window):
            acc = u
            for k in range(1, window):
                acc = acc + ubuf_ref[POOL_HALO - k:POOL_HALO - k + tm, :]
            mean = acc / jnp.minimum(pos, float(window))
            p_ref[...] = (mean - u).astype(p_ref.dtype)

    ubuf_ref[0:POOL_HALO, :] = ubuf_ref[tm:tm + POOL_HALO, :]


def pool_in(xb, w_in, *, tm, tn):
    m_rows, k = xb.shape
    n_cols = w_in.shape[1]
    assert m_rows % tm == 0 and n_cols % tn == 0 and POOL_GC % tn == 0 and tm >= POOL_HALO
    return pl.pallas_call(
        functools.partial(_pool_in_kernel, tm=tm, tn=tn),
        grid=(n_cols // tn, m_rows // tm),
        in_specs=[pl.BlockSpec((tm, k), lambda n, m: (m, 0)),
                  pl.BlockSpec((k, tn), lambda n, m: (0, n))],
        out_specs=pl.BlockSpec((tm, tn), lambda n, m: (m, n)),
        out_shape=jax.ShapeDtypeStruct((m_rows, n_cols), BF16),
        scratch_shapes=[pltpu.VMEM((k, tn), BF16), pltpu.VMEM((tm + POOL_HALO, tn), F32)],
        compiler_params=_params(
            ("arbitrary", "arbitrary"),
            [_nbytes((tm, k), BF16), _nbytes((k, tn), F32), _nbytes((tm, tn), BF16)],
            [_nbytes((k, tn), BF16), 4 * _nbytes((tm + POOL_HALO, tn), F32)]),
        name="pool_in",
    )(xb, w_in)


def _layer_norm_rows(v, g, b):
    mu = jnp.mean(v, axis=-1, keepdims=True)
    vc = v - mu
    var = jnp.mean(vc * vc, axis=-1, keepdims=True)
    return vc * lax.rsqrt(var + LN_EPS) * g + b


def _ln_router_kernel(x_ref, h_ref, g_ref, b_ref, wr_ref, rb_ref,
                      xo_ref, xpk_ref, e_ref, gate_ref, rank_ref, cnt_ref, base_ref, *, tm):
    i = pl.program_id(0)

    @pl.when(i == 0)
    def _():
        base_ref[...] = jnp.zeros_like(base_ref)

    y = _layer_norm_rows(DEEPNORM_ALPHA * x_ref[...] + h_ref[...], g_ref[...], b_ref[...])
    xo_ref[...] = y
    yb = y.astype(BF16)
    half = y.shape[1] // 2
    lo_bits = pltpu.bitcast(yb[:, :half].astype(F32), U32) >> 16
    hi_bits = pltpu.bitcast(yb[:, half:].astype(F32), U32)
    xpk_ref[...] = hi_bits | lo_bits

    ng, epg = N_EXPERT_GROUPS, EXPERTS_PER_GROUP
    logits = lax.dot_general(wr_ref[...], yb, (((1,), (1,)), ((), ())),
                             preferred_element_type=F32)
    scores = jax.nn.sigmoid(logits)
    sel = scores + rb_ref[...]
    sel_j = [sel[ng * j:ng * (j + 1), :] for j in range(epg)]
    sc_j = [scores[ng * j:ng * (j + 1), :] for j in range(epg)]
    gscore = None
    for j1 in range(epg):
        for j2 in range(j1 + 1, epg):
            s = sel_j[j1] + sel_j[j2]
            gscore = s if gscore is None else jnp.maximum(gscore, s)
    gmax = jnp.max(gscore, axis=0, keepdims=True)
    giota = lax.broadcasted_iota(I32, (ng, tm), 0)
    g_idx = jnp.min(jnp.where(gscore == gmax, giota, ng), axis=0, keepdims=True)
    in_g = giota == g_idx
    v = [jnp.sum(jnp.where(in_g, sel_j[j], 0.0), axis=0, keepdims=True) for j in range(epg)]
    s = [jnp.sum(jnp.where(in_g, sc_j[j], 0.0), axis=0, keepdims=True) for j in range(epg)]
    order = []
    for j in range(epg):
        r = jnp.zeros((1, tm), I32)
        for k in range(epg):
            if k == j:
                continue
            beats = (v[k] >= v[j]) if k < j else (v[k] > v[j])
            r = r + beats.astype(I32)
        order.append(r)
    eiota = lax.broadcasted_iota(I32, (N_EXPERTS, tm), 0)
    onehot = jnp.zeros((N_EXPERTS, tm), F32)
    loc, raw, rows = [], [], []
    for slot in range(TOP_K):
        lj = jnp.zeros((1, tm), I32)
        gs = jnp.zeros((1, tm), F32)
        for j in range(epg):
            hit = order[j] == slot
            lj = lj + jnp.where(hit, j, 0)
            gs = gs + jnp.where(hit, s[j], 0.0)
        row = lj * ng + g_idx
        onehot = onehot + (eiota == row).astype(F32)
        loc.append(lj)
        raw.append(gs)
        rows.append(row)
    denom = raw[0] + raw[1]
    srow = lax.broadcasted_iota(I32, (tm, tm), 0)
    scol = lax.broadcasted_iota(I32, (tm, tm), 1)
    before = (srow < scol).astype(BF16)
    prefix = jnp.dot(onehot.astype(BF16), before, preferred_element_type=F32) + base_ref[...]
    for slot in range(TOP_K):
        e_ref[slot:slot + 1, :] = g_idx * epg + loc[slot]
        gate_ref[slot:slot + 1, :] = raw[slot] / denom
        rk = jnp.sum(jnp.where(eiota == rows[slot], prefix, 0.0), axis=0, keepdims=True)
        rank_ref[slot:slot + 1, :] = rk.astype(I32)
    base_ref[...] = base_ref[...] + jnp.sum(onehot, axis=1, keepdims=True)
    cnt_ref[...] = jnp.broadcast_to(base_ref[...], cnt_ref.shape)


def ln_router(x, h, g, b, w_router, router_bias, *, tm):
    t, d = x.shape
    assert t % tm == 0
    ng, epg = N_EXPERT_GROUPS, EXPERTS_PER_GROUP
    wr = w_router.T.reshape(ng, epg, d).transpose(1, 0, 2).reshape(N_EXPERTS, d).astype(BF16)
    rb = router_bias.astype(F32).reshape(ng, epg).T.reshape(N_EXPERTS, 1)
    row = pl.BlockSpec((tm, d), lambda i: (i, 0))
    vec = pl.BlockSpec((1, d), lambda i: (0, 0))
    tok = pl.BlockSpec((TOP_K, tm), lambda i: (0, i))
    outs = pl.pallas_call(
        functools.partial(_ln_router_kernel, tm=tm),
        grid=(t // tm,),
        in_specs=[row, row, vec, vec,
                  pl.BlockSpec((N_EXPERTS, d), lambda i: (0, 0)),
                  pl.BlockSpec((N_EXPERTS, 1), lambda i: (0, 0))],
        out_specs=[row, pl.BlockSpec((tm, d // 2), lambda i: (i, 0)), tok, tok, tok,
                   pl.BlockSpec((N_EXPERTS, LANES), lambda i: (0, 0))],
        out_shape=[jax.ShapeDtypeStruct((t, d), F32), jax.ShapeDtypeStruct((t, d // 2), U32),
                   jax.ShapeDtypeStruct((TOP_K, t), I32), jax.ShapeDtypeStruct((TOP_K, t), F32),
                   jax.ShapeDtypeStruct((TOP_K, t), I32),
                   jax.ShapeDtypeStruct((N_EXPERTS, LANES), F32)],
        scratch_shapes=[pltpu.VMEM((N_EXPERTS, 1), F32)],
        compiler_params=_params(
            ("arbitrary",),
            [3 * _nbytes((tm, d), F32), _nbytes((tm, d), BF16), _nbytes((N_EXPERTS, d), BF16)],
            [4 * _nbytes((tm, d), F32)]),
        name="ln_router",
    )(x, h, g.reshape(1, d), b.reshape(1, d), wr, rb)
    x_new, xpk, e_idx, gate, rank, cnt = outs
    counts = cnt[:, 0].astype(I32).reshape(epg, ng).T.reshape(N_EXPERTS)
    return x_new, xpk, e_idx, gate, rank, counts


def _weight_copies(w_hbm, layer, ex, stage, wsem):
    rows = stage.shape[0] // WEIGHT_DMA_PARTS
    assert rows * WEIGHT_DMA_PARTS == stage.shape[0] and rows % SUBLANES == 0
    return [pltpu.make_async_copy(w_hbm.at[layer, ex, pl.ds(p * rows, rows)],
                                  stage.at[pl.ds(p * rows, rows)], wsem.at[p])
            for p in range(WEIGHT_DMA_PARTS)]


def _start_all(copies):
    for p, cp in enumerate(copies):
        cp.start(priority=p % 2)


def _wait_all(copies):
    for cp in copies:
        cp.wait()


def _wait_and_cast(copies, stage, wbf):
    rows = stage.shape[0] // len(copies)
    for p, cp in enumerate(copies):
        cp.wait()
        wbf[p * rows:(p + 1) * rows, :] = stage[p * rows:(p + 1) * rows, :].astype(BF16)


def _expert_in_kernel(src_ref, crow_ref, cbase_ref, nch_ref, total_ref, xpk_hbm, w_hbm, act_hbm,
                      stage, wbf, ibuf, obuf, wsem, isem, osem, *, layer):
    e = pl.program_id(0)
    last = pl.num_programs(0) - 1
    total = total_ref[0]
    half = xpk_hbm.shape[1]

    def w_copies(ex):
        return _weight_copies(w_hbm, layer, ex, stage, wsem)

    def start_gather(g):
        row0 = crow_ref[g]

        def body(r8, carry):
            for u in range(ROW_UNROLL):
                r = r8 * ROW_UNROLL + u
                tok = src_ref[row0 + r]
                pltpu.make_async_copy(xpk_hbm.at[pl.ds(tok, 1)], ibuf.at[g % 2, pl.ds(r, 1)],
                                      isem.at[g % 2]).start(priority=u % 2)
            return carry
        lax.fori_loop(0, EXPERT_CHUNK // ROW_UNROLL, body, 0)

    def wait_gather(g):
        pltpu.make_async_copy(xpk_hbm.at[pl.ds(0, EXPERT_CHUNK)], ibuf.at[g % 2], isem.at[g % 2]).wait()

    def out_copy(g):
        rows = pl.ds(pl.multiple_of(crow_ref[g], BLOCK_ROWS), EXPERT_CHUNK)
        return pltpu.make_async_copy(obuf.at[g % 2], act_hbm.at[rows], osem.at[g % 2])

    @pl.when(e == 0)
    def _():
        _start_all(w_copies(e))

        @pl.when(total > 0)
        def _():
            start_gather(0)

    _wait_and_cast(w_copies(e), stage, wbf)

    @pl.when(e < last)
    def _():
        _start_all(w_copies(e + 1))

    base = cbase_ref[e]

    def body(i, carry):
        g = base + i

        @pl.when(g + 1 < total)
        def _():
            start_gather(g + 1)

        wait_gather(g)
        packed = ibuf[g % 2]
        x_lo = pltpu.bitcast(packed << 16, F32).astype(BF16)
        x_hi = pltpu.bitcast(packed & jnp.uint32(0xFFFF0000), F32).astype(BF16)
        h = (jnp.dot(x_lo, wbf[0:half, :], preferred_element_type=F32)
             + jnp.dot(x_hi, wbf[half:2 * half, :], preferred_element_type=F32))
        h1 = h[:, :D_FF]
        obuf[g % 2] = (h1 * jax.nn.sigmoid(h1) * h[:, D_FF:]).astype(obuf.dtype)

        @pl.when(g >= 1)
        def _():
            out_copy(g - 1).wait()

        out_copy(g).start()
        return carry

    lax.fori_loop(0, nch_ref[e], body, 0)

    @pl.when(e == last)
    def _():
        @pl.when(total > 0)
        def _():
            out_copy(total - 1).wait()

        end = jnp.where(total > 0, crow_ref[jnp.maximum(total - 1, 0)] + EXPERT_CHUNK, 0)
        obuf[0] = jnp.zeros(obuf.shape[1:], obuf.dtype)

        def tail_copy(blk):
            rows = pl.ds(pl.multiple_of(end + blk * BLOCK_ROWS, BLOCK_ROWS), BLOCK_ROWS)
            return pltpu.make_async_copy(obuf.at[0, 0:BLOCK_ROWS], act_hbm.at[rows], osem.at[0])

        n_tail = (act_hbm.shape[0] - end) // BLOCK_ROWS
        lax.fori_loop(0, n_tail, lambda blk, c: (tail_copy(blk).start(), c)[1], 0)
        lax.fori_loop(0, n_tail, lambda blk, c: (tail_copy(blk).wait(), c)[1], 0)


def expert_in(xpk, src, chunk_row, chunk_base, n_chunks, total, moe_w_in, layer):
    rows = src.shape[0]
    half = xpk.shape[1]
    wshape = moe_w_in.shape[-2:]
    resident = [_nbytes(wshape, F32), _nbytes(wshape, BF16),
                2 * _nbytes((EXPERT_CHUNK, half), U32), 2 * _nbytes((EXPERT_CHUNK, D_FF), BF16),
                _nbytes((EXPERT_CHUNK, 2 * half), BF16), 2 * _nbytes((EXPERT_CHUNK, wshape[1]), F32)]
    return pl.pallas_call(
        functools.partial(_expert_in_kernel, layer=layer),
        grid_spec=pltpu.PrefetchScalarGridSpec(
            num_scalar_prefetch=5,
            grid=(N_EXPERTS,),
            in_specs=[pl.BlockSpec(memory_space=pl.ANY), pl.BlockSpec(memory_space=pl.ANY)],
            out_specs=pl.BlockSpec(memory_space=pl.ANY),
            scratch_shapes=[pltpu.VMEM(wshape, F32), pltpu.VMEM(wshape, BF16),
                            pltpu.VMEM((2, EXPERT_CHUNK, half), U32),
                            pltpu.VMEM((2, EXPERT_CHUNK, D_FF), BF16),
                            pltpu.SemaphoreType.DMA((WEIGHT_DMA_PARTS,)),
                            pltpu.SemaphoreType.DMA((2,)),
                            pltpu.SemaphoreType.DMA((2,))]),
        out_shape=jax.ShapeDtypeStruct((rows, D_FF), BF16),
        compiler_params=pltpu.CompilerParams(
            dimension_semantics=("arbitrary",), has_side_effects=True,
            vmem_limit_bytes=int(min(VMEM_CAP, sum(resident) + 4 * 1024 * 1024))),
        name="expert_in",
    )(src, chunk_row, chunk_base, n_chunks, total, xpk, moe_w_in)


def _expert_kernel(pstart_ref, padded_ref, in_hbm, w_hbm, out_hbm,
                   stage, wbf, ibuf, obuf, wsem, isem, osem, *, layer):
    e = pl.program_id(0)

    def w_copies(ex):
        return _weight_copies(w_hbm, layer, ex, stage, wsem)

    @pl.when(e == 0)
    def _():
        _start_all(w_copies(e))

    _wait_and_cast(w_copies(e), stage, wbf)

    @pl.when(e + 1 < pl.num_programs(0))
    def _():
        _start_all(w_copies(e + 1))

    start = pstart_ref[e]
    n_chunks = (padded_ref[e] + EXPERT_CHUNK - 1) // EXPERT_CHUNK

    def chunk_rows(i):
        return pl.ds(pl.multiple_of(start + i * EXPERT_CHUNK, BLOCK_ROWS), EXPERT_CHUNK)

    def in_copy(i):
        return pltpu.make_async_copy(in_hbm.at[chunk_rows(i)], ibuf.at[i % 2], isem.at[i % 2])

    def out_copies(i):
        copies = []
        for p in range(EXPERT_CHUNK // BLOCK_ROWS):
            rows = pl.ds(pl.multiple_of(start + i * EXPERT_CHUNK + p * BLOCK_ROWS, BLOCK_ROWS),
                         BLOCK_ROWS)
            copies.append(pltpu.make_async_copy(
                obuf.at[i % 2, pl.ds(p * BLOCK_ROWS, BLOCK_ROWS)], out_hbm.at[rows], osem.at[i % 2]))
        return copies

    @pl.when(n_chunks > 0)
    def _():
        in_copy(0).start()

    def body(i, carry):
        @pl.when(i + 1 < n_chunks)
        def _():
            in_copy(i + 1).start()

        in_copy(i).wait()
        obuf[i % 2] = jnp.dot(ibuf[i % 2], wbf[...], preferred_element_type=F32).astype(obuf.dtype)
        _start_all(out_copies(i))

        @pl.when(i >= 1)
        def _():
            _wait_all(out_copies(i - 1))
        return carry

    lax.fori_loop(0, n_chunks, body, 0)

    @pl.when(n_chunks > 0)
    def _():
        _wait_all(out_copies(n_chunks - 1))

    @pl.when(e + 1 == pl.num_programs(0))
    def _():
        end = start + n_chunks * EXPERT_CHUNK
        obuf[0] = jnp.zeros(obuf.shape[1:], obuf.dtype)

        def tail_copy(blk):
            rows = pl.ds(pl.multiple_of(end + blk * BLOCK_ROWS, BLOCK_ROWS), BLOCK_ROWS)
            return pltpu.make_async_copy(obuf.at[0, 0:BLOCK_ROWS], out_hbm.at[rows], osem.at[0])

        n_tail = (out_hbm.shape[0] - end) // BLOCK_ROWS
        lax.fori_loop(0, n_tail, lambda blk, c: (tail_copy(blk).start(), c)[1], 0)
        lax.fori_loop(0, n_tail, lambda blk, c: (tail_copy(blk).wait(), c)[1], 0)


def _expert_call(rows_in, w, pad_start, padded, layer, *, out_dtype, name):
    rows, k = rows_in.shape
    n_out = w.shape[-1]
    wshape = w.shape[-2:]
    resident = [_nbytes(wshape, F32), _nbytes(wshape, BF16),
                2 * _nbytes((EXPERT_CHUNK, k), rows_in.dtype),
                2 * _nbytes((EXPERT_CHUNK, n_out), out_dtype),
                2 * _nbytes((EXPERT_CHUNK, wshape[1]), F32)]
    return pl.pallas_call(
        functools.partial(_expert_kernel, layer=layer),
        grid_spec=pltpu.PrefetchScalarGridSpec(
            num_scalar_prefetch=2,
            grid=(N_EXPERTS,),
            in_specs=[pl.BlockSpec(memory_space=pl.ANY), pl.BlockSpec(memory_space=pl.ANY)],
            out_specs=pl.BlockSpec(memory_space=pl.ANY),
            scratch_shapes=[pltpu.VMEM(wshape, F32), pltpu.VMEM(wshape, BF16),
                            pltpu.VMEM((2, EXPERT_CHUNK, k), rows_in.dtype),
                            pltpu.VMEM((2, EXPERT_CHUNK, n_out), out_dtype),
                            pltpu.SemaphoreType.DMA((WEIGHT_DMA_PARTS,)),
                            pltpu.SemaphoreType.DMA((2,)),
                            pltpu.SemaphoreType.DMA((2,))]),
        out_shape=jax.ShapeDtypeStruct((rows, n_out), out_dtype),
        compiler_params=pltpu.CompilerParams(
            dimension_semantics=("arbitrary",), has_side_effects=True,
            vmem_limit_bytes=int(min(VMEM_CAP, sum(resident) + 4 * 1024 * 1024))),
        name=name,
    )(pad_start, padded, rows_in, w)


def expert_out(act, pad_start, padded, moe_w_out, layer):
    return _expert_call(act, moe_w_out, pad_start, padded, layer, out_dtype=F32,
                        name="expert_out")


def _combine_ln_kernel(dest_ref, x_ref, gate_ref, g_ref, b_ref, y_hbm, *rest, tm, t, with_bf16):
    if with_bf16:
        xo_ref, xb_ref, ybuf, sem = rest
    else:
        xo_ref, ybuf, sem = rest
        xb_ref = None
    i = pl.program_id(0)
    n = pl.num_programs(0)

    def start_rows(step):
        buf = step % 2

        def body(r8, carry):
            for u in range(ROW_UNROLL):
                r = r8 * ROW_UNROLL + u
                for k in range(TOP_K):
                    d = dest_ref[k * t + step * tm + r]
                    pltpu.make_async_copy(y_hbm.at[pl.ds(d, 1)], ybuf.at[buf, k, pl.ds(r, 1)],
                                          sem.at[buf]).start(priority=(u + k) % 2)
            return carry
        lax.fori_loop(0, tm // ROW_UNROLL, body, 0)

    def wait_rows(step):
        buf = step % 2
        for k in range(TOP_K):
            pltpu.make_async_copy(y_hbm.at[pl.ds(0, tm)], ybuf.at[buf, k], sem.at[buf]).wait()

    @pl.when(i == 0)
    def _():
        start_rows(i)

    @pl.when(i + 1 < n)
    def _():
        start_rows(i + 1)

    wait_rows(i)
    buf = i % 2
    gate = gate_ref[...]
    h = gate[:, 0:1] * ybuf[buf, 0] + gate[:, 1:2] * ybuf[buf, 1]
    y = _layer_norm_rows(DEEPNORM_ALPHA * x_ref[...] + h, g_ref[...], b_ref[...])
    xo_ref[...] = y
    if with_bf16:
        xb_ref[...] = y.astype(BF16)


def combine_ln(x, y_disp, dest_flat, gate_t, g, b, *, tm, with_bf16):
    t, d = x.shape
    assert t % tm == 0
    row = pl.BlockSpec((tm, d), lambda i, dest: (i, 0))
    vec = pl.BlockSpec((1, d), lambda i, dest: (0, 0))
    out_specs = [row, row] if with_bf16 else [row]
    out_shape = [jax.ShapeDtypeStruct((t, d), F32)]
    if with_bf16:
        out_shape.append(jax.ShapeDtypeStruct((t, d), BF16))
    outs = pl.pallas_call(
        functools.partial(_combine_ln_kernel, tm=tm, t=t, with_bf16=with_bf16),
        grid_spec=pltpu.PrefetchScalarGridSpec(
            num_scalar_prefetch=1,
            grid=(t // tm,),
            in_specs=[row, pl.BlockSpec((tm, TOP_K), lambda i, dest: (i, 0)), vec, vec,
                      pl.BlockSpec(memory_space=pl.ANY)],
            out_specs=out_specs,
            scratch_shapes=[pltpu.VMEM((2, TOP_K, tm, d), F32), pltpu.SemaphoreType.DMA((2,))]),
        out_shape=out_shape,
        compiler_params=_params(
            ("arbitrary",),
            [2 * _nbytes((tm, d), F32), _nbytes((tm, d), BF16)],
            [_nbytes((2, TOP_K, tm, d), F32), 4 * _nbytes((tm, d), F32)]),
        name="moe_combine_ln",
    )(dest_flat, x, gate_t, g.reshape(1, d), b.reshape(1, d), y_disp)
    return (outs[0], outs[1]) if with_bf16 else (outs[0], None)


def moe_ffn_ln(x, xpk, e_idx, gate, rank, counts, moe_w_in, moe_w_out, layer, g, b, *,
               with_bf16, tm):
    t, d = x.shape
    tk = t * TOP_K
    padded = ((counts + BLOCK_ROWS - 1) // BLOCK_ROWS * BLOCK_ROWS).astype(I32)
    pad_end = jnp.cumsum(padded).astype(I32)
    pad_start = pad_end - padded
    n_blocks = (tk + N_EXPERTS * (BLOCK_ROWS - 1) + BLOCK_ROWS - 1) // BLOCK_ROWS + 1
    e_flat = e_idx.reshape(tk)
    dest_flat = pad_start[e_flat] + rank.reshape(tk)
    tok_flat = jnp.arange(tk, dtype=I32) % t
    src = jnp.zeros((n_blocks * BLOCK_ROWS,), I32).at[dest_flat].set(tok_flat)
    n_chunks = (padded + EXPERT_CHUNK - 1) // EXPERT_CHUNK
    chunk_end = jnp.cumsum(n_chunks).astype(I32)
    chunk_base = chunk_end - n_chunks
    max_chunks = n_blocks // (EXPERT_CHUNK // BLOCK_ROWS) + N_EXPERTS
    gidx = jnp.arange(max_chunks, dtype=I32)
    owner = jnp.minimum(jnp.sum(gidx[:, None] >= chunk_end[None, :], axis=1), N_EXPERTS - 1)
    chunk_row = (pad_start[owner] + (gidx - chunk_base[owner]) * EXPERT_CHUNK).astype(I32)
    act = expert_in(xpk, src, chunk_row, chunk_base, n_chunks.astype(I32), chunk_end[-1:],
                    moe_w_in, layer)
    y_disp = expert_out(act, pad_start, padded, moe_w_out, layer)
    return combine_ln(x, y_disp, dest_flat, gate.T, g, b, tm=tm, with_bf16=with_bf16)


def _softplus(x):
    return jnp.maximum(x, 0.0) + jnp.log1p(jnp.exp(-jnp.abs(x)))


def _silu(x):
    return x * jax.nn.sigmoid(x)


def _dot_01_f32(m01, v, *, ones_left):
    hi = v.astype(BF16)
    r = v - hi.astype(F32)
    mid = r.astype(BF16)
    lo = (r - mid.astype(F32)).astype(BF16)

    def dot(p):
        lhs, rhs = (m01, p) if ones_left else (p, m01)
        return jnp.dot(lhs, rhs, preferred_element_type=F32)
    return dot(hi) + dot(mid) + dot(lo)


def _ssd_kernel(xs_ref, b_ref, c_ref, z_ref, dt_ref, dtT_ref,
                cwx_ref, cwb_ref, cwc_ref, cbx_ref, cbb_ref, cbc_ref,
                dtb_ref, dtbT_ref, alog_ref, alogT_ref, dsk_ref, nw_ref,
                y_ref, state_ref, hx_ref, hb_ref, hc_ref):
    q = SSD_CHUNK

    @pl.when(pl.program_id(1) == 0)
    def _():
        state_ref[...] = jnp.zeros_like(state_ref)
        hx_ref[...] = jnp.zeros_like(hx_ref)
        hb_ref[...] = jnp.zeros_like(hb_ref)
        hc_ref[...] = jnp.zeros_like(hc_ref)

    def conv_silu(raw_ref, halo_ref, w_ref, bias_ref):
        raw = raw_ref[...]
        ext = jnp.concatenate([halo_ref[...], raw], axis=0)
        w = w_ref[...]
        acc = bias_ref[...] + w[SSD_CONV - 1:SSD_CONV, :] * raw
        for k in range(SSD_CONV - 1):
            lo = CONV_HALO - (SSD_CONV - 1) + k
            acc = acc + w[k:k + 1, :] * ext[lo:lo + q, :]
        halo_ref[...] = raw[q - CONV_HALO:q, :]
        return _silu(acc)

    xs = conv_silu(xs_ref, hx_ref, cwx_ref, cbx_ref)
    bm = conv_silu(b_ref, hb_ref, cwb_ref, cbb_ref)
    cm = conv_silu(c_ref, hc_ref, cwc_ref, cbc_ref)

    dt = _softplus(dt_ref[...] + dtb_ref[...])
    dt_t = _softplus(dtT_ref[...] + dtbT_ref[...])
    a = -jnp.exp(alog_ref[...])
    a_t = -jnp.exp(alogT_ref[...])
    row = lax.broadcasted_iota(I32, (q, q), 0)
    col = lax.broadcasted_iota(I32, (q, q), 1)
    causal = row >= col
    a_cum = _dot_01_f32(jnp.where(causal, 1.0, 0.0).astype(BF16), dt * a, ones_left=True)
    a_cum_t = _dot_01_f32(jnp.where(row <= col, 1.0, 0.0).astype(BF16), dt_t * a_t,
                          ones_left=False)
    a_last = a_cum[q - 1:q, :]
    hrow = lax.broadcasted_iota(I32, (SSD_HPG, SSD_GW), 0)
    hcol = lax.broadcasted_iota(I32, (SSD_HPG, SSD_GW), 1)
    expand = jnp.where(hcol // SSD_HEAD_DIM == hrow, 1.0, 0.0).astype(BF16)
    decay_in = _dot_01_f32(expand, jnp.exp(a_cum), ones_left=False)
    w_state = _dot_01_f32(expand, jnp.exp(a_last - a_cum) * dt, ones_left=False)
    chunk_decay = decay_in[q - 1:q, :]

    cmb = cm.astype(BF16)
    bmb = bm.astype(BF16)
    cb = lax.dot_general(cmb, bmb, (((1,), (1,)), ((), ())), preferred_element_type=F32)
    prev = state_ref[...]
    y_off = jnp.dot(cmb, prev.astype(BF16), preferred_element_type=F32) * decay_in
    xw = (w_state * xs).astype(BF16)
    state_ref[...] = chunk_decay * prev + jnp.dot(bm.T.astype(BF16), xw, preferred_element_type=F32)

    lane = lax.broadcasted_iota(I32, (q, LANES), 1)
    heads_per_tile = LANES // SSD_HEAD_DIM
    ys = []
    for tile in range(SSD_GW // LANES):
        ms = []
        for hh in range(heads_per_tile):
            h = tile * heads_per_tile + hh
            seg = a_cum[:, h:h + 1] - a_cum_t[h:h + 1, :]
            decay = jnp.where(causal, jnp.exp(seg), 0.0)
            ms.append((cb * decay * dt_t[h:h + 1, :]).astype(BF16))
        x_tile = xs[:, tile * LANES:(tile + 1) * LANES]
        rhs = jnp.concatenate(
            [jnp.where(lane // SSD_HEAD_DIM == hh, x_tile, 0.0).astype(BF16)
             for hh in range(heads_per_tile)], axis=0)
        ys.append(jnp.dot(jnp.concatenate(ms, axis=1), rhs, preferred_element_type=F32))
    y = jnp.concatenate(ys, axis=1) + y_off + dsk_ref[...] * xs
    y = y * _silu(z_ref[...])
    y = y * lax.rsqrt(jnp.mean(y * y, axis=-1, keepdims=True) + RMS_EPS) * nw_ref[...]
    y_ref[...] = y.astype(y_ref.dtype)


def ssd_core(z, xbc, dt_raw, conv_w, conv_b, dt_bias, a_log, d_skip, norm_w):
    t = z.shape[0]
    q, gw, ns, hpg, ng = SSD_CHUNK, SSD_GW, SSD_STATE, SSD_HPG, SSD_GROUPS
    assert t % q == 0
    b_blk = D_INNER // ns
    c_blk = b_blk + ng
    dt_g = dt_raw.reshape(t, ng, hpg).transpose(1, 0, 2)
    dt_gt = dt_raw.T.reshape(ng, hpg, t)
    dtb = dt_bias.astype(F32).reshape(ng, 1, hpg)
    dtb_t = dt_bias.astype(F32).reshape(ng, hpg, 1)
    alog = a_log.astype(F32).reshape(ng, 1, hpg)
    alog_t = a_log.astype(F32).reshape(ng, hpg, 1)
    dsk = jnp.repeat(d_skip.astype(F32), SSD_HEAD_DIM).reshape(1, D_INNER)
    cw = conv_w.astype(F32)
    cbias = conv_b.astype(F32).reshape(1, SSD_CONV_DIM)
    nw = norm_w.astype(F32).reshape(1, D_INNER)

    def cspec(rows, width, blk):
        return pl.BlockSpec((rows, width), lambda g, c: (0, blk(g)))

    in_specs = [
        pl.BlockSpec((q, gw), lambda g, c: (c, g)),
        pl.BlockSpec((q, ns), lambda g, c: (c, b_blk + g)),
        pl.BlockSpec((q, ns), lambda g, c: (c, c_blk + g)),
        pl.BlockSpec((q, gw), lambda g, c: (c, g)),
        pl.BlockSpec((None, q, hpg), lambda g, c: (g, c, 0)),
        pl.BlockSpec((None, hpg, q), lambda g, c: (g, 0, c)),
        cspec(SSD_CONV, gw, lambda g: g), cspec(SSD_CONV, ns, lambda g: b_blk + g),
        cspec(SSD_CONV, ns, lambda g: c_blk + g),
        cspec(1, gw, lambda g: g), cspec(1, ns, lambda g: b_blk + g), cspec(1, ns, lambda g: c_blk + g),
        pl.BlockSpec((None, 1, hpg), lambda g, c: (g, 0, 0)),
        pl.BlockSpec((None, hpg, 1), lambda g, c: (g, 0, 0)),
        pl.BlockSpec((None, 1, hpg), lambda g, c: (g, 0, 0)),
        pl.BlockSpec((None, hpg, 1), lambda g, c: (g, 0, 0)),
        cspec(1, gw, lambda g: g), cspec(1, gw, lambda g: g),
    ]
    return pl.pallas_call(
        _ssd_kernel,
        grid=(ng, t // q),
        in_specs=in_specs,
        out_specs=pl.BlockSpec((q, gw), lambda g, c: (c, g)),
        out_shape=jax.ShapeDtypeStruct((t, D_INNER), BF16),
        scratch_shapes=[pltpu.VMEM((ns, gw), F32), pltpu.VMEM((CONV_HALO, gw), F32),
                        pltpu.VMEM((CONV_HALO, ns), F32), pltpu.VMEM((CONV_HALO, ns), F32)],
        compiler_params=_params(
            ("arbitrary", "arbitrary"),
            [2 * _nbytes((q, gw), F32), 2 * _nbytes((q, ns), F32), _nbytes((q, gw), BF16)],
            [_nbytes((ns, gw), F32), 16 * _nbytes((q, gw), F32)]),
        name="ssd_core",
    )(xbc, xbc, xbc, z, dt_g, dt_gt, cw, cw, cw, cbias, cbias, cbias,
      dtb, dtb_t, alog, alog_t, dsk, nw)


MM_TM = 1024
MM_TN = 512
WIDE_K_TM = 512
LN_TM = 256
COMBINE_TM = 128


def pool_mixer(xb, w_in, w_group, scale, w_out):
    p = pool_in(xb, w_in, tm=MM_TM, tn=MM_TN)
    mixed = pool_group_matmul(p, w_group, scale, tm=MM_TM)
    return matmul_cols(mixed, w_out, 0, D_MODEL, tm=MM_TM, tn=MM_TN, out_dtype=F32, name="pool_out")


def ssd_mixer(xb, w_in, conv_w, conv_b, dt_bias, a_log, d_skip, norm_w, w_out):
    z = matmul_cols(xb, w_in, 0, D_INNER, tm=MM_TM, tn=MM_TN, out_dtype=F32, name="ssd_in_z")
    xbc = matmul_cols(xb, w_in, D_INNER, SSD_CONV_DIM, tm=MM_TM, tn=MM_TN, out_dtype=F32,
                      name="ssd_in_xbc")
    dt_raw = matmul_cols(xb, w_in, D_INNER + SSD_CONV_DIM, SSD_HEADS, tm=MM_TM, tn=SSD_HEADS,
                         out_dtype=F32, name="ssd_in_dt")
    y = ssd_core(z, xbc, dt_raw, conv_w, conv_b, dt_bias, a_log, d_skip, norm_w)
    return matmul_bf16w(y, w_out.astype(BF16), tm=WIDE_K_TM, tn=MM_TN, out_dtype=F32, name="ssd_out")


def kernel(x, pool_w_in, pool_w_group, pool_scale, pool_w_out, ssd_w_in, ssd_conv_w, ssd_conv_b,
           ssd_dt_bias, ssd_a_log, ssd_d, ssd_norm_w, ssd_w_out, moe_w_router, moe_router_bias,
           moe_w_in, moe_w_out, ln_mix_g, ln_mix_b, ln_ffn_g, ln_ffn_b):
    bsz, seq, d = x.shape
    x = x.reshape(bsz * seq, d)
    assert bsz == 1
    xb = x.astype(BF16)
    for i in range(DEPTH):
        j = i // N_MIXERS
        if i % N_MIXERS == 0:
            h = pool_mixer(xb, pool_w_in[j], pool_w_group[j], pool_scale[j], pool_w_out[j])
        else:
            h = ssd_mixer(xb, ssd_w_in[j], ssd_conv_w[j], ssd_conv_b[j], ssd_dt_bias[j],
                          ssd_a_log[j], ssd_d[j], ssd_norm_w[j], ssd_w_out[j])
        x, xpk, e_idx, gate, rank, counts = ln_router(
            x, h, ln_mix_g[i], ln_mix_b[i], moe_w_router, moe_router_bias, tm=LN_TM)
        x, xb = moe_ffn_ln(x, xpk, e_idx, gate, rank, counts, moe_w_in, moe_w_out, i,
                           ln_ffn_g[i], ln_ffn_b[i], with_bf16=(i + 1 < DEPTH), tm=COMBINE_TM)
    return x.reshape(bsz, seq, d)
```

```python
import functools

import jax
import jax.numpy as jnp
from jax import lax
from jax.experimental import pallas as pl
from jax.experimental.pallas import tpu as pltpu

F32 = jnp.float32
BF16 = jnp.bfloat16
I32 = jnp.int32
U32 = jnp.uint32

D_MODEL = 4096
DEPTH = 2
N_MIXERS = 2
DEEPNORM_ALPHA = (2 * DEPTH) ** 0.25
LN_EPS = 1e-5
POOL_WINDOWS = (2, 4, 8, 16)
POOL_GROUPS = len(POOL_WINDOWS)
POOL_GC = D_MODEL // POOL_GROUPS
D_INNER = 2 * D_MODEL
SSD_HEAD_DIM = 64
SSD_HEADS = D_INNER // SSD_HEAD_DIM
SSD_GROUPS = 8
SSD_HPG = SSD_HEADS // SSD_GROUPS
SSD_STATE = 128
SSD_CONV = 4
SSD_CHUNK = 128
SSD_GW = D_INNER // SSD_GROUPS
SSD_CONV_DIM = D_INNER + 2 * SSD_GROUPS * SSD_STATE
SSD_IN_DIM = D_INNER + SSD_CONV_DIM + SSD_HEADS
RMS_EPS = 1e-5
N_EXPERTS = 32
N_EXPERT_GROUPS = 8
EXPERTS_PER_GROUP = N_EXPERTS // N_EXPERT_GROUPS
TOP_K = 2
D_FF = 768
BLOCK_ROWS = 128

LANES = 128
SUBLANES = 8
VMEM_BYTES_V7X = 64 * 1024 * 1024
VMEM_CAP = VMEM_BYTES_V7X * 7 // 8

POOL_HALO = 16
CONV_HALO = SUBLANES
EXPERT_CHUNK = 2 * BLOCK_ROWS
ROW_UNROLL = 8
WEIGHT_DMA_PARTS = 4
ROW_DMA_PRIORITY = 0
WEIGHT_DMA_PRIORITY = 1


def _nbytes(shape, dtype):
    n = 1
    for s in shape:
        n *= s
    return n * jnp.dtype(dtype).itemsize


def _vmem_limit(pipelined, resident):
    est = 2 * sum(pipelined) + sum(resident)
    return int(min(VMEM_CAP, est + max(est // 4, 8 * 1024 * 1024)))


def _params(semantics, pipelined, resident=()):
    return pltpu.CompilerParams(dimension_semantics=semantics,
                                vmem_limit_bytes=_vmem_limit(pipelined, resident))


def _mm_kernel(a_ref, w_ref, o_ref, wbf_ref):
    @pl.when(pl.program_id(1) == 0)
    def _():
        wbf_ref[...] = w_ref[...].astype(BF16)

    o_ref[...] = jnp.dot(a_ref[...], wbf_ref[...], preferred_element_type=F32).astype(o_ref.dtype)


def _mm_scale_kernel(a_ref, w_ref, s_ref, o_ref, wbf_ref):
    @pl.when(pl.program_id(1) == 0)
    def _():
        wbf_ref[...] = w_ref[...].astype(BF16)

    acc = jnp.dot(a_ref[...], wbf_ref[...], preferred_element_type=F32)
    o_ref[...] = (acc * s_ref[...]).astype(o_ref.dtype)


def matmul_cols(a, w, col_off, n_cols, *, tm, tn, out_dtype, name):
    m_rows, k = a.shape
    assert w.shape[0] == k and m_rows % tm == 0 and n_cols % tn == 0 and col_off % tn == 0
    off = col_off // tn
    return pl.pallas_call(
        _mm_kernel,
        grid=(n_cols // tn, m_rows // tm),
        in_specs=[pl.BlockSpec((tm, k), lambda n, m: (m, 0)),
                  pl.BlockSpec((k, tn), lambda n, m: (0, n + off))],
        out_specs=pl.BlockSpec((tm, tn), lambda n, m: (m, n)),
        out_shape=jax.ShapeDtypeStruct((m_rows, n_cols), out_dtype),
        scratch_shapes=[pltpu.VMEM((k, tn), BF16)],
        compiler_params=_params(
            ("arbitrary", "arbitrary"),
            [_nbytes((tm, k), BF16), _nbytes((k, tn), F32), _nbytes((tm, tn), out_dtype)],
            [_nbytes((k, tn), BF16), _nbytes((tm, tn), F32)]),
        name=name,
    )(a, w)


def _mm_silu_kernel(a_ref, w_ref, o_ref, wbf_ref, *, parts):
    @pl.when(pl.program_id(1) == 0)
    def _():
        wbf_ref[...] = w_ref[...].astype(BF16)

    rows = a_ref.shape[0] // parts
    for p in range(parts):
        part = slice(p * rows, (p + 1) * rows)
        o_ref[part, :] = _silu(jnp.dot(a_ref[part, :], wbf_ref[...], preferred_element_type=F32))


def _mm_conv_silu_kernel(a_ref, w_ref, cw_ref, cb_ref, o_ref, wbf_ref, raw_ref, *, parts):
    tm, tn = o_ref.shape

    @pl.when(pl.program_id(1) == 0)
    def _():
        wbf_ref[...] = w_ref[...].astype(BF16)
        raw_ref[0:CONV_HALO, :] = jnp.zeros((CONV_HALO, tn), F32)

    rows = tm // parts
    for p in range(parts):
        lo = CONV_HALO + p * rows
        raw_ref[lo:lo + rows, :] = jnp.dot(a_ref[p * rows:(p + 1) * rows, :], wbf_ref[...],
                                           preferred_element_type=F32)
        for s in range(tn // LANES):
            cols = slice(s * LANES, (s + 1) * LANES)
            acc = cb_ref[:, cols] + cw_ref[SSD_CONV - 1:SSD_CONV, cols] * raw_ref[lo:lo + rows, cols]
            for k in range(SSD_CONV - 1):
                back = SSD_CONV - 1 - k
                acc = acc + cw_ref[k:k + 1, cols] * raw_ref[lo - back:lo - back + rows, cols]
            o_ref[p * rows:(p + 1) * rows, cols] = _silu(acc)
    raw_ref[0:CONV_HALO, :] = raw_ref[tm:tm + CONV_HALO, :]


def matmul_cols_act(a, w, col_off, n_cols, *, tm, tn, parts, conv=None, name):
    m_rows, k = a.shape
    assert w.shape[0] == k and m_rows % tm == 0 and n_cols % tn == 0 and col_off % tn == 0
    assert tm % parts == 0 and (tm // parts) % CONV_HALO == 0
    off = col_off // tn
    in_specs = [pl.BlockSpec((tm, k), lambda n, m: (m, 0)),
                pl.BlockSpec((k, tn), lambda n, m: (0, n + off))]
    scratch = [pltpu.VMEM((k, tn), BF16)]
    operands = [a, w]
    if conv is None:
        body = functools.partial(_mm_silu_kernel, parts=parts)
    else:
        body = functools.partial(_mm_conv_silu_kernel, parts=parts)
        in_specs += [pl.BlockSpec((SSD_CONV, tn), lambda n, m: (0, n)),
                     pl.BlockSpec((1, tn), lambda n, m: (0, n))]
        scratch.append(pltpu.VMEM((CONV_HALO + tm, tn), F32))
        operands += list(conv)
    return pl.pallas_call(
        body,
        grid=(n_cols // tn, m_rows // tm),
        in_specs=in_specs,
        out_specs=pl.BlockSpec((tm, tn), lambda n, m: (m, n)),
        out_shape=jax.ShapeDtypeStruct((m_rows, n_cols), F32),
        scratch_shapes=scratch,
        compiler_params=_params(
            ("arbitrary", "arbitrary"),
            [_nbytes((tm, k), BF16), _nbytes((k, tn), F32), _nbytes((tm, tn), F32)],
            [_nbytes((k, tn), BF16), 2 * _nbytes((tm, tn), F32)]),
        name=name,
    )(*operands)


def _mm_bf16w_kernel(a_ref, w_ref, o_ref):
    o_ref[...] = jnp.dot(a_ref[...], w_ref[...], preferred_element_type=F32).astype(o_ref.dtype)


def matmul_bf16w(a, w, *, tm, tn, out_dtype, name):
    m_rows, k = a.shape
    n_cols = w.shape[1]
    assert w.shape[0] == k and m_rows % tm == 0 and n_cols % tn == 0
    return pl.pallas_call(
        _mm_bf16w_kernel,
        grid=(n_cols // tn, m_rows // tm),
        in_specs=[pl.BlockSpec((tm, k), lambda n, m: (m, 0)),
                  pl.BlockSpec((k, tn), lambda n, m: (0, n))],
        out_specs=pl.BlockSpec((tm, tn), lambda n, m: (m, n)),
        out_shape=jax.ShapeDtypeStruct((m_rows, n_cols), out_dtype),
        compiler_params=_params(
            ("arbitrary", "arbitrary"),
            [_nbytes((tm, k), BF16), _nbytes((k, tn), BF16), _nbytes((tm, tn), out_dtype)],
            [_nbytes((tm, tn), F32)]),
        name=name,
    )(a, w)


def pool_group_matmul(p, w_group, scale, *, tm):
    m_rows = p.shape[0]
    gc = POOL_GC
    assert m_rows % tm == 0
    return pl.pallas_call(
        _mm_scale_kernel,
        grid=(POOL_GROUPS, m_rows // tm),
        in_specs=[pl.BlockSpec((tm, gc), lambda g, m: (m, g)),
                  pl.BlockSpec((None, gc, gc), lambda g, m: (g, 0, 0)),
                  pl.BlockSpec((1, gc), lambda g, m: (0, g))],
        out_specs=pl.BlockSpec((tm, gc), lambda g, m: (m, g)),
        out_shape=jax.ShapeDtypeStruct((m_rows, POOL_GROUPS * gc), BF16),
        scratch_shapes=[pltpu.VMEM((gc, gc), BF16)],
        compiler_params=_params(
            ("arbitrary", "arbitrary"),
            [_nbytes((tm, gc), BF16), _nbytes((gc, gc), F32), _nbytes((tm, gc), BF16)],
            [_nbytes((gc, gc), BF16), _nbytes((tm, gc), F32)]),
        name="pool_group_matmul",
    )(p, w_group, scale.reshape(1, -1))


def _pool_in_kernel(x_ref, w_ref, p_ref, wbf_ref, ubuf_ref, *, tm, tn):
    n = pl.program_id(0)
    m = pl.program_id(1)

    @pl.when(m == 0)
    def _():
        wbf_ref[...] = w_ref[...].astype(BF16)
        ubuf_ref[0:POOL_HALO, :] = jnp.zeros((POOL_HALO, tn), F32)

    u = jnp.dot(x_ref[...], wbf_ref[...], preferred_element_type=F32)
    ubuf_ref[POOL_HALO:POOL_HALO + tm, :] = u
    pos = (m * tm + 1 + lax.broadcasted_iota(I32, (tm, tn), 0)).astype(F32)
    group = (n * tn) // POOL_GC
    for gi, window in enumerate(POOL_WINDOWS):
        @pl.when(group == gi)
        def _(window=window):
            acc = u
            for k in range(1, window):
                acc = acc + ubuf_ref[POOL_HALO - k:POOL_HALO - k + tm, :]
            mean = acc / jnp.minimum(pos, float(window))
            p_ref[...] = (mean - u).astype(p_ref.dtype)

    ubuf_ref[0:POOL_HALO, :] = ubuf_ref[tm:tm + POOL_HALO, :]


def pool_in(xb, w_in, *, tm, tn):
    m_rows, k = xb.shape
    n_cols = w_in.shape[1]
    assert m_rows % tm == 0 and n_cols % tn == 0 and POOL_GC % tn == 0 and tm >= POOL_HALO
    return pl.pallas_call(
        functools.partial(_pool_in_kernel, tm=tm, tn=tn),
        grid=(n_cols // tn, m_rows // tm),
        in_specs=[pl.BlockSpec((tm, k), lambda n, m: (m, 0)),
                  pl.BlockSpec((k, tn), lambda n, m: (0, n))],
        out_specs=pl.BlockSpec((tm, tn), lambda n, m: (m, n)),
        out_shape=jax.ShapeDtypeStruct((m_rows, n_cols), BF16),
        scratch_shapes=[pltpu.VMEM((k, tn), BF16), pltpu.VMEM((tm + POOL_HALO, tn), F32)],
        compiler_params=_params(
            ("arbitrary", "arbitrary"),
            [_nbytes((tm, k), BF16), _nbytes((k, tn), F32), _nbytes((tm, tn), BF16)],
            [_nbytes((k, tn), BF16), 4 * _nbytes((tm + POOL_HALO, tn), F32)]),
        name="pool_in",
    )(xb, w_in)


def _layer_norm_rows(v, g, b):
    mu = jnp.mean(v, axis=-1, keepdims=True)
    vc = v - mu
    var = jnp.mean(vc * vc, axis=-1, keepdims=True)
    return vc * lax.rsqrt(var + LN_EPS) * g + b


def _ln_router_kernel(x_ref, h_ref, g_ref, b_ref, wr_ref, rb_ref,
                      xo_ref, xpk_ref, e_ref, gate_ref, rank_ref, cnt_ref, base_ref, *, tm):
    i = pl.program_id(0)

    @pl.when(i == 0)
    def _():
        base_ref[...] = jnp.zeros_like(base_ref)

    y = _layer_norm_rows(DEEPNORM_ALPHA * x_ref[...] + h_ref[...], g_ref[...], b_ref[...])
    xo_ref[...] = y
    yb = y.astype(BF16)
    half = y.shape[1] // 2
    lo_bits = pltpu.bitcast(yb[:, :half].astype(F32), U32) >> 16
    hi_bits = pltpu.bitcast(yb[:, half:].astype(F32), U32)
    xpk_ref[...] = hi_bits | lo_bits

    ng, epg = N_EXPERT_GROUPS, EXPERTS_PER_GROUP
    logits = lax.dot_general(wr_ref[...], yb, (((1,), (1,)), ((), ())),
                             preferred_element_type=F32)
    scores = jax.nn.sigmoid(logits)
    sel = scores + rb_ref[...]
    sel_j = [sel[ng * j:ng * (j + 1), :] for j in range(epg)]
    sc_j = [scores[ng * j:ng * (j + 1), :] for j in range(epg)]
    gscore = None
    for j1 in range(epg):
        for j2 in range(j1 + 1, epg):
            s = sel_j[j1] + sel_j[j2]
            gscore = s if gscore is None else jnp.maximum(gscore, s)
    gmax = jnp.max(gscore, axis=0, keepdims=True)
    giota = lax.broadcasted_iota(I32, (ng, tm), 0)
    g_idx = jnp.min(jnp.where(gscore == gmax, giota, ng), axis=0, keepdims=True)
    in_g = giota == g_idx
    v = [jnp.sum(jnp.where(in_g, sel_j[j], 0.0), axis=0, keepdims=True) for j in range(epg)]
    s = [jnp.sum(jnp.where(in_g, sc_j[j], 0.0), axis=0, keepdims=True) for j in range(epg)]
    order = []
    for j in range(epg):
        r = jnp.zeros((1, tm), I32)
        for k in range(epg):
            if k == j:
                continue
            beats = (v[k] >= v[j]) if k < j else (v[k] > v[j])
            r = r + beats.astype(I32)
        order.append(r)
    eiota = lax.broadcasted_iota(I32, (N_EXPERTS, tm), 0)
    onehot = jnp.zeros((N_EXPERTS, tm), F32)
    loc, raw, rows = [], [], []
    for slot in range(TOP_K):
        lj = jnp.zeros((1, tm), I32)
        gs = jnp.zeros((1, tm), F32)
        for j in range(epg):
            hit = order[j] == slot
            lj = lj + jnp.where(hit, j, 0)
            gs = gs + jnp.where(hit, s[j], 0.0)
        row = lj * ng + g_idx
        onehot = onehot + (eiota == row).astype(F32)
        loc.append(lj)
        raw.append(gs)
        rows.append(row)
    denom = raw[0] + raw[1]
    srow = lax.broadcasted_iota(I32, (tm, tm), 0)
    scol = lax.broadcasted_iota(I32, (tm, tm), 1)
    before = (srow < scol).astype(BF16)
    prefix = jnp.dot(onehot.astype(BF16), before, preferred_element_type=F32) + base_ref[...]
    for slot in range(TOP_K):
        e_ref[slot:slot + 1, :] = g_idx * epg + loc[slot]
        gate_ref[slot:slot + 1, :] = raw[slot] / denom
        rk = jnp.sum(jnp.where(eiota == rows[slot], prefix, 0.0), axis=0, keepdims=True)
        rank_ref[slot:slot + 1, :] = rk.astype(I32)
    base_ref[...] = base_ref[...] + jnp.sum(onehot, axis=1, keepdims=True)
    cnt_ref[...] = jnp.broadcast_to(base_ref[...], cnt_ref.shape)


def ln_router(x, h, g, b, w_router, router_bias, *, tm):
    t, d = x.shape
    assert t % tm == 0
    ng, epg = N_EXPERT_GROUPS, EXPERTS_PER_GROUP
    wr = w_router.T.reshape(ng, epg, d).transpose(1, 0, 2).reshape(N_EXPERTS, d).astype(BF16)
    rb = router_bias.astype(F32).reshape(ng, epg).T.reshape(N_EXPERTS, 1)
    row = pl.BlockSpec((tm, d), lambda i: (i, 0))
    vec = pl.BlockSpec((1, d), lambda i: (0, 0))
    tok = pl.BlockSpec((TOP_K, tm), lambda i: (0, i))
    outs = pl.pallas_call(
        functools.partial(_ln_router_kernel, tm=tm),
        grid=(t // tm,),
        in_specs=[row, row, vec, vec,
                  pl.BlockSpec((N_EXPERTS, d), lambda i: (0, 0)),
                  pl.BlockSpec((N_EXPERTS, 1), lambda i: (0, 0))],
        out_specs=[row, pl.BlockSpec((tm, d // 2), lambda i: (i, 0)), tok, tok, tok,
                   pl.BlockSpec((N_EXPERTS, LANES), lambda i: (0, 0))],
        out_shape=[jax.ShapeDtypeStruct((t, d), F32), jax.ShapeDtypeStruct((t, d // 2), U32),
                   jax.ShapeDtypeStruct((TOP_K, t), I32), jax.ShapeDtypeStruct((TOP_K, t), F32),
                   jax.ShapeDtypeStruct((TOP_K, t), I32),
                   jax.ShapeDtypeStruct((N_EXPERTS, LANES), F32)],
        scratch_shapes=[pltpu.VMEM((N_EXPERTS, 1), F32)],
        compiler_params=_params(
            ("arbitrary",),
            [3 * _nbytes((tm, d), F32), _nbytes((tm, d), BF16), _nbytes((N_EXPERTS, d), BF16)],
            [4 * _nbytes((tm, d), F32)]),
        name="ln_router",
    )(x, h, g.reshape(1, d), b.reshape(1, d), wr, rb)
    x_new, xpk, e_idx, gate, rank, cnt = outs
    counts = cnt[:, 0].astype(I32).reshape(epg, ng).T.reshape(N_EXPERTS)
    return x_new, xpk, e_idx, gate, rank, counts


def _weight_copies(w_hbm, layer, ex, stage, wsem):
    rows = stage.shape[0] // WEIGHT_DMA_PARTS
    assert rows * WEIGHT_DMA_PARTS == stage.shape[0] and rows % SUBLANES == 0
    return [pltpu.make_async_copy(w_hbm.at[layer, ex, pl.ds(p * rows, rows)],
                                  stage.at[pl.ds(p * rows, rows)], wsem.at[p])
            for p in range(WEIGHT_DMA_PARTS)]


def _start_all(copies, priority=None):
    for p, cp in enumerate(copies):
        cp.start(priority=p % 2 if priority is None else priority)


def _wait_all(copies):
    for cp in copies:
        cp.wait()


def _wait_and_cast(copies, stage, wbf):
    rows = stage.shape[0] // len(copies)
    for p, cp in enumerate(copies):
        cp.wait()
        wbf[p * rows:(p + 1) * rows, :] = stage[p * rows:(p + 1) * rows, :].astype(BF16)


def _expert_in_kernel(src_ref, crow_ref, cbase_ref, nch_ref, total_ref, xpk_hbm, w_hbm, act_hbm,
                      stage, wbf, ibuf, obuf, wsem, isem, osem, *, layer):
    e = pl.program_id(0)
    last = pl.num_programs(0) - 1
    total = total_ref[0]
    half = xpk_hbm.shape[1]

    def w_copies(ex):
        return _weight_copies(w_hbm, layer, ex, stage, wsem)

    def start_gather(g):
        row0 = crow_ref[g]

        def body(r8, carry):
            for u in range(ROW_UNROLL):
                r = r8 * ROW_UNROLL + u
                tok = src_ref[row0 + r]
                pltpu.make_async_copy(xpk_hbm.at[pl.ds(tok, 1)], ibuf.at[g % 2, pl.ds(r, 1)],
                                      isem.at[g % 2]).start(priority=ROW_DMA_PRIORITY)
            return carry
        lax.fori_loop(0, EXPERT_CHUNK // ROW_UNROLL, body, 0)

    def wait_gather(g):
        pltpu.make_async_copy(xpk_hbm.at[pl.ds(0, EXPERT_CHUNK)], ibuf.at[g % 2], isem.at[g % 2]).wait()

    def out_copy(g):
        rows = pl.ds(pl.multiple_of(crow_ref[g], BLOCK_ROWS), EXPERT_CHUNK)
        return pltpu.make_async_copy(obuf.at[g % 2], act_hbm.at[rows], osem.at[g % 2])

    @pl.when(e == 0)
    def _():
        _start_all(w_copies(e), priority=WEIGHT_DMA_PRIORITY)

        @pl.when(total > 0)
        def _():
            start_gather(0)

    _wait_and_cast(w_copies(e), stage, wbf)

    @pl.when(e < last)
    def _():
        _start_all(w_copies(e + 1), priority=WEIGHT_DMA_PRIORITY)

    base = cbase_ref[e]

    def body(i, carry):
        g = base + i

        @pl.when(g + 1 < total)
        def _():
            start_gather(g + 1)

        wait_gather(g)
        packed = ibuf[g % 2]
        x_lo = pltpu.bitcast(packed << 16, F32).astype(BF16)
        x_hi = pltpu.bitcast(packed & jnp.uint32(0xFFFF0000), F32).astype(BF16)
        h = (jnp.dot(x_lo, wbf[0:half, :], preferred_element_type=F32)
             + jnp.dot(x_hi, wbf[half:2 * half, :], preferred_element_type=F32))
        h1 = h[:, :D_FF]
        obuf[g % 2] = (h1 * jax.nn.sigmoid(h1) * h[:, D_FF:]).astype(obuf.dtype)

        @pl.when(g >= 1)
        def _():
            out_copy(g - 1).wait()

        out_copy(g).start()
        return carry

    lax.fori_loop(0, nch_ref[e], body, 0)

    @pl.when(e == last)
    def _():
        @pl.when(total > 0)
        def _():
            out_copy(total - 1).wait()

        end = jnp.where(total > 0, crow_ref[jnp.maximum(total - 1, 0)] + EXPERT_CHUNK, 0)
        obuf[0] = jnp.zeros(obuf.shape[1:], obuf.dtype)

        def tail_copy(blk):
            rows = pl.ds(pl.multiple_of(end + blk * BLOCK_ROWS, BLOCK_ROWS), BLOCK_ROWS)
            return pltpu.make_async_copy(obuf.at[0, 0:BLOCK_ROWS], act_hbm.at[rows], osem.at[0])

        n_tail = (act_hbm.shape[0] - end) // BLOCK_ROWS
        lax.fori_loop(0, n_tail, lambda blk, c: (tail_copy(blk).start(), c)[1], 0)
        lax.fori_loop(0, n_tail, lambda blk, c: (tail_copy(blk).wait(), c)[1], 0)


def expert_in(xpk, src, chunk_row, chunk_base, n_chunks, total, moe_w_in, layer):
    rows = src.shape[0]
    half = xpk.shape[1]
    wshape = moe_w_in.shape[-2:]
    resident = [_nbytes(wshape, F32), _nbytes(wshape, BF16),
                2 * _nbytes((EXPERT_CHUNK, half), U32), 2 * _nbytes((EXPERT_CHUNK, D_FF), BF16),
                _nbytes((EXPERT_CHUNK, 2 * half), BF16), 2 * _nbytes((EXPERT_CHUNK, wshape[1]), F32)]
    return pl.pallas_call(
        functools.partial(_expert_in_kernel, layer=layer),
        grid_spec=pltpu.PrefetchScalarGridSpec(
            num_scalar_prefetch=5,
            grid=(N_EXPERTS,),
            in_specs=[pl.BlockSpec(memory_space=pl.ANY), pl.BlockSpec(memory_space=pl.ANY)],
            out_specs=pl.BlockSpec(memory_space=pl.ANY),
            scratch_shapes=[pltpu.VMEM(wshape, F32), pltpu.VMEM(wshape, BF16),
                            pltpu.VMEM((2, EXPERT_CHUNK, half), U32),
                            pltpu.VMEM((2, EXPERT_CHUNK, D_FF), BF16),
                            pltpu.SemaphoreType.DMA((WEIGHT_DMA_PARTS,)),
                            pltpu.SemaphoreType.DMA((2,)),
                            pltpu.SemaphoreType.DMA((2,))]),
        out_shape=jax.ShapeDtypeStruct((rows, D_FF), BF16),
        compiler_params=pltpu.CompilerParams(
            dimension_semantics=("arbitrary",), has_side_effects=True,
            vmem_limit_bytes=int(min(VMEM_CAP, sum(resident) + 4 * 1024 * 1024))),
        name="expert_in",
    )(src, chunk_row, chunk_base, n_chunks, total, xpk, moe_w_in)


def _expert_kernel(pstart_ref, padded_ref, in_hbm, w_hbm, out_hbm,
                   stage, wbf, ibuf, obuf, wsem, isem, osem, *, layer):
    e = pl.program_id(0)

    def w_copies(ex):
        return _weight_copies(w_hbm, layer, ex, stage, wsem)

    @pl.when(e == 0)
    def _():
        _start_all(w_copies(e), priority=WEIGHT_DMA_PRIORITY)

    _wait_and_cast(w_copies(e), stage, wbf)

    @pl.when(e + 1 < pl.num_programs(0))
    def _():
        _start_all(w_copies(e + 1), priority=WEIGHT_DMA_PRIORITY)

    start = pstart_ref[e]
    n_chunks = (padded_ref[e] + EXPERT_CHUNK - 1) // EXPERT_CHUNK

    def chunk_rows(i):
        return pl.ds(pl.multiple_of(start + i * EXPERT_CHUNK, BLOCK_ROWS), EXPERT_CHUNK)

    def in_copy(i):
        return pltpu.make_async_copy(in_hbm.at[chunk_rows(i)], ibuf.at[i % 2], isem.at[i % 2])

    def out_copies(i):
        copies = []
        for p in range(EXPERT_CHUNK // BLOCK_ROWS):
            rows = pl.ds(pl.multiple_of(start + i * EXPERT_CHUNK + p * BLOCK_ROWS, BLOCK_ROWS),
                         BLOCK_ROWS)
            copies.append(pltpu.make_async_copy(
                obuf.at[i % 2, pl.ds(p * BLOCK_ROWS, BLOCK_ROWS)], out_hbm.at[rows], osem.at[i % 2]))
        return copies

    @pl.when(n_chunks > 0)
    def _():
        in_copy(0).start()

    def body(i, carry):
        @pl.when(i + 1 < n_chunks)
        def _():
            in_copy(i + 1).start()

        in_copy(i).wait()
        obuf[i % 2] = jnp.dot(ibuf[i % 2], wbf[...], preferred_element_type=F32).astype(obuf.dtype)
        _start_all(out_copies(i))

        @pl.when(i >= 1)
        def _():
            _wait_all(out_copies(i - 1))
        return carry

    lax.fori_loop(0, n_chunks, body, 0)

    @pl.when(n_chunks > 0)
    def _():
        _wait_all(out_copies(n_chunks - 1))

    @pl.when(e + 1 == pl.num_programs(0))
    def _():
        end = start + n_chunks * EXPERT_CHUNK
        obuf[0] = jnp.zeros(obuf.shape[1:], obuf.dtype)

        def tail_copy(blk):
            rows = pl.ds(pl.multiple_of(end + blk * BLOCK_ROWS, BLOCK_ROWS), BLOCK_ROWS)
            return pltpu.make_async_copy(obuf.at[0, 0:BLOCK_ROWS], out_hbm.at[rows], osem.at[0])

        n_tail = (out_hbm.shape[0] - end) // BLOCK_ROWS
        lax.fori_loop(0, n_tail, lambda blk, c: (tail_copy(blk).start(), c)[1], 0)
        lax.fori_loop(0, n_tail, lambda blk, c: (tail_copy(blk).wait(), c)[1], 0)


def _expert_call(rows_in, w, pad_start, padded, layer, *, out_dtype, name):
    rows, k = rows_in.shape
    n_out = w.shape[-1]
    wshape = w.shape[-2:]
    resident = [_nbytes(wshape, F32), _nbytes(wshape, BF16),
                2 * _nbytes((EXPERT_CHUNK, k), rows_in.dtype),
                2 * _nbytes((EXPERT_CHUNK, n_out), out_dtype),
                2 * _nbytes((EXPERT_CHUNK, wshape[1]), F32)]
    return pl.pallas_call(
        functools.partial(_expert_kernel, layer=layer),
        grid_spec=pltpu.PrefetchScalarGridSpec(
            num_scalar_prefetch=2,
            grid=(N_EXPERTS,),
            in_specs=[pl.BlockSpec(memory_space=pl.ANY), pl.BlockSpec(memory_space=pl.ANY)],
            out_specs=pl.BlockSpec(memory_space=pl.ANY),
            scratch_shapes=[pltpu.VMEM(wshape, F32), pltpu.VMEM(wshape, BF16),
                            pltpu.VMEM((2, EXPERT_CHUNK, k), rows_in.dtype),
                            pltpu.VMEM((2, EXPERT_CHUNK, n_out), out_dtype),
                            pltpu.SemaphoreType.DMA((WEIGHT_DMA_PARTS,)),
                            pltpu.SemaphoreType.DMA((2,)),
                            pltpu.SemaphoreType.DMA((2,))]),
        out_shape=jax.ShapeDtypeStruct((rows, n_out), out_dtype),
        compiler_params=pltpu.CompilerParams(
            dimension_semantics=("arbitrary",), has_side_effects=True,
            vmem_limit_bytes=int(min(VMEM_CAP, sum(resident) + 4 * 1024 * 1024))),
        name=name,
    )(pad_start, padded, rows_in, w)


def expert_out(act, pad_start, padded, moe_w_out, layer):
    return _expert_call(act, moe_w_out, pad_start, padded, layer, out_dtype=F32,
                        name="expert_out")


def _combine_ln_kernel(dest_ref, x_ref, gate_ref, g_ref, b_ref, y_hbm, *rest, tm, t, with_bf16):
    if with_bf16:
        xo_ref, xb_ref, ybuf, sem = rest
    else:
        xo_ref, ybuf, sem = rest
        xb_ref = None
    i = pl.program_id(0)
    n = pl.num_programs(0)

    def start_rows(step):
        buf = step % 2

        def body(r8, carry):
            for u in range(ROW_UNROLL):
                r = r8 * ROW_UNROLL + u
                for k in range(TOP_K):
                    d = dest_ref[k * t + step * tm + r]
                    pltpu.make_async_copy(y_hbm.at[pl.ds(d, 1)], ybuf.at[buf, k, pl.ds(r, 1)],
                                          sem.at[buf]).start(priority=(u + k) % 2)
            return carry
        lax.fori_loop(0, tm // ROW_UNROLL, body, 0)

    def wait_rows(step):
        buf = step % 2
        for k in range(TOP_K):
            pltpu.make_async_copy(y_hbm.at[pl.ds(0, tm)], ybuf.at[buf, k], sem.at[buf]).wait()

    @pl.when(i == 0)
    def _():
        start_rows(i)

    @pl.when(i + 1 < n)
    def _():
        start_rows(i + 1)

    wait_rows(i)
    buf = i % 2
    gate = gate_ref[...]
    h = gate[:, 0:1] * ybuf[buf, 0] + gate[:, 1:2] * ybuf[buf, 1]
    y = _layer_norm_rows(DEEPNORM_ALPHA * x_ref[...] + h, g_ref[...], b_ref[...])
    xo_ref[...] = y
    if with_bf16:
        xb_ref[...] = y.astype(BF16)


def combine_ln(x, y_disp, dest_flat, gate_t, g, b, *, tm, with_bf16):
    t, d = x.shape
    assert t % tm == 0
    row = pl.BlockSpec((tm, d), lambda i, dest: (i, 0))
    vec = pl.BlockSpec((1, d), lambda i, dest: (0, 0))
    out_specs = [row, row] if with_bf16 else [row]
    out_shape = [jax.ShapeDtypeStruct((t, d), F32)]
    if with_bf16:
        out_shape.append(jax.ShapeDtypeStruct((t, d), BF16))
    outs = pl.pallas_call(
        functools.partial(_combine_ln_kernel, tm=tm, t=t, with_bf16=with_bf16),
        grid_spec=pltpu.PrefetchScalarGridSpec(
            num_scalar_prefetch=1,
            grid=(t // tm,),
            in_specs=[row, pl.BlockSpec((tm, TOP_K), lambda i, dest: (i, 0)), vec, vec,
                      pl.BlockSpec(memory_space=pl.ANY)],
            out_specs=out_specs,
            scratch_shapes=[pltpu.VMEM((2, TOP_K, tm, d), F32), pltpu.SemaphoreType.DMA((2,))]),
        out_shape=out_shape,
        compiler_params=_params(
            ("arbitrary",),
            [2 * _nbytes((tm, d), F32), _nbytes((tm, d), BF16)],
            [_nbytes((2, TOP_K, tm, d), F32), 4 * _nbytes((tm, d), F32)]),
        name="moe_combine_ln",
    )(dest_flat, x, gate_t, g.reshape(1, d), b.reshape(1, d), y_disp)
    return (outs[0], outs[1]) if with_bf16 else (outs[0], None)


def moe_ffn_ln(x, xpk, e_idx, gate, rank, counts, moe_w_in, moe_w_out, layer, g, b, *,
               with_bf16, tm):
    t, d = x.shape
    tk = t * TOP_K
    padded = ((counts + BLOCK_ROWS - 1) // BLOCK_ROWS * BLOCK_ROWS).astype(I32)
    pad_end = jnp.cumsum(padded).astype(I32)
    pad_start = pad_end - padded
    n_blocks = (tk + N_EXPERTS * (BLOCK_ROWS - 1) + BLOCK_ROWS - 1) // BLOCK_ROWS + 1
    e_flat = e_idx.reshape(tk)
    dest_flat = pad_start[e_flat] + rank.reshape(tk)
    tok_flat = jnp.arange(tk, dtype=I32) % t
    src = jnp.zeros((n_blocks * BLOCK_ROWS,), I32).at[dest_flat].set(tok_flat)
    n_chunks = (padded + EXPERT_CHUNK - 1) // EXPERT_CHUNK
    chunk_end = jnp.cumsum(n_chunks).astype(I32)
    chunk_base = chunk_end - n_chunks
    max_chunks = n_blocks // (EXPERT_CHUNK // BLOCK_ROWS) + N_EXPERTS
    gidx = jnp.arange(max_chunks, dtype=I32)
    owner = jnp.minimum(jnp.sum(gidx[:, None] >= chunk_end[None, :], axis=1), N_EXPERTS - 1)
    chunk_row = (pad_start[owner] + (gidx - chunk_base[owner]) * EXPERT_CHUNK).astype(I32)
    act = expert_in(xpk, src, chunk_row, chunk_base, n_chunks.astype(I32), chunk_end[-1:],
                    moe_w_in, layer)
    y_disp = expert_out(act, pad_start, padded, moe_w_out, layer)
    return combine_ln(x, y_disp, dest_flat, gate.T, g, b, tm=tm, with_bf16=with_bf16)


def _softplus(x):
    return jnp.maximum(x, 0.0) + jnp.log1p(jnp.exp(-jnp.abs(x)))


def _silu(x):
    return x * jax.nn.sigmoid(x)


def _dot_01_f32(m01, v, *, ones_left):
    hi = v.astype(BF16)
    r = v - hi.astype(F32)
    mid = r.astype(BF16)
    lo = (r - mid.astype(F32)).astype(BF16)

    def dot(p):
        lhs, rhs = (m01, p) if ones_left else (p, m01)
        return jnp.dot(lhs, rhs, preferred_element_type=F32)
    return dot(hi) + dot(mid) + dot(lo)


def _ssd_kernel(xs_ref, b_ref, c_ref, gz_ref, dt_ref, dtT_ref,
                dtb_ref, dtbT_ref, alog_ref, alogT_ref, dsk_ref, nw_ref,
                y_ref, state_ref):
    q = SSD_CHUNK

    @pl.when(pl.program_id(1) == 0)
    def _():
        state_ref[...] = jnp.zeros_like(state_ref)

    xs = xs_ref[...]
    bm = b_ref[...]
    cm = c_ref[...]

    dt = _softplus(dt_ref[...] + dtb_ref[...])
    dt_t = _softplus(dtT_ref[...] + dtbT_ref[...])
    a = -jnp.exp(alog_ref[...])
    a_t = -jnp.exp(alogT_ref[...])
    row = lax.broadcasted_iota(I32, (q, q), 0)
    col = lax.broadcasted_iota(I32, (q, q), 1)
    causal = row >= col
    a_cum = _dot_01_f32(jnp.where(causal, 1.0, 0.0).astype(BF16), dt * a, ones_left=True)
    a_cum_t = _dot_01_f32(jnp.where(row <= col, 1.0, 0.0).astype(BF16), dt_t * a_t,
                          ones_left=False)
    a_last = a_cum[q - 1:q, :]
    hrow = lax.broadcasted_iota(I32, (SSD_HPG, SSD_GW), 0)
    hcol = lax.broadcasted_iota(I32, (SSD_HPG, SSD_GW), 1)
    expand = jnp.where(hcol // SSD_HEAD_DIM == hrow, 1.0, 0.0).astype(BF16)
    decay_in = _dot_01_f32(expand, jnp.exp(a_cum), ones_left=False)
    w_state = _dot_01_f32(expand, jnp.exp(a_last - a_cum) * dt, ones_left=False)
    chunk_decay = decay_in[q - 1:q, :]

    cmb = cm.astype(BF16)
    bmb = bm.astype(BF16)
    cb = lax.dot_general(cmb, bmb, (((1,), (1,)), ((), ())), preferred_element_type=F32)
    prev = state_ref[...]
    y_off = jnp.dot(cmb, prev.astype(BF16), preferred_element_type=F32) * decay_in
    xw = (w_state * xs).astype(BF16)
    state_ref[...] = chunk_decay * prev + jnp.dot(bm.T.astype(BF16), xw, preferred_element_type=F32)

    lane = lax.broadcasted_iota(I32, (q, LANES), 1)
    heads_per_tile = LANES // SSD_HEAD_DIM
    ys = []
    for tile in range(SSD_GW // LANES):
        ms = []
        for hh in range(heads_per_tile):
            h = tile * heads_per_tile + hh
            seg = a_cum[:, h:h + 1] - a_cum_t[h:h + 1, :]
            decay = jnp.where(causal, jnp.exp(seg), 0.0)
            ms.append((cb * decay * dt_t[h:h + 1, :]).astype(BF16))
        x_tile = xs[:, tile * LANES:(tile + 1) * LANES]
        rhs = jnp.concatenate(
            [jnp.where(lane // SSD_HEAD_DIM == hh, x_tile, 0.0).astype(BF16)
             for hh in range(heads_per_tile)], axis=0)
        ys.append(jnp.dot(jnp.concatenate(ms, axis=1), rhs, preferred_element_type=F32))
    y = jnp.concatenate(ys, axis=1) + y_off + dsk_ref[...] * xs
    y = y * gz_ref[...]
    y = y * lax.rsqrt(jnp.mean(y * y, axis=-1, keepdims=True) + RMS_EPS) * nw_ref[...]
    y_ref[...] = y.astype(y_ref.dtype)


def ssd_core(gz, xbc, dt_raw, dt_bias, a_log, d_skip, norm_w):
    t = gz.shape[0]
    q, gw, ns, hpg, ng = SSD_CHUNK, SSD_GW, SSD_STATE, SSD_HPG, SSD_GROUPS
    assert t % q == 0
    b_blk = D_INNER // ns
    c_blk = b_blk + ng
    dt_g = dt_raw.reshape(t, ng, hpg).transpose(1, 0, 2)
    dt_gt = dt_raw.T.reshape(ng, hpg, t)
    dtb = dt_bias.astype(F32).reshape(ng, 1, hpg)
    dtb_t = dt_bias.astype(F32).reshape(ng, hpg, 1)
    alog = a_log.astype(F32).reshape(ng, 1, hpg)
    alog_t = a_log.astype(F32).reshape(ng, hpg, 1)
    dsk = jnp.repeat(d_skip.astype(F32), SSD_HEAD_DIM).reshape(1, D_INNER)
    nw = norm_w.astype(F32).reshape(1, D_INNER)

    def cspec(rows, width, blk):
        return pl.BlockSpec((rows, width), lambda g, c: (0, blk(g)))

    in_specs = [
        pl.BlockSpec((q, gw), lambda g, c: (c, g)),
        pl.BlockSpec((q, ns), lambda g, c: (c, b_blk + g)),
        pl.BlockSpec((q, ns), lambda g, c: (c, c_blk + g)),
        pl.BlockSpec((q, gw), lambda g, c: (c, g)),
        pl.BlockSpec((None, q, hpg), lambda g, c: (g, c, 0)),
        pl.BlockSpec((None, hpg, q), lambda g, c: (g, 0, c)),
        pl.BlockSpec((None, 1, hpg), lambda g, c: (g, 0, 0)),
        pl.BlockSpec((None, hpg, 1), lambda g, c: (g, 0, 0)),
        pl.BlockSpec((None, 1, hpg), lambda g, c: (g, 0, 0)),
        pl.BlockSpec((None, hpg, 1), lambda g, c: (g, 0, 0)),
        cspec(1, gw, lambda g: g), cspec(1, gw, lambda g: g),
    ]
    return pl.pallas_call(
        _ssd_kernel,
        grid=(ng, t // q),
        in_specs=in_specs,
        out_specs=pl.BlockSpec((q, gw), lambda g, c: (c, g)),
        out_shape=jax.ShapeDtypeStruct((t, D_INNER), BF16),
        scratch_shapes=[pltpu.VMEM((ns, gw), F32)],
        compiler_params=_params(
            ("arbitrary", "arbitrary"),
            [2 * _nbytes((q, gw), F32), 2 * _nbytes((q, ns), F32), _nbytes((q, gw), BF16)],
            [_nbytes((ns, gw), F32), 16 * _nbytes((q, gw), F32)]),
        name="ssd_core",
    )(xbc, xbc, xbc, gz, dt_g, dt_gt, dtb, dtb_t, alog, alog_t, dsk, nw)


MM_TM = 1024
MM_TN = 512
WIDE_K_TM = 512
EPILOGUE_PARTS = 8
LN_TM = 256
COMBINE_TM = 128


def pool_mixer(xb, w_in, w_group, scale, w_out):
    p = pool_in(xb, w_in, tm=MM_TM, tn=MM_TN)
    mixed = pool_group_matmul(p, w_group, scale, tm=MM_TM)
    return matmul_cols(mixed, w_out, 0, D_MODEL, tm=MM_TM, tn=MM_TN, out_dtype=F32, name="pool_out")


def ssd_mixer(xb, w_in, conv_w, conv_b, dt_bias, a_log, d_skip, norm_w, w_out):
    gz = matmul_cols_act(xb, w_in, 0, D_INNER, tm=MM_TM, tn=MM_TN, parts=EPILOGUE_PARTS,
                         name="ssd_in_z")
    conv = (conv_w.astype(F32), conv_b.astype(F32).reshape(1, SSD_CONV_DIM))
    xbc = matmul_cols_act(xb, w_in, D_INNER, SSD_CONV_DIM, tm=MM_TM, tn=MM_TN, parts=EPILOGUE_PARTS,
                          conv=conv, name="ssd_in_xbc")
    dt_raw = matmul_cols(xb, w_in, D_INNER + SSD_CONV_DIM, SSD_HEADS, tm=MM_TM, tn=SSD_HEADS,
                         out_dtype=F32, name="ssd_in_dt")
    y = ssd_core(gz, xbc, dt_raw, dt_bias, a_log, d_skip, norm_w)
    return matmul_bf16w(y, w_out.astype(BF16), tm=WIDE_K_TM, tn=MM_TN, out_dtype=F32, name="ssd_out")


def kernel(x, pool_w_in, pool_w_group, pool_scale, pool_w_out, ssd_w_in, ssd_conv_w, ssd_conv_b,
           ssd_dt_bias, ssd_a_log, ssd_d, ssd_norm_w, ssd_w_out, moe_w_router, moe_router_bias,
           moe_w_in, moe_w_out, ln_mix_g, ln_mix_b, ln_ffn_g, ln_ffn_b):
    bsz, seq, d = x.shape
    x = x.reshape(bsz * seq, d)
    assert bsz == 1
    xb = x.astype(BF16)
    for i in range(DEPTH):
        j = i // N_MIXERS
        if i % N_MIXERS == 0:
            h = pool_mixer(xb, pool_w_in[j], pool_w_group[j], pool_scale[j], pool_w_out[j])
        else:
            h = ssd_mixer(xb, ssd_w_in[j], ssd_conv_w[j], ssd_conv_b[j], ssd_dt_bias[j],
                          ssd_a_log[j], ssd_d[j], ssd_norm_w[j], ssd_w_out[j])
        x, xpk, e_idx, gate, rank, counts = ln_router(
            x, h, ln_mix_g[i], ln_mix_b[i], moe_w_router, moe_router_bias, tm=LN_TM)
        x, xb = moe_ffn_ln(x, xpk, e_idx, gate, rank, counts, moe_w_in, moe_w_out, i,
                           ln_ffn_g[i], ln_ffn_b[i], with_bf16=(i + 1 < DEPTH), tm=COMBINE_TM)
    return x.reshape(bsz, seq, d)
```

```python
import functools

import jax
import jax.numpy as jnp
from jax import lax
from jax.experimental import pallas as pl
from jax.experimental.pallas import tpu as pltpu

F32 = jnp.float32
BF16 = jnp.bfloat16
I32 = jnp.int32
U32 = jnp.uint32

D_MODEL = 4096
DEPTH = 2
N_MIXERS = 2
DEEPNORM_ALPHA = (2 * DEPTH) ** 0.25
LN_EPS = 1e-5
POOL_WINDOWS = (2, 4, 8, 16)
POOL_GROUPS = len(POOL_WINDOWS)
POOL_GC = D_MODEL // POOL_GROUPS
D_INNER = 2 * D_MODEL
SSD_HEAD_DIM = 64
SSD_HEADS = D_INNER // SSD_HEAD_DIM
SSD_GROUPS = 8
SSD_HPG = SSD_HEADS // SSD_GROUPS
SSD_STATE = 128
SSD_CONV = 4
SSD_CHUNK = 128
SSD_GW = D_INNER // SSD_GROUPS
SSD_CONV_DIM = D_INNER + 2 * SSD_GROUPS * SSD_STATE
SSD_IN_DIM = D_INNER + SSD_CONV_DIM + SSD_HEADS
RMS_EPS = 1e-5
N_EXPERTS = 32
N_EXPERT_GROUPS = 8
EXPERTS_PER_GROUP = N_EXPERTS // N_EXPERT_GROUPS
TOP_K = 2
D_FF = 768
BLOCK_ROWS = 128

LANES = 128
SUBLANES = 8
VMEM_BYTES_V7X = 64 * 1024 * 1024
VMEM_CAP = VMEM_BYTES_V7X * 7 // 8

POOL_HALO = 16
CONV_HALO = SUBLANES
EXPERT_CHUNK = 2 * BLOCK_ROWS
WEIGHT_DMA_PARTS = 4
ROW_DMA_PRIORITY = 0
WEIGHT_DMA_PRIORITY = 1


def _nbytes(shape, dtype):
    n = 1
    for s in shape:
        n *= s
    return n * jnp.dtype(dtype).itemsize


def _vmem_limit(pipelined, resident):
    est = 2 * sum(pipelined) + sum(resident)
    return int(min(VMEM_CAP, est + max(est // 4, 8 * 1024 * 1024)))


def _params(semantics, pipelined, resident=()):
    return pltpu.CompilerParams(dimension_semantics=semantics,
                                vmem_limit_bytes=_vmem_limit(pipelined, resident))


def _mm_kernel(a_ref, w_ref, o_ref, wbf_ref):
    @pl.when(pl.program_id(1) == 0)
    def _():
        wbf_ref[...] = w_ref[...].astype(BF16)

    o_ref[...] = jnp.dot(a_ref[...], wbf_ref[...], preferred_element_type=F32).astype(o_ref.dtype)


def _mm_scale_kernel(a_ref, w_ref, s_ref, o_ref, wbf_ref):
    @pl.when(pl.program_id(1) == 0)
    def _():
        wbf_ref[...] = w_ref[...].astype(BF16)

    acc = jnp.dot(a_ref[...], wbf_ref[...], preferred_element_type=F32)
    o_ref[...] = (acc * s_ref[...]).astype(o_ref.dtype)


def matmul_cols(a, w, col_off, n_cols, *, tm, tn, out_dtype, name):
    m_rows, k = a.shape
    assert w.shape[0] == k and m_rows % tm == 0 and n_cols % tn == 0 and col_off % tn == 0
    off = col_off // tn
    return pl.pallas_call(
        _mm_kernel,
        grid=(n_cols // tn, m_rows // tm),
        in_specs=[pl.BlockSpec((tm, k), lambda n, m: (m, 0)),
                  pl.BlockSpec((k, tn), lambda n, m: (0, n + off))],
        out_specs=pl.BlockSpec((tm, tn), lambda n, m: (m, n)),
        out_shape=jax.ShapeDtypeStruct((m_rows, n_cols), out_dtype),
        scratch_shapes=[pltpu.VMEM((k, tn), BF16)],
        compiler_params=_params(
            ("arbitrary", "arbitrary"),
            [_nbytes((tm, k), BF16), _nbytes((k, tn), F32), _nbytes((tm, tn), out_dtype)],
            [_nbytes((k, tn), BF16), _nbytes((tm, tn), F32)]),
        name=name,
    )(a, w)


def _mm_silu_kernel(a_ref, w_ref, o_ref, wbf_ref, *, parts):
    @pl.when(pl.program_id(1) == 0)
    def _():
        wbf_ref[...] = w_ref[...].astype(BF16)

    rows = a_ref.shape[0] // parts
    for p in range(parts):
        part = slice(p * rows, (p + 1) * rows)
        o_ref[part, :] = _silu(jnp.dot(a_ref[part, :], wbf_ref[...], preferred_element_type=F32))


def _mm_conv_silu_kernel(a_ref, w_ref, cw_ref, cb_ref, o_ref, wbf_ref, raw_ref, *, parts):
    tm, tn = o_ref.shape

    @pl.when(pl.program_id(1) == 0)
    def _():
        wbf_ref[...] = w_ref[...].astype(BF16)
        raw_ref[0:CONV_HALO, :] = jnp.zeros((CONV_HALO, tn), F32)

    rows = tm // parts
    for p in range(parts):
        lo = CONV_HALO + p * rows
        raw_ref[lo:lo + rows, :] = jnp.dot(a_ref[p * rows:(p + 1) * rows, :], wbf_ref[...],
                                           preferred_element_type=F32)
        for s in range(tn // LANES):
            cols = slice(s * LANES, (s + 1) * LANES)
            acc = cb_ref[:, cols] + cw_ref[SSD_CONV - 1:SSD_CONV, cols] * raw_ref[lo:lo + rows, cols]
            for k in range(SSD_CONV - 1):
                back = SSD_CONV - 1 - k
                acc = acc + cw_ref[k:k + 1, cols] * raw_ref[lo - back:lo - back + rows, cols]
            o_ref[p * rows:(p + 1) * rows, cols] = _silu(acc)
    raw_ref[0:CONV_HALO, :] = raw_ref[tm:tm + CONV_HALO, :]


def matmul_cols_act(a, w, col_off, n_cols, *, tm, tn, parts, conv=None, name):
    m_rows, k = a.shape
    assert w.shape[0] == k and m_rows % tm == 0 and n_cols % tn == 0 and col_off % tn == 0
    assert tm % parts == 0 and (tm // parts) % CONV_HALO == 0
    off = col_off // tn
    in_specs = [pl.BlockSpec((tm, k), lambda n, m: (m, 0)),
                pl.BlockSpec((k, tn), lambda n, m: (0, n + off))]
    scratch = [pltpu.VMEM((k, tn), BF16)]
    operands = [a, w]
    if conv is None:
        body = functools.partial(_mm_silu_kernel, parts=parts)
    else:
        body = functools.partial(_mm_conv_silu_kernel, parts=parts)
        in_specs += [pl.BlockSpec((SSD_CONV, tn), lambda n, m: (0, n)),
                     pl.BlockSpec((1, tn), lambda n, m: (0, n))]
        scratch.append(pltpu.VMEM((CONV_HALO + tm, tn), F32))
        operands += list(conv)
    return pl.pallas_call(
        body,
        grid=(n_cols // tn, m_rows // tm),
        in_specs=in_specs,
        out_specs=pl.BlockSpec((tm, tn), lambda n, m: (m, n)),
        out_shape=jax.ShapeDtypeStruct((m_rows, n_cols), F32),
        scratch_shapes=scratch,
        compiler_params=_params(
            ("arbitrary", "arbitrary"),
            [_nbytes((tm, k), BF16), _nbytes((k, tn), F32), _nbytes((tm, tn), F32)],
            [_nbytes((k, tn), BF16), 2 * _nbytes((tm, tn), F32)]),
        name=name,
    )(*operands)


def _mm_bf16w_kernel(a_ref, w_ref, o_ref):
    o_ref[...] = jnp.dot(a_ref[...], w_ref[...], preferred_element_type=F32).astype(o_ref.dtype)


def matmul_bf16w(a, w, *, tm, tn, out_dtype, name):
    m_rows, k = a.shape
    n_cols = w.shape[1]
    assert w.shape[0] == k and m_rows % tm == 0 and n_cols % tn == 0
    return pl.pallas_call(
        _mm_bf16w_kernel,
        grid=(n_cols // tn, m_rows // tm),
        in_specs=[pl.BlockSpec((tm, k), lambda n, m: (m, 0)),
                  pl.BlockSpec((k, tn), lambda n, m: (0, n))],
        out_specs=pl.BlockSpec((tm, tn), lambda n, m: (m, n)),
        out_shape=jax.ShapeDtypeStruct((m_rows, n_cols), out_dtype),
        compiler_params=_params(
            ("arbitrary", "arbitrary"),
            [_nbytes((tm, k), BF16), _nbytes((k, tn), BF16), _nbytes((tm, tn), out_dtype)],
            [_nbytes((tm, tn), F32)]),
        name=name,
    )(a, w)


def pool_group_matmul(p, w_group, scale, *, tm):
    m_rows = p.shape[0]
    gc = POOL_GC
    assert m_rows % tm == 0
    return pl.pallas_call(
        _mm_scale_kernel,
        grid=(POOL_GROUPS, m_rows // tm),
        in_specs=[pl.BlockSpec((tm, gc), lambda g, m: (m, g)),
                  pl.BlockSpec((None, gc, gc), lambda g, m: (g, 0, 0)),
                  pl.BlockSpec((1, gc), lambda g, m: (0, g))],
        out_specs=pl.BlockSpec((tm, gc), lambda g, m: (m, g)),
        out_shape=jax.ShapeDtypeStruct((m_rows, POOL_GROUPS * gc), BF16),
        scratch_shapes=[pltpu.VMEM((gc, gc), BF16)],
        compiler_params=_params(
            ("arbitrary", "arbitrary"),
            [_nbytes((tm, gc), BF16), _nbytes((gc, gc), F32), _nbytes((tm, gc), BF16)],
            [_nbytes((gc, gc), BF16), _nbytes((tm, gc), F32)]),
        name="pool_group_matmul",
    )(p, w_group, scale.reshape(1, -1))


def _pool_in_kernel(x_ref, w_ref, p_ref, wbf_ref, ubuf_ref, *, tm, tn):
    n = pl.program_id(0)
    m = pl.program_id(1)

    @pl.when(m == 0)
    def _():
        wbf_ref[...] = w_ref[...].astype(BF16)
        ubuf_ref[0:POOL_HALO, :] = jnp.zeros((POOL_HALO, tn), F32)

    u = jnp.dot(x_ref[...], wbf_ref[...], preferred_element_type=F32)
    ubuf_ref[POOL_HALO:POOL_HALO + tm, :] = u
    pos = (m * tm + 1 + lax.broadcasted_iota(I32, (tm, tn), 0)).astype(F32)
    group = (n * tn) // POOL_GC
    for gi, window in enumerate(POOL_WINDOWS):
        @pl.when(group == gi)
        def _(window=window):
            acc = u
            for k in range(1, window):
                acc = acc + ubuf_ref[POOL_HALO - k:POOL_HALO - k + tm, :]
            mean = acc / jnp.minimum(pos, float(window))
            p_ref[...] = (mean - u).astype(p_ref.dtype)

    ubuf_ref[0:POOL_HALO, :] = ubuf_ref[tm:tm + POOL_HALO, :]


def pool_in(xb, w_in, *, tm, tn):
    m_rows, k = xb.shape
    n_cols = w_in.shape[1]
    assert m_rows % tm == 0 and n_cols % tn == 0 and POOL_GC % tn == 0 and tm >= POOL_HALO
    return pl.pallas_call(
        functools.partial(_pool_in_kernel, tm=tm, tn=tn),
        grid=(n_cols // tn, m_rows // tm),
        in_specs=[pl.BlockSpec((tm, k), lambda n, m: (m, 0)),
                  pl.BlockSpec((k, tn), lambda n, m: (0, n))],
        out_specs=pl.BlockSpec((tm, tn), lambda n, m: (m, n)),
        out_shape=jax.ShapeDtypeStruct((m_rows, n_cols), BF16),
        scratch_shapes=[pltpu.VMEM((k, tn), BF16), pltpu.VMEM((tm + POOL_HALO, tn), F32)],
        compiler_params=_params(
            ("arbitrary", "arbitrary"),
            [_nbytes((tm, k), BF16), _nbytes((k, tn), F32), _nbytes((tm, tn), BF16)],
            [_nbytes((k, tn), BF16), 4 * _nbytes((tm + POOL_HALO, tn), F32)]),
        name="pool_in",
    )(xb, w_in)


def _layer_norm_rows(v, g, b):
    mu = jnp.mean(v, axis=-1, keepdims=True)
    vc = v - mu
    var = jnp.mean(vc * vc, axis=-1, keepdims=True)
    return vc * lax.rsqrt(var + LN_EPS) * g + b


def _ln_router_kernel(x_ref, h_ref, g_ref, b_ref, wr_ref, rb_ref,
                      xo_ref, xpk_ref, e_ref, gate_ref, rank_ref, cnt_ref, base_ref, *, tm):
    i = pl.program_id(0)

    @pl.when(i == 0)
    def _():
        base_ref[...] = jnp.zeros_like(base_ref)

    y = _layer_norm_rows(DEEPNORM_ALPHA * x_ref[...] + h_ref[...], g_ref[...], b_ref[...])
    xo_ref[...] = y
    yb = y.astype(BF16)
    half = y.shape[1] // 2
    lo_bits = pltpu.bitcast(yb[:, :half].astype(F32), U32) >> 16
    hi_bits = pltpu.bitcast(yb[:, half:].astype(F32), U32)
    xpk_ref[...] = hi_bits | lo_bits

    ng, epg = N_EXPERT_GROUPS, EXPERTS_PER_GROUP
    logits = lax.dot_general(wr_ref[...], yb, (((1,), (1,)), ((), ())),
                             preferred_element_type=F32)
    scores = jax.nn.sigmoid(logits)
    sel = scores + rb_ref[...]
    sel_j = [sel[ng * j:ng * (j + 1), :] for j in range(epg)]
    sc_j = [scores[ng * j:ng * (j + 1), :] for j in range(epg)]
    gscore = None
    for j1 in range(epg):
        for j2 in range(j1 + 1, epg):
            s = sel_j[j1] + sel_j[j2]
            gscore = s if gscore is None else jnp.maximum(gscore, s)
    gmax = jnp.max(gscore, axis=0, keepdims=True)
    giota = lax.broadcasted_iota(I32, (ng, tm), 0)
    g_idx = jnp.min(jnp.where(gscore == gmax, giota, ng), axis=0, keepdims=True)
    in_g = giota == g_idx
    v = [jnp.sum(jnp.where(in_g, sel_j[j], 0.0), axis=0, keepdims=True) for j in range(epg)]
    s = [jnp.sum(jnp.where(in_g, sc_j[j], 0.0), axis=0, keepdims=True) for j in range(epg)]
    order = []
    for j in range(epg):
        r = jnp.zeros((1, tm), I32)
        for k in range(epg):
            if k == j:
                continue
            beats = (v[k] >= v[j]) if k < j else (v[k] > v[j])
            r = r + beats.astype(I32)
        order.append(r)
    eiota = lax.broadcasted_iota(I32, (N_EXPERTS, tm), 0)
    onehot = jnp.zeros((N_EXPERTS, tm), F32)
    loc, raw, rows = [], [], []
    for slot in range(TOP_K):
        lj = jnp.zeros((1, tm), I32)
        gs = jnp.zeros((1, tm), F32)
        for j in range(epg):
            hit = order[j] == slot
            lj = lj + jnp.where(hit, j, 0)
            gs = gs + jnp.where(hit, s[j], 0.0)
        row = lj * ng + g_idx
        onehot = onehot + (eiota == row).astype(F32)
        loc.append(lj)
        raw.append(gs)
        rows.append(row)
    denom = raw[0] + raw[1]
    srow = lax.broadcasted_iota(I32, (tm, tm), 0)
    scol = lax.broadcasted_iota(I32, (tm, tm), 1)
    before = (srow < scol).astype(BF16)
    prefix = jnp.dot(onehot.astype(BF16), before, preferred_element_type=F32) + base_ref[...]
    for slot in range(TOP_K):
        e_ref[slot:slot + 1, :] = g_idx * epg + loc[slot]
        gate_ref[slot:slot + 1, :] = raw[slot] / denom
        rk = jnp.sum(jnp.where(eiota == rows[slot], prefix, 0.0), axis=0, keepdims=True)
        rank_ref[slot:slot + 1, :] = rk.astype(I32)
    base_ref[...] = base_ref[...] + jnp.sum(onehot, axis=1, keepdims=True)
    cnt_ref[...] = jnp.broadcast_to(base_ref[...], cnt_ref.shape)


def ln_router(x, h, g, b, w_router, router_bias, *, tm):
    t, d = x.shape
    assert t % tm == 0
    ng, epg = N_EXPERT_GROUPS, EXPERTS_PER_GROUP
    wr = w_router.T.reshape(ng, epg, d).transpose(1, 0, 2).reshape(N_EXPERTS, d).astype(BF16)
    rb = router_bias.astype(F32).reshape(ng, epg).T.reshape(N_EXPERTS, 1)
    row = pl.BlockSpec((tm, d), lambda i: (i, 0))
    vec = pl.BlockSpec((1, d), lambda i: (0, 0))
    tok = pl.BlockSpec((TOP_K, tm), lambda i: (0, i))
    outs = pl.pallas_call(
        functools.partial(_ln_router_kernel, tm=tm),
        grid=(t // tm,),
        in_specs=[row, row, vec, vec,
                  pl.BlockSpec((N_EXPERTS, d), lambda i: (0, 0)),
                  pl.BlockSpec((N_EXPERTS, 1), lambda i: (0, 0))],
        out_specs=[row, pl.BlockSpec((tm, d // 2), lambda i: (i, 0)), tok, tok, tok,
                   pl.BlockSpec((N_EXPERTS, LANES), lambda i: (0, 0))],
        out_shape=[jax.ShapeDtypeStruct((t, d), F32), jax.ShapeDtypeStruct((t, d // 2), U32),
                   jax.ShapeDtypeStruct((TOP_K, t), I32), jax.ShapeDtypeStruct((TOP_K, t), F32),
                   jax.ShapeDtypeStruct((TOP_K, t), I32),
                   jax.ShapeDtypeStruct((N_EXPERTS, LANES), F32)],
        scratch_shapes=[pltpu.VMEM((N_EXPERTS, 1), F32)],
        compiler_params=_params(
            ("arbitrary",),
            [3 * _nbytes((tm, d), F32), _nbytes((tm, d), BF16), _nbytes((N_EXPERTS, d), BF16)],
            [4 * _nbytes((tm, d), F32)]),
        name="ln_router",
    )(x, h, g.reshape(1, d), b.reshape(1, d), wr, rb)
    x_new, xpk, e_idx, gate, rank, cnt = outs
    counts = cnt[:, 0].astype(I32).reshape(epg, ng).T.reshape(N_EXPERTS)
    return x_new, xpk, e_idx, gate, rank, counts


def _weight_copies(w_hbm, layer, ex, stage, wsem):
    rows = stage.shape[0] // WEIGHT_DMA_PARTS
    assert rows * WEIGHT_DMA_PARTS == stage.shape[0] and rows % SUBLANES == 0
    return [pltpu.make_async_copy(w_hbm.at[layer, ex, pl.ds(p * rows, rows)],
                                  stage.at[pl.ds(p * rows, rows)], wsem.at[p])
            for p in range(WEIGHT_DMA_PARTS)]


def _start_all(copies, priority=None):
    for p, cp in enumerate(copies):
        cp.start(priority=p % 2 if priority is None else priority)


def _wait_all(copies):
    for cp in copies:
        cp.wait()


def _wait_and_cast(copies, stage, wbf):
    rows = stage.shape[0] // len(copies)
    for p, cp in enumerate(copies):
        cp.wait()
        wbf[p * rows:(p + 1) * rows, :] = stage[p * rows:(p + 1) * rows, :].astype(BF16)


def _expert_in_kernel(src_ref, crow_ref, cbase_ref, nch_ref, total_ref, xpk_hbm, w_hbm, act_hbm,
                      stage, wbf, ibuf, obuf, wsem, isem, osem, *, layer):
    e = pl.program_id(0)
    last = pl.num_programs(0) - 1
    total = total_ref[0]
    half = xpk_hbm.shape[1]

    def w_copies(ex):
        return _weight_copies(w_hbm, layer, ex, stage, wsem)

    def start_gather(g):
        row0 = crow_ref[g]
        for r in range(EXPERT_CHUNK):
            tok = src_ref[row0 + r]
            pltpu.make_async_copy(xpk_hbm.at[pl.ds(tok, 1)], ibuf.at[g % 2, pl.ds(r, 1)],
                                  isem.at[g % 2]).start(priority=ROW_DMA_PRIORITY)

    def wait_gather(g):
        pltpu.make_async_copy(xpk_hbm.at[pl.ds(0, EXPERT_CHUNK)], ibuf.at[g % 2], isem.at[g % 2]).wait()

    def out_copy(g):
        rows = pl.ds(pl.multiple_of(crow_ref[g], BLOCK_ROWS), EXPERT_CHUNK)
        return pltpu.make_async_copy(obuf.at[g % 2], act_hbm.at[rows], osem.at[g % 2])

    @pl.when(e == 0)
    def _():
        _start_all(w_copies(e), priority=WEIGHT_DMA_PRIORITY)

        @pl.when(total > 0)
        def _():
            start_gather(0)

    _wait_and_cast(w_copies(e), stage, wbf)

    @pl.when(e < last)
    def _():
        _start_all(w_copies(e + 1), priority=WEIGHT_DMA_PRIORITY)

    base = cbase_ref[e]

    def body(i, carry):
        g = base + i

        @pl.when(g + 1 < total)
        def _():
            start_gather(g + 1)

        wait_gather(g)
        packed = ibuf[g % 2]
        x_lo = pltpu.bitcast(packed << 16, F32).astype(BF16)
        x_hi = pltpu.bitcast(packed & jnp.uint32(0xFFFF0000), F32).astype(BF16)
        h = (jnp.dot(x_lo, wbf[0:half, :], preferred_element_type=F32)
             + jnp.dot(x_hi, wbf[half:2 * half, :], preferred_element_type=F32))
        h1 = h[:, :D_FF]
        obuf[g % 2] = (h1 * jax.nn.sigmoid(h1) * h[:, D_FF:]).astype(obuf.dtype)

        @pl.when(g >= 1)
        def _():
            out_copy(g - 1).wait()

        out_copy(g).start()
        return carry

    lax.fori_loop(0, nch_ref[e], body, 0)

    @pl.when(e == last)
    def _():
        @pl.when(total > 0)
        def _():
            out_copy(total - 1).wait()

        end = jnp.where(total > 0, crow_ref[jnp.maximum(total - 1, 0)] + EXPERT_CHUNK, 0)
        obuf[0] = jnp.zeros(obuf.shape[1:], obuf.dtype)

        def tail_copy(blk):
            rows = pl.ds(pl.multiple_of(end + blk * BLOCK_ROWS, BLOCK_ROWS), BLOCK_ROWS)
            return pltpu.make_async_copy(obuf.at[0, 0:BLOCK_ROWS], act_hbm.at[rows], osem.at[0])

        n_tail = (act_hbm.shape[0] - end) // BLOCK_ROWS
        lax.fori_loop(0, n_tail, lambda blk, c: (tail_copy(blk).start(), c)[1], 0)
        lax.fori_loop(0, n_tail, lambda blk, c: (tail_copy(blk).wait(), c)[1], 0)


def expert_in(xpk, src, chunk_row, chunk_base, n_chunks, total, moe_w_in, layer):
    rows = src.shape[0]
    half = xpk.shape[1]
    wshape = moe_w_in.shape[-2:]
    resident = [_nbytes(wshape, F32), _nbytes(wshape, BF16),
                2 * _nbytes((EXPERT_CHUNK, half), U32), 2 * _nbytes((EXPERT_CHUNK, D_FF), BF16),
                _nbytes((EXPERT_CHUNK, 2 * half), BF16), 2 * _nbytes((EXPERT_CHUNK, wshape[1]), F32)]
    return pl.pallas_call(
        functools.partial(_expert_in_kernel, layer=layer),
        grid_spec=pltpu.PrefetchScalarGridSpec(
            num_scalar_prefetch=5,
            grid=(N_EXPERTS,),
            in_specs=[pl.BlockSpec(memory_space=pl.ANY), pl.BlockSpec(memory_space=pl.ANY)],
            out_specs=pl.BlockSpec(memory_space=pl.ANY),
            scratch_shapes=[pltpu.VMEM(wshape, F32), pltpu.VMEM(wshape, BF16),
                            pltpu.VMEM((2, EXPERT_CHUNK, half), U32),
                            pltpu.VMEM((2, EXPERT_CHUNK, D_FF), BF16),
                            pltpu.SemaphoreType.DMA((WEIGHT_DMA_PARTS,)),
                            pltpu.SemaphoreType.DMA((2,)),
                            pltpu.SemaphoreType.DMA((2,))]),
        out_shape=jax.ShapeDtypeStruct((rows, D_FF), BF16),
        compiler_params=pltpu.CompilerParams(
            dimension_semantics=("arbitrary",), has_side_effects=True,
            vmem_limit_bytes=int(min(VMEM_CAP, sum(resident) + 4 * 1024 * 1024))),
        name="expert_in",
    )(src, chunk_row, chunk_base, n_chunks, total, xpk, moe_w_in)


def _expert_kernel(pstart_ref, padded_ref, in_hbm, w_hbm, out_hbm,
                   stage, wbf, ibuf, obuf, wsem, isem, osem, *, layer):
    e = pl.program_id(0)

    def w_copies(ex):
        return _weight_copies(w_hbm, layer, ex, stage, wsem)

    @pl.when(e == 0)
    def _():
        _start_all(w_copies(e), priority=WEIGHT_DMA_PRIORITY)

    _wait_and_cast(w_copies(e), stage, wbf)

    @pl.when(e + 1 < pl.num_programs(0))
    def _():
        _start_all(w_copies(e + 1), priority=WEIGHT_DMA_PRIORITY)

    start = pstart_ref[e]
    n_chunks = (padded_ref[e] + EXPERT_CHUNK - 1) // EXPERT_CHUNK

    def chunk_rows(i):
        return pl.ds(pl.multiple_of(start + i * EXPERT_CHUNK, BLOCK_ROWS), EXPERT_CHUNK)

    def in_copy(i):
        return pltpu.make_async_copy(in_hbm.at[chunk_rows(i)], ibuf.at[i % 2], isem.at[i % 2])

    def out_copies(i):
        copies = []
        for p in range(EXPERT_CHUNK // BLOCK_ROWS):
            rows = pl.ds(pl.multiple_of(start + i * EXPERT_CHUNK + p * BLOCK_ROWS, BLOCK_ROWS),
                         BLOCK_ROWS)
            copies.append(pltpu.make_async_copy(
                obuf.at[i % 2, pl.ds(p * BLOCK_ROWS, BLOCK_ROWS)], out_hbm.at[rows], osem.at[i % 2]))
        return copies

    @pl.when(n_chunks > 0)
    def _():
        in_copy(0).start()

    def body(i, carry):
        @pl.when(i + 1 < n_chunks)
        def _():
            in_copy(i + 1).start()

        in_copy(i).wait()
        obuf[i % 2] = jnp.dot(ibuf[i % 2], wbf[...], preferred_element_type=F32).astype(obuf.dtype)
        _start_all(out_copies(i))

        @pl.when(i >= 1)
        def _():
            _wait_all(out_copies(i - 1))
        return carry

    lax.fori_loop(0, n_chunks, body, 0)

    @pl.when(n_chunks > 0)
    def _():
        _wait_all(out_copies(n_chunks - 1))

    @pl.when(e + 1 == pl.num_programs(0))
    def _():
        end = start + n_chunks * EXPERT_CHUNK
        obuf[0] = jnp.zeros(obuf.shape[1:], obuf.dtype)

        def tail_copy(blk):
            rows = pl.ds(pl.multiple_of(end + blk * BLOCK_ROWS, BLOCK_ROWS), BLOCK_ROWS)
            return pltpu.make_async_copy(obuf.at[0, 0:BLOCK_ROWS], out_hbm.at[rows], osem.at[0])

        n_tail = (out_hbm.shape[0] - end) // BLOCK_ROWS
        lax.fori_loop(0, n_tail, lambda blk, c: (tail_copy(blk).start(), c)[1], 0)
        lax.fori_loop(0, n_tail, lambda blk, c: (tail_copy(blk).wait(), c)[1], 0)


def _expert_call(rows_in, w, pad_start, padded, layer, *, out_dtype, name):
    rows, k = rows_in.shape
    n_out = w.shape[-1]
    wshape = w.shape[-2:]
    resident = [_nbytes(wshape, F32), _nbytes(wshape, BF16),
                2 * _nbytes((EXPERT_CHUNK, k), rows_in.dtype),
                2 * _nbytes((EXPERT_CHUNK, n_out), out_dtype),
                2 * _nbytes((EXPERT_CHUNK, wshape[1]), F32)]
    return pl.pallas_call(
        functools.partial(_expert_kernel, layer=layer),
        grid_spec=pltpu.PrefetchScalarGridSpec(
            num_scalar_prefetch=2,
            grid=(N_EXPERTS,),
            in_specs=[pl.BlockSpec(memory_space=pl.ANY), pl.BlockSpec(memory_space=pl.ANY)],
            out_specs=pl.BlockSpec(memory_space=pl.ANY),
            scratch_shapes=[pltpu.VMEM(wshape, F32), pltpu.VMEM(wshape, BF16),
                            pltpu.VMEM((2, EXPERT_CHUNK, k), rows_in.dtype),
                            pltpu.VMEM((2, EXPERT_CHUNK, n_out), out_dtype),
                            pltpu.SemaphoreType.DMA((WEIGHT_DMA_PARTS,)),
                            pltpu.SemaphoreType.DMA((2,)),
                            pltpu.SemaphoreType.DMA((2,))]),
        out_shape=jax.ShapeDtypeStruct((rows, n_out), out_dtype),
        compiler_params=pltpu.CompilerParams(
            dimension_semantics=("arbitrary",), has_side_effects=True,
            vmem_limit_bytes=int(min(VMEM_CAP, sum(resident) + 4 * 1024 * 1024))),
        name=name,
    )(pad_start, padded, rows_in, w)


def expert_out(act, pad_start, padded, moe_w_out, layer):
    return _expert_call(act, moe_w_out, pad_start, padded, layer, out_dtype=F32,
                        name="expert_out")


def _combine_ln_kernel(dest_ref, x_ref, gate_ref, g_ref, b_ref, y_hbm, *rest, tm, t, with_bf16):
    if with_bf16:
        xo_ref, xb_ref, ybuf, sem = rest
    else:
        xo_ref, ybuf, sem = rest
        xb_ref = None
    i = pl.program_id(0)
    n = pl.num_programs(0)

    def start_rows(step):
        buf = step % 2

        for r in range(tm):
            for k in range(TOP_K):
                d = dest_ref[k * t + step * tm + r]
                pltpu.make_async_copy(y_hbm.at[pl.ds(d, 1)], ybuf.at[buf, k, pl.ds(r, 1)],
                                      sem.at[buf]).start(priority=(r + k) % 2)

    def wait_rows(step):
        buf = step % 2
        for k in range(TOP_K):
            pltpu.make_async_copy(y_hbm.at[pl.ds(0, tm)], ybuf.at[buf, k], sem.at[buf]).wait()

    @pl.when(i == 0)
    def _():
        start_rows(i)

    @pl.when(i + 1 < n)
    def _():
        start_rows(i + 1)

    wait_rows(i)
    buf = i % 2
    gate = gate_ref[...]
    h = gate[:, 0:1] * ybuf[buf, 0] + gate[:, 1:2] * ybuf[buf, 1]
    y = _layer_norm_rows(DEEPNORM_ALPHA * x_ref[...] + h, g_ref[...], b_ref[...])
    xo_ref[...] = y
    if with_bf16:
        xb_ref[...] = y.astype(BF16)


def combine_ln(x, y_disp, dest_flat, gate_t, g, b, *, tm, with_bf16):
    t, d = x.shape
    assert t % tm == 0
    row = pl.BlockSpec((tm, d), lambda i, dest: (i, 0))
    vec = pl.BlockSpec((1, d), lambda i, dest: (0, 0))
    out_specs = [row, row] if with_bf16 else [row]
    out_shape = [jax.ShapeDtypeStruct((t, d), F32)]
    if with_bf16:
        out_shape.append(jax.ShapeDtypeStruct((t, d), BF16))
    outs = pl.pallas_call(
        functools.partial(_combine_ln_kernel, tm=tm, t=t, with_bf16=with_bf16),
        grid_spec=pltpu.PrefetchScalarGridSpec(
            num_scalar_prefetch=1,
            grid=(t // tm,),
            in_specs=[row, pl.BlockSpec((tm, TOP_K), lambda i, dest: (i, 0)), vec, vec,
                      pl.BlockSpec(memory_space=pl.ANY)],
            out_specs=out_specs,
            scratch_shapes=[pltpu.VMEM((2, TOP_K, tm, d), F32), pltpu.SemaphoreType.DMA((2,))]),
        out_shape=out_shape,
        compiler_params=_params(
            ("arbitrary",),
            [2 * _nbytes((tm, d), F32), _nbytes((tm, d), BF16)],
            [_nbytes((2, TOP_K, tm, d), F32), 4 * _nbytes((tm, d), F32)]),
        name="moe_combine_ln",
    )(dest_flat, x, gate_t, g.reshape(1, d), b.reshape(1, d), y_disp)
    return (outs[0], outs[1]) if with_bf16 else (outs[0], None)


def moe_ffn_ln(x, xpk, e_idx, gate, rank, counts, moe_w_in, moe_w_out, layer, g, b, *,
               with_bf16, tm):
    t, d = x.shape
    tk = t * TOP_K
    padded = ((counts + BLOCK_ROWS - 1) // BLOCK_ROWS * BLOCK_ROWS).astype(I32)
    pad_end = jnp.cumsum(padded).astype(I32)
    pad_start = pad_end - padded
    n_blocks = (tk + N_EXPERTS * (BLOCK_ROWS - 1) + BLOCK_ROWS - 1) // BLOCK_ROWS + 1
    e_flat = e_idx.reshape(tk)
    dest_flat = pad_start[e_flat] + rank.reshape(tk)
    tok_flat = jnp.arange(tk, dtype=I32) % t
    src = jnp.zeros((n_blocks * BLOCK_ROWS,), I32).at[dest_flat].set(tok_flat)
    n_chunks = (padded + EXPERT_CHUNK - 1) // EXPERT_CHUNK
    chunk_end = jnp.cumsum(n_chunks).astype(I32)
    chunk_base = chunk_end - n_chunks
    max_chunks = n_blocks // (EXPERT_CHUNK // BLOCK_ROWS) + N_EXPERTS
    gidx = jnp.arange(max_chunks, dtype=I32)
    owner = jnp.minimum(jnp.sum(gidx[:, None] >= chunk_end[None, :], axis=1), N_EXPERTS - 1)
    chunk_row = (pad_start[owner] + (gidx - chunk_base[owner]) * EXPERT_CHUNK).astype(I32)
    act = expert_in(xpk, src, chunk_row, chunk_base, n_chunks.astype(I32), chunk_end[-1:],
                    moe_w_in, layer)
    y_disp = expert_out(act, pad_start, padded, moe_w_out, layer)
    return combine_ln(x, y_disp, dest_flat, gate.T, g, b, tm=tm, with_bf16=with_bf16)


def _softplus(x):
    return jnp.maximum(x, 0.0) + jnp.log1p(jnp.exp(-jnp.abs(x)))


def _silu(x):
    return x * jax.nn.sigmoid(x)


def _dot_01_f32(m01, v, *, ones_left):
    hi = v.astype(BF16)
    r = v - hi.astype(F32)
    mid = r.astype(BF16)
    lo = (r - mid.astype(F32)).astype(BF16)

    def dot(p):
        lhs, rhs = (m01, p) if ones_left else (p, m01)
        return jnp.dot(lhs, rhs, preferred_element_type=F32)
    return dot(hi) + dot(mid) + dot(lo)


def _ssd_kernel(xs_ref, b_ref, c_ref, gz_ref, dt_ref, dtT_ref,
                dtb_ref, dtbT_ref, alog_ref, alogT_ref, dsk_ref, nw_ref,
                y_ref, state_ref):
    q = SSD_CHUNK

    @pl.when(pl.program_id(1) == 0)
    def _():
        state_ref[...] = jnp.zeros_like(state_ref)

    xs = xs_ref[...]
    bm = b_ref[...]
    cm = c_ref[...]

    dt = _softplus(dt_ref[...] + dtb_ref[...])
    dt_t = _softplus(dtT_ref[...] + dtbT_ref[...])
    a = -jnp.exp(alog_ref[...])
    a_t = -jnp.exp(alogT_ref[...])
    row = lax.broadcasted_iota(I32, (q, q), 0)
    col = lax.broadcasted_iota(I32, (q, q), 1)
    causal = row >= col
    a_cum = _dot_01_f32(jnp.where(causal, 1.0, 0.0).astype(BF16), dt * a, ones_left=True)
    a_cum_t = _dot_01_f32(jnp.where(row <= col, 1.0, 0.0).astype(BF16), dt_t * a_t,
                          ones_left=False)
    a_last = a_cum[q - 1:q, :]
    hrow = lax.broadcasted_iota(I32, (SSD_HPG, SSD_GW), 0)
    hcol = lax.broadcasted_iota(I32, (SSD_HPG, SSD_GW), 1)
    expand = jnp.where(hcol // SSD_HEAD_DIM == hrow, 1.0, 0.0).astype(BF16)
    decay_in = _dot_01_f32(expand, jnp.exp(a_cum), ones_left=False)
    w_state = _dot_01_f32(expand, jnp.exp(a_last - a_cum) * dt, ones_left=False)
    chunk_decay = decay_in[q - 1:q, :]

    cmb = cm.astype(BF16)
    bmb = bm.astype(BF16)
    cb = lax.dot_general(cmb, bmb, (((1,), (1,)), ((), ())), preferred_element_type=F32)
    prev = state_ref[...]
    y_off = jnp.dot(cmb, prev.astype(BF16), preferred_element_type=F32) * decay_in
    xw = (w_state * xs).astype(BF16)
    state_ref[...] = chunk_decay * prev + jnp.dot(bm.T.astype(BF16), xw, preferred_element_type=F32)

    lane = lax.broadcasted_iota(I32, (q, LANES), 1)
    heads_per_tile = LANES // SSD_HEAD_DIM
    ys = []
    for tile in range(SSD_GW // LANES):
        ms = []
        for hh in range(heads_per_tile):
            h = tile * heads_per_tile + hh
            seg = a_cum[:, h:h + 1] - a_cum_t[h:h + 1, :]
            decay = jnp.where(causal, jnp.exp(seg), 0.0)
            ms.append((cb * decay * dt_t[h:h + 1, :]).astype(BF16))
        x_tile = xs[:, tile * LANES:(tile + 1) * LANES]
        rhs = jnp.concatenate(
            [jnp.where(lane // SSD_HEAD_DIM == hh, x_tile, 0.0).astype(BF16)
             for hh in range(heads_per_tile)], axis=0)
        ys.append(jnp.dot(jnp.concatenate(ms, axis=1), rhs, preferred_element_type=F32))
    y = jnp.concatenate(ys, axis=1) + y_off + dsk_ref[...] * xs
    y = y * gz_ref[...]
    y = y * lax.rsqrt(jnp.mean(y * y, axis=-1, keepdims=True) + RMS_EPS) * nw_ref[...]
    y_ref[...] = y.astype(y_ref.dtype)


def ssd_core(gz, xbc, dt_raw, dt_bias, a_log, d_skip, norm_w):
    t = gz.shape[0]
    q, gw, ns, hpg, ng = SSD_CHUNK, SSD_GW, SSD_STATE, SSD_HPG, SSD_GROUPS
    assert t % q == 0
    b_blk = D_INNER // ns
    c_blk = b_blk + ng
    dt_g = dt_raw.reshape(t, ng, hpg).transpose(1, 0, 2)
    dt_gt = dt_raw.T.reshape(ng, hpg, t)
    dtb = dt_bias.astype(F32).reshape(ng, 1, hpg)
    dtb_t = dt_bias.astype(F32).reshape(ng, hpg, 1)
    alog = a_log.astype(F32).reshape(ng, 1, hpg)
    alog_t = a_log.astype(F32).reshape(ng, hpg, 1)
    dsk = jnp.repeat(d_skip.astype(F32), SSD_HEAD_DIM).reshape(1, D_INNER)
    nw = norm_w.astype(F32).reshape(1, D_INNER)

    def cspec(rows, width, blk):
        return pl.BlockSpec((rows, width), lambda g, c: (0, blk(g)))

    in_specs = [
        pl.BlockSpec((q, gw), lambda g, c: (c, g)),
        pl.BlockSpec((q, ns), lambda g, c: (c, b_blk + g)),
        pl.BlockSpec((q, ns), lambda g, c: (c, c_blk + g)),
        pl.BlockSpec((q, gw), lambda g, c: (c, g)),
        pl.BlockSpec((None, q, hpg), lambda g, c: (g, c, 0)),
        pl.BlockSpec((None, hpg, q), lambda g, c: (g, 0, c)),
        pl.BlockSpec((None, 1, hpg), lambda g, c: (g, 0, 0)),
        pl.BlockSpec((None, hpg, 1), lambda g, c: (g, 0, 0)),
        pl.BlockSpec((None, 1, hpg), lambda g, c: (g, 0, 0)),
        pl.BlockSpec((None, hpg, 1), lambda g, c: (g, 0, 0)),
        cspec(1, gw, lambda g: g), cspec(1, gw, lambda g: g),
    ]
    return pl.pallas_call(
        _ssd_kernel,
        grid=(ng, t // q),
        in_specs=in_specs,
        out_specs=pl.BlockSpec((q, gw), lambda g, c: (c, g)),
        out_shape=jax.ShapeDtypeStruct((t, D_INNER), BF16),
        scratch_shapes=[pltpu.VMEM((ns, gw), F32)],
        compiler_params=_params(
            ("arbitrary", "arbitrary"),
            [2 * _nbytes((q, gw), F32), 2 * _nbytes((q, ns), F32), _nbytes((q, gw), BF16)],
            [_nbytes((ns, gw), F32), 16 * _nbytes((q, gw), F32)]),
        name="ssd_core",
    )(xbc, xbc, xbc, gz, dt_g, dt_gt, dtb, dtb_t, alog, alog_t, dsk, nw)


MM_TM = 1024
MM_TN = 512
WIDE_K_TM = 512
EPILOGUE_PARTS = 8
LN_TM = 256
COMBINE_TM = 128


def pool_mixer(xb, w_in, w_group, scale, w_out):
    p = pool_in(xb, w_in, tm=MM_TM, tn=MM_TN)
    mixed = pool_group_matmul(p, w_group, scale, tm=MM_TM)
    return matmul_cols(mixed, w_out, 0, D_MODEL, tm=MM_TM, tn=MM_TN, out_dtype=F32, name="pool_out")


def ssd_mixer(xb, w_in, conv_w, conv_b, dt_bias, a_log, d_skip, norm_w, w_out):
    gz = matmul_cols_act(xb, w_in, 0, D_INNER, tm=MM_TM, tn=MM_TN, parts=EPILOGUE_PARTS,
                         name="ssd_in_z")
    conv = (conv_w.astype(F32), conv_b.astype(F32).reshape(1, SSD_CONV_DIM))
    xbc = matmul_cols_act(xb, w_in, D_INNER, SSD_CONV_DIM, tm=MM_TM, tn=MM_TN, parts=EPILOGUE_PARTS,
                          conv=conv, name="ssd_in_xbc")
    dt_raw = matmul_cols(xb, w_in, D_INNER + SSD_CONV_DIM, SSD_HEADS, tm=MM_TM, tn=SSD_HEADS,
                         out_dtype=F32, name="ssd_in_dt")
    y = ssd_core(gz, xbc, dt_raw, dt_bias, a_log, d_skip, norm_w)
    return matmul_bf16w(y, w_out.astype(BF16), tm=WIDE_K_TM, tn=MM_TN, out_dtype=F32, name="ssd_out")


def kernel(x, pool_w_in, pool_w_group, pool_scale, pool_w_out, ssd_w_in, ssd_conv_w, ssd_conv_b,
           ssd_dt_bias, ssd_a_log, ssd_d, ssd_norm_w, ssd_w_out, moe_w_router, moe_router_bias,
           moe_w_in, moe_w_out, ln_mix_g, ln_mix_b, ln_ffn_g, ln_ffn_b):
    bsz, seq, d = x.shape
    x = x.reshape(bsz * seq, d)
    assert bsz == 1
    xb = x.astype(BF16)
    for i in range(DEPTH):
        j = i // N_MIXERS
        if i % N_MIXERS == 0:
            h = pool_mixer(xb, pool_w_in[j], pool_w_group[j], pool_scale[j], pool_w_out[j])
        else:
            h = ssd_mixer(xb, ssd_w_in[j], ssd_conv_w[j], ssd_conv_b[j], ssd_dt_bias[j],
                          ssd_a_log[j], ssd_d[j], ssd_norm_w[j], ssd_w_out[j])
        x, xpk, e_idx, gate, rank, counts = ln_router(
            x, h, ln_mix_g[i], ln_mix_b[i], moe_w_router, moe_router_bias, tm=LN_TM)
        x, xb = moe_ffn_ln(x, xpk, e_idx, gate, rank, counts, moe_w_in, moe_w_out, i,
                           ln_ffn_g[i], ln_ffn_b[i], with_bf16=(i + 1 < DEPTH), tm=COMBINE_TM)
    return x.reshape(bsz, seq, d)
```

```python
import functools

import jax
import jax.numpy as jnp
from jax import lax
from jax.experimental import pallas as pl
from jax.experimental.pallas import tpu as pltpu

F32 = jnp.float32
BF16 = jnp.bfloat16
I32 = jnp.int32
U32 = jnp.uint32

D_MODEL = 4096
DEPTH = 2
N_MIXERS = 2
DEEPNORM_ALPHA = (2 * DEPTH) ** 0.25
LN_EPS = 1e-5
POOL_WINDOWS = (2, 4, 8, 16)
POOL_GROUPS = len(POOL_WINDOWS)
POOL_GC = D_MODEL // POOL_GROUPS
D_INNER = 2 * D_MODEL
SSD_HEAD_DIM = 64
SSD_HEADS = D_INNER // SSD_HEAD_DIM
SSD_GROUPS = 8
SSD_HPG = SSD_HEADS // SSD_GROUPS
SSD_STATE = 128
SSD_CONV = 4
SSD_CHUNK = 128
SSD_GW = D_INNER // SSD_GROUPS
SSD_CONV_DIM = D_INNER + 2 * SSD_GROUPS * SSD_STATE
SSD_IN_DIM = D_INNER + SSD_CONV_DIM + SSD_HEADS
RMS_EPS = 1e-5
N_EXPERTS = 32
N_EXPERT_GROUPS = 8
EXPERTS_PER_GROUP = N_EXPERTS // N_EXPERT_GROUPS
TOP_K = 2
D_FF = 768
BLOCK_ROWS = 128

LANES = 128
SUBLANES = 8
VMEM_BYTES_V7X = 64 * 1024 * 1024
VMEM_CAP = VMEM_BYTES_V7X * 7 // 8

POOL_HALO = 16
CONV_HALO = SUBLANES
EXPERT_CHUNK = 2 * BLOCK_ROWS
WEIGHT_DMA_PARTS = 4
ROW_DMA_PRIORITY = 0
WEIGHT_DMA_PRIORITY = 1


def _nbytes(shape, dtype):
    n = 1
    for s in shape:
        n *= s
    return n * jnp.dtype(dtype).itemsize


def _vmem_limit(pipelined, resident):
    est = 2 * sum(pipelined) + sum(resident)
    return int(min(VMEM_CAP, est + max(est // 4, 8 * 1024 * 1024)))


def _params(semantics, pipelined, resident=()):
    return pltpu.CompilerParams(dimension_semantics=semantics,
                                vmem_limit_bytes=_vmem_limit(pipelined, resident))


def _mm_kernel(a_ref, w_ref, o_ref, wbf_ref):
    @pl.when(pl.program_id(1) == 0)
    def _():
        wbf_ref[...] = w_ref[...].astype(BF16)

    o_ref[...] = jnp.dot(a_ref[...], wbf_ref[...], preferred_element_type=F32).astype(o_ref.dtype)


def _mm_scale_kernel(a_ref, w_ref, s_ref, o_ref, wbf_ref):
    @pl.when(pl.program_id(1) == 0)
    def _():
        wbf_ref[...] = w_ref[...].astype(BF16)

    acc = jnp.dot(a_ref[...], wbf_ref[...], preferred_element_type=F32)
    o_ref[...] = (acc * s_ref[...]).astype(o_ref.dtype)


def matmul_cols(a, w, col_off, n_cols, *, tm, tn, out_dtype, name):
    m_rows, k = a.shape
    assert w.shape[0] == k and m_rows % tm == 0 and n_cols % tn == 0 and col_off % tn == 0
    off = col_off // tn
    return pl.pallas_call(
        _mm_kernel,
        grid=(n_cols // tn, m_rows // tm),
        in_specs=[pl.BlockSpec((tm, k), lambda n, m: (m, 0)),
                  pl.BlockSpec((k, tn), lambda n, m: (0, n + off))],
        out_specs=pl.BlockSpec((tm, tn), lambda n, m: (m, n)),
        out_shape=jax.ShapeDtypeStruct((m_rows, n_cols), out_dtype),
        scratch_shapes=[pltpu.VMEM((k, tn), BF16)],
        compiler_params=_params(
            ("arbitrary", "arbitrary"),
            [_nbytes((tm, k), BF16), _nbytes((k, tn), F32), _nbytes((tm, tn), out_dtype)],
            [_nbytes((k, tn), BF16), _nbytes((tm, tn), F32)]),
        name=name,
    )(a, w)


def _mm_silu_kernel(a_ref, w_ref, o_ref, wbf_ref, *, parts):
    @pl.when(pl.program_id(1) == 0)
    def _():
        wbf_ref[...] = w_ref[...].astype(BF16)

    rows = a_ref.shape[0] // parts
    for p in range(parts):
        part = slice(p * rows, (p + 1) * rows)
        o_ref[part, :] = _silu(jnp.dot(a_ref[part, :], wbf_ref[...], preferred_element_type=F32))


def _mm_conv_silu_kernel(a_ref, w_ref, cw_ref, cb_ref, o_ref, wbf_ref, raw_ref, *, parts):
    tm, tn = o_ref.shape

    @pl.when(pl.program_id(1) == 0)
    def _():
        wbf_ref[...] = w_ref[...].astype(BF16)
        raw_ref[0:CONV_HALO, :] = jnp.zeros((CONV_HALO, tn), F32)

    rows = tm // parts
    for p in range(parts):
        lo = CONV_HALO + p * rows
        raw_ref[lo:lo + rows, :] = jnp.dot(a_ref[p * rows:(p + 1) * rows, :], wbf_ref[...],
                                           preferred_element_type=F32)
        for s in range(tn // LANES):
            cols = slice(s * LANES, (s + 1) * LANES)
            acc = cb_ref[:, cols] + cw_ref[SSD_CONV - 1:SSD_CONV, cols] * raw_ref[lo:lo + rows, cols]
            for k in range(SSD_CONV - 1):
                back = SSD_CONV - 1 - k
                acc = acc + cw_ref[k:k + 1, cols] * raw_ref[lo - back:lo - back + rows, cols]
            o_ref[p * rows:(p + 1) * rows, cols] = _silu(acc)
    raw_ref[0:CONV_HALO, :] = raw_ref[tm:tm + CONV_HALO, :]


def matmul_cols_act(a, w, col_off, n_cols, *, tm, tn, parts, conv=None, name):
    m_rows, k = a.shape
    assert w.shape[0] == k and m_rows % tm == 0 and n_cols % tn == 0 and col_off % tn == 0
    assert tm % parts == 0 and (tm // parts) % CONV_HALO == 0
    off = col_off // tn
    in_specs = [pl.BlockSpec((tm, k), lambda n, m: (m, 0)),
                pl.BlockSpec((k, tn), lambda n, m: (0, n + off))]
    scratch = [pltpu.VMEM((k, tn), BF16)]
    operands = [a, w]
    if conv is None:
        body = functools.partial(_mm_silu_kernel, parts=parts)
    else:
        body = functools.partial(_mm_conv_silu_kernel, parts=parts)
        in_specs += [pl.BlockSpec((SSD_CONV, tn), lambda n, m: (0, n)),
                     pl.BlockSpec((1, tn), lambda n, m: (0, n))]
        scratch.append(pltpu.VMEM((CONV_HALO + tm, tn), F32))
        operands += list(conv)
    return pl.pallas_call(
        body,
        grid=(n_cols // tn, m_rows // tm),
        in_specs=in_specs,
        out_specs=pl.BlockSpec((tm, tn), lambda n, m: (m, n)),
        out_shape=jax.ShapeDtypeStruct((m_rows, n_cols), F32),
        scratch_shapes=scratch,
        compiler_params=_params(
            ("arbitrary", "arbitrary"),
            [_nbytes((tm, k), BF16), _nbytes((k, tn), F32), _nbytes((tm, tn), F32)],
            [_nbytes((k, tn), BF16), 2 * _nbytes((tm, tn), F32)]),
        name=name,
    )(*operands)


def _mm_bf16w_kernel(a_ref, w_ref, o_ref):
    o_ref[...] = jnp.dot(a_ref[...], w_ref[...], preferred_element_type=F32).astype(o_ref.dtype)


def matmul_bf16w(a, w, *, tm, tn, out_dtype, name):
    m_rows, k = a.shape
    n_cols = w.shape[1]
    assert w.shape[0] == k and m_rows % tm == 0 and n_cols % tn == 0
    return pl.pallas_call(
        _mm_bf16w_kernel,
        grid=(n_cols // tn, m_rows // tm),
        in_specs=[pl.BlockSpec((tm, k), lambda n, m: (m, 0)),
                  pl.BlockSpec((k, tn), lambda n, m: (0, n))],
        out_specs=pl.BlockSpec((tm, tn), lambda n, m: (m, n)),
        out_shape=jax.ShapeDtypeStruct((m_rows, n_cols), out_dtype),
        compiler_params=_params(
            ("arbitrary", "arbitrary"),
            [_nbytes((tm, k), BF16), _nbytes((k, tn), BF16), _nbytes((tm, tn), out_dtype)],
            [_nbytes((tm, tn), F32)]),
        name=name,
    )(a, w)


def pool_group_matmul(p, w_group, scale, *, tm):
    m_rows = p.shape[0]
    gc = POOL_GC
    assert m_rows % tm == 0
    return pl.pallas_call(
        _mm_scale_kernel,
        grid=(POOL_GROUPS, m_rows // tm),
        in_specs=[pl.BlockSpec((tm, gc), lambda g, m: (m, g)),
                  pl.BlockSpec((None, gc, gc), lambda g, m: (g, 0, 0)),
                  pl.BlockSpec((1, gc), lambda g, m: (0, g))],
        out_specs=pl.BlockSpec((tm, gc), lambda g, m: (m, g)),
        out_shape=jax.ShapeDtypeStruct((m_rows, POOL_GROUPS * gc), BF16),
        scratch_shapes=[pltpu.VMEM((gc, gc), BF16)],
        compiler_params=_params(
            ("arbitrary", "arbitrary"),
            [_nbytes((tm, gc), BF16), _nbytes((gc, gc), F32), _nbytes((tm, gc), BF16)],
            [_nbytes((gc, gc), BF16), _nbytes((tm, gc), F32)]),
        name="pool_group_matmul",
    )(p, w_group, scale.reshape(1, -1))


def _pool_in_kernel(x_ref, w_ref, p_ref, wbf_ref, ubuf_ref, *, tm, tn):
    n = pl.program_id(0)
    m = pl.program_id(1)

    @pl.when(m == 0)
    def _():
        wbf_ref[...] = w_ref[...].astype(BF16)
        ubuf_ref[0:POOL_HALO, :] = jnp.zeros((POOL_HALO, tn), F32)

    u = jnp.dot(x_ref[...], wbf_ref[...], preferred_element_type=F32)
    ubuf_ref[POOL_HALO:POOL_HALO + tm, :] = u
    pos = (m * tm + 1 + lax.broadcasted_iota(I32, (tm, tn), 0)).astype(F32)
    group = (n * tn) // POOL_GC
    for gi, window in enumerate(POOL_WINDOWS):
        @pl.when(group == gi)
        def _(window=window):
            s = ubuf_ref[...]
            span = 1
            while span < window:
                s = s + pltpu.roll(s, span, axis=0)
                span *= 2
            acc = s[POOL_HALO:POOL_HALO + tm, :]
            mean = acc / jnp.minimum(pos, float(window))
            p_ref[...] = (mean - u).astype(p_ref.dtype)

    ubuf_ref[0:POOL_HALO, :] = ubuf_ref[tm:tm + POOL_HALO, :]


def pool_in(xb, w_in, *, tm, tn):
    m_rows, k = xb.shape
    n_cols = w_in.shape[1]
    assert m_rows % tm == 0 and n_cols % tn == 0 and POOL_GC % tn == 0 and tm >= POOL_HALO
    return pl.pallas_call(
        functools.partial(_pool_in_kernel, tm=tm, tn=tn),
        grid=(n_cols // tn, m_rows // tm),
        in_specs=[pl.BlockSpec((tm, k), lambda n, m: (m, 0)),
                  pl.BlockSpec((k, tn), lambda n, m: (0, n))],
        out_specs=pl.BlockSpec((tm, tn), lambda n, m: (m, n)),
        out_shape=jax.ShapeDtypeStruct((m_rows, n_cols), BF16),
        scratch_shapes=[pltpu.VMEM((k, tn), BF16), pltpu.VMEM((tm + POOL_HALO, tn), F32)],
        compiler_params=_params(
            ("arbitrary", "arbitrary"),
            [_nbytes((tm, k), BF16), _nbytes((k, tn), F32), _nbytes((tm, tn), BF16)],
            [_nbytes((k, tn), BF16), 4 * _nbytes((tm + POOL_HALO, tn), F32)]),
        name="pool_in",
    )(xb, w_in)


def _layer_norm_rows(v, g, b):
    mu = jnp.mean(v, axis=-1, keepdims=True)
    vc = v - mu
    var = jnp.mean(vc * vc, axis=-1, keepdims=True)
    return vc * lax.rsqrt(var + LN_EPS) * g + b


def _ln_router_kernel(x_ref, h_ref, g_ref, b_ref, wr_ref, rb_ref,
                      xo_ref, xpk_ref, e_ref, gate_ref, rank_ref, cnt_ref, base_ref, *, tm):
    i = pl.program_id(0)

    @pl.when(i == 0)
    def _():
        base_ref[...] = jnp.zeros_like(base_ref)

    y = _layer_norm_rows(DEEPNORM_ALPHA * x_ref[...] + h_ref[...], g_ref[...], b_ref[...])
    xo_ref[...] = y
    yb = y.astype(BF16)
    half = y.shape[1] // 2
    lo_bits = pltpu.bitcast(yb[:, :half].astype(F32), U32) >> 16
    hi_bits = pltpu.bitcast(yb[:, half:].astype(F32), U32)
    xpk_ref[...] = hi_bits | lo_bits

    ng, epg = N_EXPERT_GROUPS, EXPERTS_PER_GROUP
    logits = lax.dot_general(wr_ref[...], yb, (((1,), (1,)), ((), ())),
                             preferred_element_type=F32)
    scores = jax.nn.sigmoid(logits)
    sel = scores + rb_ref[...]
    sel_j = [sel[ng * j:ng * (j + 1), :] for j in range(epg)]
    sc_j = [scores[ng * j:ng * (j + 1), :] for j in range(epg)]
    gscore = None
    for j1 in range(epg):
        for j2 in range(j1 + 1, epg):
            s = sel_j[j1] + sel_j[j2]
            gscore = s if gscore is None else jnp.maximum(gscore, s)
    gmax = jnp.max(gscore, axis=0, keepdims=True)
    giota = lax.broadcasted_iota(I32, (ng, tm), 0)
    g_idx = jnp.min(jnp.where(gscore == gmax, giota, ng), axis=0, keepdims=True)
    in_g = giota == g_idx
    v = [jnp.sum(jnp.where(in_g, sel_j[j], 0.0), axis=0, keepdims=True) for j in range(epg)]
    s = [jnp.sum(jnp.where(in_g, sc_j[j], 0.0), axis=0, keepdims=True) for j in range(epg)]
    order = []
    for j in range(epg):
        r = jnp.zeros((1, tm), I32)
        for k in range(epg):
            if k == j:
                continue
            beats = (v[k] >= v[j]) if k < j else (v[k] > v[j])
            r = r + beats.astype(I32)
        order.append(r)
    eiota = lax.broadcasted_iota(I32, (N_EXPERTS, tm), 0)
    onehot = jnp.zeros((N_EXPERTS, tm), F32)
    loc, raw, rows = [], [], []
    for slot in range(TOP_K):
        lj = jnp.zeros((1, tm), I32)
        gs = jnp.zeros((1, tm), F32)
        for j in range(epg):
            hit = order[j] == slot
            lj = lj + jnp.where(hit, j, 0)
            gs = gs + jnp.where(hit, s[j], 0.0)
        row = lj * ng + g_idx
        onehot = onehot + (eiota == row).astype(F32)
        loc.append(lj)
        raw.append(gs)
        rows.append(row)
    denom = raw[0] + raw[1]
    srow = lax.broadcasted_iota(I32, (tm, tm), 0)
    scol = lax.broadcasted_iota(I32, (tm, tm), 1)
    before = (srow < scol).astype(BF16)
    prefix = jnp.dot(onehot.astype(BF16), before, preferred_element_type=F32) + base_ref[...]
    for slot in range(TOP_K):
        e_ref[slot:slot + 1, :] = g_idx * epg + loc[slot]
        gate_ref[slot:slot + 1, :] = raw[slot] / denom
        rk = jnp.sum(jnp.where(eiota == rows[slot], prefix, 0.0), axis=0, keepdims=True)
        rank_ref[slot:slot + 1, :] = rk.astype(I32)
    base_ref[...] = base_ref[...] + jnp.sum(onehot, axis=1, keepdims=True)
    cnt_ref[...] = jnp.broadcast_to(base_ref[...], cnt_ref.shape)


def ln_router(x, h, g, b, w_router, router_bias, *, tm):
    t, d = x.shape
    assert t % tm == 0
    ng, epg = N_EXPERT_GROUPS, EXPERTS_PER_GROUP
    wr = w_router.T.reshape(ng, epg, d).transpose(1, 0, 2).reshape(N_EXPERTS, d).astype(BF16)
    rb = router_bias.astype(F32).reshape(ng, epg).T.reshape(N_EXPERTS, 1)
    row = pl.BlockSpec((tm, d), lambda i: (i, 0))
    vec = pl.BlockSpec((1, d), lambda i: (0, 0))
    tok = pl.BlockSpec((TOP_K, tm), lambda i: (0, i))
    outs = pl.pallas_call(
        functools.partial(_ln_router_kernel, tm=tm),
        grid=(t // tm,),
        in_specs=[row, row, vec, vec,
                  pl.BlockSpec((N_EXPERTS, d), lambda i: (0, 0)),
                  pl.BlockSpec((N_EXPERTS, 1), lambda i: (0, 0))],
        out_specs=[row, pl.BlockSpec((tm, d // 2), lambda i: (i, 0)), tok, tok, tok,
                   pl.BlockSpec((N_EXPERTS, LANES), lambda i: (0, 0))],
        out_shape=[jax.ShapeDtypeStruct((t, d), F32), jax.ShapeDtypeStruct((t, d // 2), U32),
                   jax.ShapeDtypeStruct((TOP_K, t), I32), jax.ShapeDtypeStruct((TOP_K, t), F32),
                   jax.ShapeDtypeStruct((TOP_K, t), I32),
                   jax.ShapeDtypeStruct((N_EXPERTS, LANES), F32)],
        scratch_shapes=[pltpu.VMEM((N_EXPERTS, 1), F32)],
        compiler_params=_params(
            ("arbitrary",),
            [3 * _nbytes((tm, d), F32), _nbytes((tm, d), BF16), _nbytes((N_EXPERTS, d), BF16)],
            [4 * _nbytes((tm, d), F32)]),
        name="ln_router",
    )(x, h, g.reshape(1, d), b.reshape(1, d), wr, rb)
    x_new, xpk, e_idx, gate, rank, cnt = outs
    counts = cnt[:, 0].astype(I32).reshape(epg, ng).T.reshape(N_EXPERTS)
    return x_new, xpk, e_idx, gate, rank, counts


def _weight_copies(w_hbm, layer, ex, stage, wsem):
    rows = stage.shape[0] // WEIGHT_DMA_PARTS
    assert rows * WEIGHT_DMA_PARTS == stage.shape[0] and rows % SUBLANES == 0
    return [pltpu.make_async_copy(w_hbm.at[layer, ex, pl.ds(p * rows, rows)],
                                  stage.at[pl.ds(p * rows, rows)], wsem.at[p])
            for p in range(WEIGHT_DMA_PARTS)]


def _start_all(copies, priority=None):
    for p, cp in enumerate(copies):
        cp.start(priority=p % 2 if priority is None else priority)


def _wait_all(copies):
    for cp in copies:
        cp.wait()


def _wait_and_cast(copies, stage, wbf):
    rows = stage.shape[0] // len(copies)
    for p, cp in enumerate(copies):
        cp.wait()
        wbf[p * rows:(p + 1) * rows, :] = stage[p * rows:(p + 1) * rows, :].astype(BF16)


def _expert_in_kernel(src_ref, crow_ref, cbase_ref, nch_ref, total_ref, xpk_hbm, w_hbm, act_hbm,
                      stage, wbf, ibuf, obuf, wsem, isem, osem, *, layer):
    e = pl.program_id(0)
    last = pl.num_programs(0) - 1
    total = total_ref[0]
    half = xpk_hbm.shape[1]

    def w_copies(ex):
        return _weight_copies(w_hbm, layer, ex, stage, wsem)

    def start_gather(g):
        row0 = crow_ref[g]
        for r in range(EXPERT_CHUNK):
            tok = src_ref[row0 + r]
            pltpu.make_async_copy(xpk_hbm.at[pl.ds(tok, 1)], ibuf.at[g % 2, pl.ds(r, 1)],
                                  isem.at[g % 2]).start(priority=ROW_DMA_PRIORITY)

    def wait_gather(g):
        pltpu.make_async_copy(xpk_hbm.at[pl.ds(0, EXPERT_CHUNK)], ibuf.at[g % 2], isem.at[g % 2]).wait()

    def out_copy(g):
        rows = pl.ds(pl.multiple_of(crow_ref[g], BLOCK_ROWS), EXPERT_CHUNK)
        return pltpu.make_async_copy(obuf.at[g % 2], act_hbm.at[rows], osem.at[g % 2])

    @pl.when(e == 0)
    def _():
        _start_all(w_copies(e), priority=WEIGHT_DMA_PRIORITY)

        @pl.when(total > 0)
        def _():
            start_gather(0)

    _wait_and_cast(w_copies(e), stage, wbf)

    @pl.when(e < last)
    def _():
        _start_all(w_copies(e + 1), priority=WEIGHT_DMA_PRIORITY)

    base = cbase_ref[e]

    def body(i, carry):
        g = base + i

        @pl.when(g + 1 < total)
        def _():
            start_gather(g + 1)

        wait_gather(g)
        packed = ibuf[g % 2]
        x_lo = pltpu.bitcast(packed << 16, F32).astype(BF16)
        x_hi = pltpu.bitcast(packed & jnp.uint32(0xFFFF0000), F32).astype(BF16)
        h = (jnp.dot(x_lo, wbf[0:half, :], preferred_element_type=F32)
             + jnp.dot(x_hi, wbf[half:2 * half, :], preferred_element_type=F32))
        h1 = h[:, :D_FF]
        obuf[g % 2] = (h1 * jax.nn.sigmoid(h1) * h[:, D_FF:]).astype(obuf.dtype)

        @pl.when(g >= 1)
        def _():
            out_copy(g - 1).wait()

        out_copy(g).start()
        return carry

    lax.fori_loop(0, nch_ref[e], body, 0)

    @pl.when(e == last)
    def _():
        @pl.when(total > 0)
        def _():
            out_copy(total - 1).wait()

        end = jnp.where(total > 0, crow_ref[jnp.maximum(total - 1, 0)] + EXPERT_CHUNK, 0)
        obuf[0] = jnp.zeros(obuf.shape[1:], obuf.dtype)

        def tail_copy(blk):
            rows = pl.ds(pl.multiple_of(end + blk * BLOCK_ROWS, BLOCK_ROWS), BLOCK_ROWS)
            return pltpu.make_async_copy(obuf.at[0, 0:BLOCK_ROWS], act_hbm.at[rows], osem.at[0])

        n_tail = (act_hbm.shape[0] - end) // BLOCK_ROWS
        lax.fori_loop(0, n_tail, lambda blk, c: (tail_copy(blk).start(), c)[1], 0)
        lax.fori_loop(0, n_tail, lambda blk, c: (tail_copy(blk).wait(), c)[1], 0)


def expert_in(xpk, src, chunk_row, chunk_base, n_chunks, total, moe_w_in, layer):
    rows = src.shape[0]
    half = xpk.shape[1]
    wshape = moe_w_in.shape[-2:]
    resident = [_nbytes(wshape, F32), _nbytes(wshape, BF16),
                2 * _nbytes((EXPERT_CHUNK, half), U32), 2 * _nbytes((EXPERT_CHUNK, D_FF), BF16),
                _nbytes((EXPERT_CHUNK, 2 * half), BF16), 2 * _nbytes((EXPERT_CHUNK, wshape[1]), F32)]
    return pl.pallas_call(
        functools.partial(_expert_in_kernel, layer=layer),
        grid_spec=pltpu.PrefetchScalarGridSpec(
            num_scalar_prefetch=5,
            grid=(N_EXPERTS,),
            in_specs=[pl.BlockSpec(memory_space=pl.ANY), pl.BlockSpec(memory_space=pl.ANY)],
            out_specs=pl.BlockSpec(memory_space=pl.ANY),
            scratch_shapes=[pltpu.VMEM(wshape, F32), pltpu.VMEM(wshape, BF16),
                            pltpu.VMEM((2, EXPERT_CHUNK, half), U32),
                            pltpu.VMEM((2, EXPERT_CHUNK, D_FF), BF16),
                            pltpu.SemaphoreType.DMA((WEIGHT_DMA_PARTS,)),
                            pltpu.SemaphoreType.DMA((2,)),
                            pltpu.SemaphoreType.DMA((2,))]),
        out_shape=jax.ShapeDtypeStruct((rows, D_FF), BF16),
        compiler_params=pltpu.CompilerParams(
            dimension_semantics=("arbitrary",), has_side_effects=True,
            vmem_limit_bytes=int(min(VMEM_CAP, sum(resident) + 4 * 1024 * 1024))),
        name="expert_in",
    )(src, chunk_row, chunk_base, n_chunks, total, xpk, moe_w_in)


def _expert_kernel(pstart_ref, padded_ref, in_hbm, w_hbm, out_hbm,
                   stage, wbf, ibuf, obuf, wsem, isem, osem, *, layer):
    e = pl.program_id(0)

    def w_copies(ex):
        return _weight_copies(w_hbm, layer, ex, stage, wsem)

    @pl.when(e == 0)
    def _():
        _start_all(w_copies(e), priority=WEIGHT_DMA_PRIORITY)

    _wait_and_cast(w_copies(e), stage, wbf)

    @pl.when(e + 1 < pl.num_programs(0))
    def _():
        _start_all(w_copies(e + 1), priority=WEIGHT_DMA_PRIORITY)

    start = pstart_ref[e]
    n_chunks = (padded_ref[e] + EXPERT_CHUNK - 1) // EXPERT_CHUNK

    def chunk_rows(i):
        return pl.ds(pl.multiple_of(start + i * EXPERT_CHUNK, BLOCK_ROWS), EXPERT_CHUNK)

    def in_copy(i):
        return pltpu.make_async_copy(in_hbm.at[chunk_rows(i)], ibuf.at[i % 2], isem.at[i % 2])

    def out_copies(i):
        copies = []
        for p in range(EXPERT_CHUNK // BLOCK_ROWS):
            rows = pl.ds(pl.multiple_of(start + i * EXPERT_CHUNK + p * BLOCK_ROWS, BLOCK_ROWS),
                         BLOCK_ROWS)
            copies.append(pltpu.make_async_copy(
                obuf.at[i % 2, pl.ds(p * BLOCK_ROWS, BLOCK_ROWS)], out_hbm.at[rows], osem.at[i % 2]))
        return copies

    @pl.when(n_chunks > 0)
    def _():
        in_copy(0).start()

    def body(i, carry):
        @pl.when(i + 1 < n_chunks)
        def _():
            in_copy(i + 1).start()

        in_copy(i).wait()
        obuf[i % 2] = jnp.dot(ibuf[i % 2], wbf[...], preferred_element_type=F32).astype(obuf.dtype)
        _start_all(out_copies(i))

        @pl.when(i >= 1)
        def _():
            _wait_all(out_copies(i - 1))
        return carry

    lax.fori_loop(0, n_chunks, body, 0)

    @pl.when(n_chunks > 0)
    def _():
        _wait_all(out_copies(n_chunks - 1))

    @pl.when(e + 1 == pl.num_programs(0))
    def _():
        end = start + n_chunks * EXPERT_CHUNK
        obuf[0] = jnp.zeros(obuf.shape[1:], obuf.dtype)

        def tail_copy(blk):
            rows = pl.ds(pl.multiple_of(end + blk * BLOCK_ROWS, BLOCK_ROWS), BLOCK_ROWS)
            return pltpu.make_async_copy(obuf.at[0, 0:BLOCK_ROWS], out_hbm.at[rows], osem.at[0])

        n_tail = (out_hbm.shape[0] - end) // BLOCK_ROWS
        lax.fori_loop(0, n_tail, lambda blk, c: (tail_copy(blk).start(), c)[1], 0)
        lax.fori_loop(0, n_tail, lambda blk, c: (tail_copy(blk).wait(), c)[1], 0)


def _expert_call(rows_in, w, pad_start, padded, layer, *, out_dtype, name):
    rows, k = rows_in.shape
    n_out = w.shape[-1]
    wshape = w.shape[-2:]
    resident = [_nbytes(wshape, F32), _nbytes(wshape, BF16),
                2 * _nbytes((EXPERT_CHUNK, k), rows_in.dtype),
                2 * _nbytes((EXPERT_CHUNK, n_out), out_dtype),
                2 * _nbytes((EXPERT_CHUNK, wshape[1]), F32)]
    return pl.pallas_call(
        functools.partial(_expert_kernel, layer=layer),
        grid_spec=pltpu.PrefetchScalarGridSpec(
            num_scalar_prefetch=2,
            grid=(N_EXPERTS,),
            in_specs=[pl.BlockSpec(memory_space=pl.ANY), pl.BlockSpec(memory_space=pl.ANY)],
            out_specs=pl.BlockSpec(memory_space=pl.ANY),
            scratch_shapes=[pltpu.VMEM(wshape, F32), pltpu.VMEM(wshape, BF16),
                            pltpu.VMEM((2, EXPERT_CHUNK, k), rows_in.dtype),
                            pltpu.VMEM((2, EXPERT_CHUNK, n_out), out_dtype),
                            pltpu.SemaphoreType.DMA((WEIGHT_DMA_PARTS,)),
                            pltpu.SemaphoreType.DMA((2,)),
                            pltpu.SemaphoreType.DMA((2,))]),
        out_shape=jax.ShapeDtypeStruct((rows, n_out), out_dtype),
        compiler_params=pltpu.CompilerParams(
            dimension_semantics=("arbitrary",), has_side_effects=True,
            vmem_limit_bytes=int(min(VMEM_CAP, sum(resident) + 4 * 1024 * 1024))),
        name=name,
    )(pad_start, padded, rows_in, w)


def expert_out(act, pad_start, padded, moe_w_out, layer):
    return _expert_call(act, moe_w_out, pad_start, padded, layer, out_dtype=F32,
                        name="expert_out")


def _combine_ln_kernel(dest_ref, x_ref, gate_ref, g_ref, b_ref, y_hbm, *rest, tm, t, with_bf16):
    if with_bf16:
        xo_ref, xb_ref, ybuf, sem = rest
    else:
        xo_ref, ybuf, sem = rest
        xb_ref = None
    i = pl.program_id(0)
    n = pl.num_programs(0)

    def start_rows(step):
        buf = step % 2

        for r in range(tm):
            for k in range(TOP_K):
                d = dest_ref[k * t + step * tm + r]
                pltpu.make_async_copy(y_hbm.at[pl.ds(d, 1)], ybuf.at[buf, k, pl.ds(r, 1)],
                                      sem.at[buf]).start(priority=(r + k) % 2)

    def wait_rows(step):
        buf = step % 2
        for k in range(TOP_K):
            pltpu.make_async_copy(y_hbm.at[pl.ds(0, tm)], ybuf.at[buf, k], sem.at[buf]).wait()

    @pl.when(i == 0)
    def _():
        start_rows(i)

    @pl.when(i + 1 < n)
    def _():
        start_rows(i + 1)

    wait_rows(i)
    buf = i % 2
    gate = gate_ref[...]
    h = gate[:, 0:1] * ybuf[buf, 0] + gate[:, 1:2] * ybuf[buf, 1]
    y = _layer_norm_rows(DEEPNORM_ALPHA * x_ref[...] + h, g_ref[...], b_ref[...])
    xo_ref[...] = y
    if with_bf16:
        xb_ref[...] = y.astype(BF16)


def combine_ln(x, y_disp, dest_flat, gate_t, g, b, *, tm, with_bf16):
    t, d = x.shape
    assert t % tm == 0
    row = pl.BlockSpec((tm, d), lambda i, dest: (i, 0))
    vec = pl.BlockSpec((1, d), lambda i, dest: (0, 0))
    out_specs = [row, row] if with_bf16 else [row]
    out_shape = [jax.ShapeDtypeStruct((t, d), F32)]
    if with_bf16:
        out_shape.append(jax.ShapeDtypeStruct((t, d), BF16))
    outs = pl.pallas_call(
        functools.partial(_combine_ln_kernel, tm=tm, t=t, with_bf16=with_bf16),
        grid_spec=pltpu.PrefetchScalarGridSpec(
            num_scalar_prefetch=1,
            grid=(t // tm,),
            in_specs=[row, pl.BlockSpec((tm, TOP_K), lambda i, dest: (i, 0)), vec, vec,
                      pl.BlockSpec(memory_space=pl.ANY)],
            out_specs=out_specs,
            scratch_shapes=[pltpu.VMEM((2, TOP_K, tm, d), F32), pltpu.SemaphoreType.DMA((2,))]),
        out_shape=out_shape,
        compiler_params=_params(
            ("arbitrary",),
            [2 * _nbytes((tm, d), F32), _nbytes((tm, d), BF16)],
            [_nbytes((2, TOP_K, tm, d), F32), 4 * _nbytes((tm, d), F32)]),
        name="moe_combine_ln",
    )(dest_flat, x, gate_t, g.reshape(1, d), b.reshape(1, d), y_disp)
    return (outs[0], outs[1]) if with_bf16 else (outs[0], None)


def moe_ffn_ln(x, xpk, e_idx, gate, rank, counts, moe_w_in, moe_w_out, layer, g, b, *,
               with_bf16, tm):
    t, d = x.shape
    tk = t * TOP_K
    padded = ((counts + BLOCK_ROWS - 1) // BLOCK_ROWS * BLOCK_ROWS).astype(I32)
    pad_end = jnp.cumsum(padded).astype(I32)
    pad_start = pad_end - padded
    n_blocks = (tk + N_EXPERTS * (BLOCK_ROWS - 1) + BLOCK_ROWS - 1) // BLOCK_ROWS + 1
    e_flat = e_idx.reshape(tk)
    dest_flat = pad_start[e_flat] + rank.reshape(tk)
    tok_flat = jnp.arange(tk, dtype=I32) % t
    src = jnp.zeros((n_blocks * BLOCK_ROWS,), I32).at[dest_flat].set(tok_flat)
    n_chunks = (padded + EXPERT_CHUNK - 1) // EXPERT_CHUNK
    chunk_end = jnp.cumsum(n_chunks).astype(I32)
    chunk_base = chunk_end - n_chunks
    max_chunks = n_blocks // (EXPERT_CHUNK // BLOCK_ROWS) + N_EXPERTS
    gidx = jnp.arange(max_chunks, dtype=I32)
    owner = jnp.minimum(jnp.sum(gidx[:, None] >= chunk_end[None, :], axis=1), N_EXPERTS - 1)
    chunk_row = (pad_start[owner] + (gidx - chunk_base[owner]) * EXPERT_CHUNK).astype(I32)
    act = expert_in(xpk, src, chunk_row, chunk_base, n_chunks.astype(I32), chunk_end[-1:],
                    moe_w_in, layer)
    y_disp = expert_out(act, pad_start, padded, moe_w_out, layer)
    return combine_ln(x, y_disp, dest_flat, gate.T, g, b, tm=tm, with_bf16=with_bf16)


def _softplus(x):
    return jnp.maximum(x, 0.0) + jnp.log1p(jnp.exp(-jnp.abs(x)))


def _silu(x):
    return x * jax.nn.sigmoid(x)


def _dot_01_f32(m01, v, *, ones_left):
    hi = v.astype(BF16)
    r = v - hi.astype(F32)
    mid = r.astype(BF16)
    lo = (r - mid.astype(F32)).astype(BF16)

    def dot(p):
        lhs, rhs = (m01, p) if ones_left else (p, m01)
        return jnp.dot(lhs, rhs, preferred_element_type=F32)
    return dot(hi) + dot(mid) + dot(lo)


def _ssd_kernel(xs_ref, b_ref, c_ref, gz_ref, dt_ref, dtT_ref,
                dtb_ref, dtbT_ref, alog_ref, alogT_ref, dsk_ref, nw_ref,
                y_ref, state_ref):
    q = SSD_CHUNK

    @pl.when(pl.program_id(1) == 0)
    def _():
        state_ref[...] = jnp.zeros_like(state_ref)

    xs = xs_ref[...]
    bm = b_ref[...]
    cm = c_ref[...]

    dt = _softplus(dt_ref[...] + dtb_ref[...])
    dt_t = _softplus(dtT_ref[...] + dtbT_ref[...])
    a = -jnp.exp(alog_ref[...])
    a_t = -jnp.exp(alogT_ref[...])
    row = lax.broadcasted_iota(I32, (q, q), 0)
    col = lax.broadcasted_iota(I32, (q, q), 1)
    causal = row >= col
    a_cum = _dot_01_f32(jnp.where(causal, 1.0, 0.0).astype(BF16), dt * a, ones_left=True)
    a_cum_t = _dot_01_f32(jnp.where(row <= col, 1.0, 0.0).astype(BF16), dt_t * a_t,
                          ones_left=False)
    a_last = a_cum[q - 1:q, :]
    hrow = lax.broadcasted_iota(I32, (SSD_HPG, SSD_GW), 0)
    hcol = lax.broadcasted_iota(I32, (SSD_HPG, SSD_GW), 1)
    expand = jnp.where(hcol // SSD_HEAD_DIM == hrow, 1.0, 0.0).astype(BF16)
    decay_in = _dot_01_f32(expand, jnp.exp(a_cum), ones_left=False)
    w_state = _dot_01_f32(expand, jnp.exp(a_last - a_cum) * dt, ones_left=False)
    chunk_decay = decay_in[q - 1:q, :]

    cmb = cm.astype(BF16)
    bmb = bm.astype(BF16)
    cb = lax.dot_general(cmb, bmb, (((1,), (1,)), ((), ())), preferred_element_type=F32)
    prev = state_ref[...]
    y_off = jnp.dot(cmb, prev.astype(BF16), preferred_element_type=F32) * decay_in
    xw = (w_state * xs).astype(BF16)
    state_ref[...] = chunk_decay * prev + jnp.dot(bm.T.astype(BF16), xw, preferred_element_type=F32)

    lane = lax.broadcasted_iota(I32, (q, LANES), 1)
    heads_per_tile = LANES // SSD_HEAD_DIM
    ys = []
    for tile in range(SSD_GW // LANES):
        ms = []
        for hh in range(heads_per_tile):
            h = tile * heads_per_tile + hh
            seg = a_cum[:, h:h + 1] - a_cum_t[h:h + 1, :]
            decay = jnp.where(causal, jnp.exp(seg), 0.0)
            ms.append((cb * decay * dt_t[h:h + 1, :]).astype(BF16))
        x_tile = xs[:, tile * LANES:(tile + 1) * LANES]
        rhs = jnp.concatenate(
            [jnp.where(lane // SSD_HEAD_DIM == hh, x_tile, 0.0).astype(BF16)
             for hh in range(heads_per_tile)], axis=0)
        ys.append(jnp.dot(jnp.concatenate(ms, axis=1), rhs, preferred_element_type=F32))
    y = jnp.concatenate(ys, axis=1) + y_off + dsk_ref[...] * xs
    y = y * gz_ref[...]
    y = y * lax.rsqrt(jnp.mean(y * y, axis=-1, keepdims=True) + RMS_EPS) * nw_ref[...]
    y_ref[...] = y.astype(y_ref.dtype)


def ssd_core(gz, xbc, dt_raw, dt_bias, a_log, d_skip, norm_w):
    t = gz.shape[0]
    q, gw, ns, hpg, ng = SSD_CHUNK, SSD_GW, SSD_STATE, SSD_HPG, SSD_GROUPS
    assert t % q == 0
    b_blk = D_INNER // ns
    c_blk = b_blk + ng
    dt_g = dt_raw.reshape(t, ng, hpg).transpose(1, 0, 2)
    dt_gt = dt_raw.T.reshape(ng, hpg, t)
    dtb = dt_bias.astype(F32).reshape(ng, 1, hpg)
    dtb_t = dt_bias.astype(F32).reshape(ng, hpg, 1)
    alog = a_log.astype(F32).reshape(ng, 1, hpg)
    alog_t = a_log.astype(F32).reshape(ng, hpg, 1)
    dsk = jnp.repeat(d_skip.astype(F32), SSD_HEAD_DIM).reshape(1, D_INNER)
    nw = norm_w.astype(F32).reshape(1, D_INNER)

    def cspec(rows, width, blk):
        return pl.BlockSpec((rows, width), lambda g, c: (0, blk(g)))

    in_specs = [
        pl.BlockSpec((q, gw), lambda g, c: (c, g)),
        pl.BlockSpec((q, ns), lambda g, c: (c, b_blk + g)),
        pl.BlockSpec((q, ns), lambda g, c: (c, c_blk + g)),
        pl.BlockSpec((q, gw), lambda g, c: (c, g)),
        pl.BlockSpec((None, q, hpg), lambda g, c: (g, c, 0)),
        pl.BlockSpec((None, hpg, q), lambda g, c: (g, 0, c)),
        pl.BlockSpec((None, 1, hpg), lambda g, c: (g, 0, 0)),
        pl.BlockSpec((None, hpg, 1), lambda g, c: (g, 0, 0)),
        pl.BlockSpec((None, 1, hpg), lambda g, c: (g, 0, 0)),
        pl.BlockSpec((None, hpg, 1), lambda g, c: (g, 0, 0)),
        cspec(1, gw, lambda g: g), cspec(1, gw, lambda g: g),
    ]
    return pl.pallas_call(
        _ssd_kernel,
        grid=(ng, t // q),
        in_specs=in_specs,
        out_specs=pl.BlockSpec((q, gw), lambda g, c: (c, g)),
        out_shape=jax.ShapeDtypeStruct((t, D_INNER), BF16),
        scratch_shapes=[pltpu.VMEM((ns, gw), F32)],
        compiler_params=_params(
            ("arbitrary", "arbitrary"),
            [2 * _nbytes((q, gw), F32), 2 * _nbytes((q, ns), F32), _nbytes((q, gw), BF16)],
            [_nbytes((ns, gw), F32), 16 * _nbytes((q, gw), F32)]),
        name="ssd_core",
    )(xbc, xbc, xbc, gz, dt_g, dt_gt, dtb, dtb_t, alog, alog_t, dsk, nw)


MM_TM = 1024
MM_TN = 512
WIDE_K_TM = 512
EPILOGUE_PARTS = 8
LN_TM = 256
COMBINE_TM = 128


def pool_mixer(xb, w_in, w_group, scale, w_out):
    p = pool_in(xb, w_in, tm=MM_TM, tn=MM_TN)
    mixed = pool_group_matmul(p, w_group, scale, tm=MM_TM)
    return matmul_cols(mixed, w_out, 0, D_MODEL, tm=MM_TM, tn=MM_TN, out_dtype=F32, name="pool_out")


def ssd_mixer(xb, w_in, conv_w, conv_b, dt_bias, a_log, d_skip, norm_w, w_out):
    gz = matmul_cols_act(xb, w_in, 0, D_INNER, tm=MM_TM, tn=MM_TN, parts=EPILOGUE_PARTS,
                         name="ssd_in_z")
    conv = (conv_w.astype(F32), conv_b.astype(F32).reshape(1, SSD_CONV_DIM))
    xbc = matmul_cols_act(xb, w_in, D_INNER, SSD_CONV_DIM, tm=MM_TM, tn=MM_TN, parts=EPILOGUE_PARTS,
                          conv=conv, name="ssd_in_xbc")
    dt_raw = matmul_cols(xb, w_in, D_INNER + SSD_CONV_DIM, SSD_HEADS, tm=MM_TM, tn=SSD_HEADS,
                         out_dtype=F32, name="ssd_in_dt")
    y = ssd_core(gz, xbc, dt_raw, dt_bias, a_log, d_skip, norm_w)
    return matmul_bf16w(y, w_out.astype(BF16), tm=WIDE_K_TM, tn=MM_TN, out_dtype=F32, name="ssd_out")


def kernel(x, pool_w_in, pool_w_group, pool_scale, pool_w_out, ssd_w_in, ssd_conv_w, ssd_conv_b,
           ssd_dt_bias, ssd_a_log, ssd_d, ssd_norm_w, ssd_w_out, moe_w_router, moe_router_bias,
           moe_w_in, moe_w_out, ln_mix_g, ln_mix_b, ln_ffn_g, ln_ffn_b):
    bsz, seq, d = x.shape
    x = x.reshape(bsz * seq, d)
    assert bsz == 1
    xb = x.astype(BF16)
    for i in range(DEPTH):
        j = i // N_MIXERS
        if i % N_MIXERS == 0:
            h = pool_mixer(xb, pool_w_in[j], pool_w_group[j], pool_scale[j], pool_w_out[j])
        else:
            h = ssd_mixer(xb, ssd_w_in[j], ssd_conv_w[j], ssd_conv_b[j], ssd_dt_bias[j],
                          ssd_a_log[j], ssd_d[j], ssd_norm_w[j], ssd_w_out[j])
        x, xpk, e_idx, gate, rank, counts = ln_router(
            x, h, ln_mix_g[i], ln_mix_b[i], moe_w_router, moe_router_bias, tm=LN_TM)
        x, xb = moe_ffn_ln(x, xpk, e_idx, gate, rank, counts, moe_w_in, moe_w_out, i,
                           ln_ffn_g[i], ln_ffn_b[i], with_bf16=(i + 1 < DEPTH), tm=COMBINE_TM)
    return x.reshape(bsz, seq, d)
```

```python
import functools

import jax
import jax.numpy as jnp
from jax import lax
from jax.experimental import pallas as pl
from jax.experimental.pallas import tpu as pltpu

F32 = jnp.float32
BF16 = jnp.bfloat16
I32 = jnp.int32
U32 = jnp.uint32

D_MODEL = 4096
DEPTH = 2
N_MIXERS = 2
DEEPNORM_ALPHA = (2 * DEPTH) ** 0.25
LN_EPS = 1e-5
POOL_WINDOWS = (2, 4, 8, 16)
POOL_GROUPS = len(POOL_WINDOWS)
POOL_GC = D_MODEL // POOL_GROUPS
D_INNER = 2 * D_MODEL
SSD_HEAD_DIM = 64
SSD_HEADS = D_INNER // SSD_HEAD_DIM
SSD_GROUPS = 8
SSD_HPG = SSD_HEADS // SSD_GROUPS
SSD_STATE = 128
SSD_CONV = 4
SSD_CHUNK = 128
SSD_GW = D_INNER // SSD_GROUPS
SSD_CONV_DIM = D_INNER + 2 * SSD_GROUPS * SSD_STATE
SSD_IN_DIM = D_INNER + SSD_CONV_DIM + SSD_HEADS
RMS_EPS = 1e-5
N_EXPERTS = 32
N_EXPERT_GROUPS = 8
EXPERTS_PER_GROUP = N_EXPERTS // N_EXPERT_GROUPS
TOP_K = 2
D_FF = 768
BLOCK_ROWS = 128

LANES = 128
SUBLANES = 8
VMEM_BYTES_V7X = 64 * 1024 * 1024
VMEM_CAP = VMEM_BYTES_V7X * 7 // 8

POOL_HALO = 16
CONV_HALO = SUBLANES
EXPERT_CHUNK = 2 * BLOCK_ROWS
WEIGHT_DMA_PARTS = 8
ROW_DMA_PRIORITY = 0
WEIGHT_DMA_PRIORITY = 1


def _nbytes(shape, dtype):
    n = 1
    for s in shape:
        n *= s
    return n * jnp.dtype(dtype).itemsize


def _vmem_limit(pipelined, resident):
    est = 2 * sum(pipelined) + sum(resident)
    return int(min(VMEM_CAP, est + max(est // 4, 8 * 1024 * 1024)))


def _params(semantics, pipelined, resident=()):
    return pltpu.CompilerParams(dimension_semantics=semantics,
                                vmem_limit_bytes=_vmem_limit(pipelined, resident))


def _mm_kernel(a_ref, w_ref, o_ref, wbf_ref):
    @pl.when(pl.program_id(1) == 0)
    def _():
        wbf_ref[...] = w_ref[...].astype(BF16)

    o_ref[...] = jnp.dot(a_ref[...], wbf_ref[...], preferred_element_type=F32).astype(o_ref.dtype)


def _mm_scale_kernel(a_ref, w_ref, s_ref, o_ref, wbf_ref):
    @pl.when(pl.program_id(1) == 0)
    def _():
        wbf_ref[...] = w_ref[...].astype(BF16)

    acc = jnp.dot(a_ref[...], wbf_ref[...], preferred_element_type=F32)
    o_ref[...] = (acc * s_ref[...]).astype(o_ref.dtype)


def matmul_cols(a, w, col_off, n_cols, *, tm, tn, out_dtype, name):
    m_rows, k = a.shape
    assert w.shape[0] == k and m_rows % tm == 0 and n_cols % tn == 0 and col_off % tn == 0
    off = col_off // tn
    return pl.pallas_call(
        _mm_kernel,
        grid=(n_cols // tn, m_rows // tm),
        in_specs=[pl.BlockSpec((tm, k), lambda n, m: (m, 0)),
                  pl.BlockSpec((k, tn), lambda n, m: (0, n + off))],
        out_specs=pl.BlockSpec((tm, tn), lambda n, m: (m, n)),
        out_shape=jax.ShapeDtypeStruct((m_rows, n_cols), out_dtype),
        scratch_shapes=[pltpu.VMEM((k, tn), BF16)],
        compiler_params=_params(
            ("arbitrary", "arbitrary"),
            [_nbytes((tm, k), BF16), _nbytes((k, tn), F32), _nbytes((tm, tn), out_dtype)],
            [_nbytes((k, tn), BF16), _nbytes((tm, tn), F32)]),
        name=name,
    )(a, w)


def _mm_silu_kernel(a_ref, w_ref, o_ref, wbf_ref, *, parts):
    @pl.when(pl.program_id(1) == 0)
    def _():
        wbf_ref[...] = w_ref[...].astype(BF16)

    rows = a_ref.shape[0] // parts
    for p in range(parts):
        part = slice(p * rows, (p + 1) * rows)
        o_ref[part, :] = _silu(jnp.dot(a_ref[part, :], wbf_ref[...], preferred_element_type=F32))


def _mm_conv_silu_kernel(a_ref, w_ref, cw_ref, cb_ref, o_ref, wbf_ref, raw_ref, *, parts):
    tm, tn = o_ref.shape

    @pl.when(pl.program_id(1) == 0)
    def _():
        wbf_ref[...] = w_ref[...].astype(BF16)
        raw_ref[0:CONV_HALO, :] = jnp.zeros((CONV_HALO, tn), F32)

    rows = tm // parts
    for p in range(parts):
        lo = CONV_HALO + p * rows
        raw_ref[lo:lo + rows, :] = jnp.dot(a_ref[p * rows:(p + 1) * rows, :], wbf_ref[...],
                                           preferred_element_type=F32)
        for s in range(tn // LANES):
            cols = slice(s * LANES, (s + 1) * LANES)
            acc = cb_ref[:, cols] + cw_ref[SSD_CONV - 1:SSD_CONV, cols] * raw_ref[lo:lo + rows, cols]
            for k in range(SSD_CONV - 1):
                back = SSD_CONV - 1 - k
                acc = acc + cw_ref[k:k + 1, cols] * raw_ref[lo - back:lo - back + rows, cols]
            o_ref[p * rows:(p + 1) * rows, cols] = _silu(acc)
    raw_ref[0:CONV_HALO, :] = raw_ref[tm:tm + CONV_HALO, :]


def matmul_cols_act(a, w, col_off, n_cols, *, tm, tn, parts, conv=None, name):
    m_rows, k = a.shape
    assert w.shape[0] == k and m_rows % tm == 0 and n_cols % tn == 0 and col_off % tn == 0
    assert tm % parts == 0 and (tm // parts) % CONV_HALO == 0
    off = col_off // tn
    in_specs = [pl.BlockSpec((tm, k), lambda n, m: (m, 0)),
                pl.BlockSpec((k, tn), lambda n, m: (0, n + off))]
    scratch = [pltpu.VMEM((k, tn), BF16)]
    operands = [a, w]
    if conv is None:
        body = functools.partial(_mm_silu_kernel, parts=parts)
    else:
        body = functools.partial(_mm_conv_silu_kernel, parts=parts)
        in_specs += [pl.BlockSpec((SSD_CONV, tn), lambda n, m: (0, n)),
                     pl.BlockSpec((1, tn), lambda n, m: (0, n))]
        scratch.append(pltpu.VMEM((CONV_HALO + tm, tn), F32))
        operands += list(conv)
    return pl.pallas_call(
        body,
        grid=(n_cols // tn, m_rows // tm),
        in_specs=in_specs,
        out_specs=pl.BlockSpec((tm, tn), lambda n, m: (m, n)),
        out_shape=jax.ShapeDtypeStruct((m_rows, n_cols), F32),
        scratch_shapes=scratch,
        compiler_params=_params(
            ("arbitrary", "arbitrary"),
            [_nbytes((tm, k), BF16), _nbytes((k, tn), F32), _nbytes((tm, tn), F32)],
            [_nbytes((k, tn), BF16), 2 * _nbytes((tm, tn), F32)]),
        name=name,
    )(*operands)


def _mm_bf16w_kernel(a_ref, w_ref, o_ref):
    o_ref[...] = jnp.dot(a_ref[...], w_ref[...], preferred_element_type=F32).astype(o_ref.dtype)


def matmul_bf16w(a, w, *, tm, tn, out_dtype, name):
    m_rows, k = a.shape
    n_cols = w.shape[1]
    assert w.shape[0] == k and m_rows % tm == 0 and n_cols % tn == 0
    return pl.pallas_call(
        _mm_bf16w_kernel,
        grid=(n_cols // tn, m_rows // tm),
        in_specs=[pl.BlockSpec((tm, k), lambda n, m: (m, 0)),
                  pl.BlockSpec((k, tn), lambda n, m: (0, n))],
        out_specs=pl.BlockSpec((tm, tn), lambda n, m: (m, n)),
        out_shape=jax.ShapeDtypeStruct((m_rows, n_cols), out_dtype),
        compiler_params=_params(
            ("arbitrary", "arbitrary"),
            [_nbytes((tm, k), BF16), _nbytes((k, tn), BF16), _nbytes((tm, tn), out_dtype)],
            [_nbytes((tm, tn), F32)]),
        name=name,
    )(a, w)


def pool_group_matmul(p, w_group, scale, *, tm):
    m_rows = p.shape[0]
    gc = POOL_GC
    assert m_rows % tm == 0
    return pl.pallas_call(
        _mm_scale_kernel,
        grid=(POOL_GROUPS, m_rows // tm),
        in_specs=[pl.BlockSpec((tm, gc), lambda g, m: (m, g)),
                  pl.BlockSpec((None, gc, gc), lambda g, m: (g, 0, 0)),
                  pl.BlockSpec((1, gc), lambda g, m: (0, g))],
        out_specs=pl.BlockSpec((tm, gc), lambda g, m: (m, g)),
        out_shape=jax.ShapeDtypeStruct((m_rows, POOL_GROUPS * gc), BF16),
        scratch_shapes=[pltpu.VMEM((gc, gc), BF16)],
        compiler_params=_params(
            ("arbitrary", "arbitrary"),
            [_nbytes((tm, gc), BF16), _nbytes((gc, gc), F32), _nbytes((tm, gc), BF16)],
            [_nbytes((gc, gc), BF16), _nbytes((tm, gc), F32)]),
        name="pool_group_matmul",
    )(p, w_group, scale.reshape(1, -1))


def _pool_in_kernel(x_ref, w_ref, p_ref, wbf_ref, ubuf_ref, *, tm, tn):
    n = pl.program_id(0)
    m = pl.program_id(1)

    @pl.when(m == 0)
    def _():
        wbf_ref[...] = w_ref[...].astype(BF16)
        ubuf_ref[0:POOL_HALO, :] = jnp.zeros((POOL_HALO, tn), F32)

    u = jnp.dot(x_ref[...], wbf_ref[...], preferred_element_type=F32)
    ubuf_ref[POOL_HALO:POOL_HALO + tm, :] = u
    pos = (m * tm + 1 + lax.broadcasted_iota(I32, (tm, tn), 0)).astype(F32)
    group = (n * tn) // POOL_GC
    for gi, window in enumerate(POOL_WINDOWS):
        @pl.when(group == gi)
        def _(window=window):
            s = ubuf_ref[...]
            span = 1
            while span < window:
                s = s + pltpu.roll(s, span, axis=0)
                span *= 2
            acc = s[POOL_HALO:POOL_HALO + tm, :]
            mean = acc / jnp.minimum(pos, float(window))
            p_ref[...] = (mean - u).astype(p_ref.dtype)

    ubuf_ref[0:POOL_HALO, :] = ubuf_ref[tm:tm + POOL_HALO, :]


def pool_in(xb, w_in, *, tm, tn):
    m_rows, k = xb.shape
    n_cols = w_in.shape[1]
    assert m_rows % tm == 0 and n_cols % tn == 0 and POOL_GC % tn == 0 and tm >= POOL_HALO
    return pl.pallas_call(
        functools.partial(_pool_in_kernel, tm=tm, tn=tn),
        grid=(n_cols // tn, m_rows // tm),
        in_specs=[pl.BlockSpec((tm, k), lambda n, m: (m, 0)),
                  pl.BlockSpec((k, tn), lambda n, m: (0, n))],
        out_specs=pl.BlockSpec((tm, tn), lambda n, m: (m, n)),
        out_shape=jax.ShapeDtypeStruct((m_rows, n_cols), BF16),
        scratch_shapes=[pltpu.VMEM((k, tn), BF16), pltpu.VMEM((tm + POOL_HALO, tn), F32)],
        compiler_params=_params(
            ("arbitrary", "arbitrary"),
            [_nbytes((tm, k), BF16), _nbytes((k, tn), F32), _nbytes((tm, tn), BF16)],
            [_nbytes((k, tn), BF16), 4 * _nbytes((tm + POOL_HALO, tn), F32)]),
        name="pool_in",
    )(xb, w_in)


def _layer_norm_rows(v, g, b):
    mu = jnp.mean(v, axis=-1, keepdims=True)
    vc = v - mu
    var = jnp.mean(vc * vc, axis=-1, keepdims=True)
    return vc * lax.rsqrt(var + LN_EPS) * g + b


def _ln_router_kernel(x_ref, h_ref, g_ref, b_ref, wr_ref, rb_ref,
                      xo_ref, xpk_ref, e_ref, gate_ref, rank_ref, cnt_ref, base_ref, *, tm):
    i = pl.program_id(0)

    @pl.when(i == 0)
    def _():
        base_ref[...] = jnp.zeros_like(base_ref)

    y = _layer_norm_rows(DEEPNORM_ALPHA * x_ref[...] + h_ref[...], g_ref[...], b_ref[...])
    xo_ref[...] = y
    yb = y.astype(BF16)
    half = y.shape[1] // 2
    lo_bits = pltpu.bitcast(yb[:, :half].astype(F32), U32) >> 16
    hi_bits = pltpu.bitcast(yb[:, half:].astype(F32), U32)
    xpk_ref[...] = hi_bits | lo_bits

    ng, epg = N_EXPERT_GROUPS, EXPERTS_PER_GROUP
    logits = lax.dot_general(wr_ref[...], yb, (((1,), (1,)), ((), ())),
                             preferred_element_type=F32)
    scores = jax.nn.sigmoid(logits)
    sel = scores + rb_ref[...]
    sel_j = [sel[ng * j:ng * (j + 1), :] for j in range(epg)]
    sc_j = [scores[ng * j:ng * (j + 1), :] for j in range(epg)]
    gscore = None
    for j1 in range(epg):
        for j2 in range(j1 + 1, epg):
            s = sel_j[j1] + sel_j[j2]
            gscore = s if gscore is None else jnp.maximum(gscore, s)
    gmax = jnp.max(gscore, axis=0, keepdims=True)
    giota = lax.broadcasted_iota(I32, (ng, tm), 0)
    g_idx = jnp.min(jnp.where(gscore == gmax, giota, ng), axis=0, keepdims=True)
    in_g = giota == g_idx
    v = [jnp.sum(jnp.where(in_g, sel_j[j], 0.0), axis=0, keepdims=True) for j in range(epg)]
    s = [jnp.sum(jnp.where(in_g, sc_j[j], 0.0), axis=0, keepdims=True) for j in range(epg)]
    order = []
    for j in range(epg):
        r = jnp.zeros((1, tm), I32)
        for k in range(epg):
            if k == j:
                continue
            beats = (v[k] >= v[j]) if k < j else (v[k] > v[j])
            r = r + beats.astype(I32)
        order.append(r)
    eiota = lax.broadcasted_iota(I32, (N_EXPERTS, tm), 0)
    onehot = jnp.zeros((N_EXPERTS, tm), F32)
    loc, raw, rows = [], [], []
    for slot in range(TOP_K):
        lj = jnp.zeros((1, tm), I32)
        gs = jnp.zeros((1, tm), F32)
        for j in range(epg):
            hit = order[j] == slot
            lj = lj + jnp.where(hit, j, 0)
            gs = gs + jnp.where(hit, s[j], 0.0)
        row = lj * ng + g_idx
        onehot = onehot + (eiota == row).astype(F32)
        loc.append(lj)
        raw.append(gs)
        rows.append(row)
    denom = raw[0] + raw[1]
    srow = lax.broadcasted_iota(I32, (tm, tm), 0)
    scol = lax.broadcasted_iota(I32, (tm, tm), 1)
    before = (srow < scol).astype(BF16)
    prefix = jnp.dot(onehot.astype(BF16), before, preferred_element_type=F32) + base_ref[...]
    for slot in range(TOP_K):
        e_ref[slot:slot + 1, :] = g_idx * epg + loc[slot]
        gate_ref[slot:slot + 1, :] = raw[slot] / denom
        rk = jnp.sum(jnp.where(eiota == rows[slot], prefix, 0.0), axis=0, keepdims=True)
        rank_ref[slot:slot + 1, :] = rk.astype(I32)
    base_ref[...] = base_ref[...] + jnp.sum(onehot, axis=1, keepdims=True)
    cnt_ref[...] = jnp.broadcast_to(base_ref[...], cnt_ref.shape)


def ln_router(x, h, g, b, w_router, router_bias, *, tm):
    t, d = x.shape
    assert t % tm == 0
    ng, epg = N_EXPERT_GROUPS, EXPERTS_PER_GROUP
    wr = w_router.T.reshape(ng, epg, d).transpose(1, 0, 2).reshape(N_EXPERTS, d).astype(BF16)
    rb = router_bias.astype(F32).reshape(ng, epg).T.reshape(N_EXPERTS, 1)
    row = pl.BlockSpec((tm, d), lambda i: (i, 0))
    vec = pl.BlockSpec((1, d), lambda i: (0, 0))
    tok = pl.BlockSpec((TOP_K, tm), lambda i: (0, i))
    outs = pl.pallas_call(
        functools.partial(_ln_router_kernel, tm=tm),
        grid=(t // tm,),
        in_specs=[row, row, vec, vec,
                  pl.BlockSpec((N_EXPERTS, d), lambda i: (0, 0)),
                  pl.BlockSpec((N_EXPERTS, 1), lambda i: (0, 0))],
        out_specs=[row, pl.BlockSpec((tm, d // 2), lambda i: (i, 0)), tok, tok, tok,
                   pl.BlockSpec((N_EXPERTS, LANES), lambda i: (0, 0))],
        out_shape=[jax.ShapeDtypeStruct((t, d), F32), jax.ShapeDtypeStruct((t, d // 2), U32),
                   jax.ShapeDtypeStruct((TOP_K, t), I32), jax.ShapeDtypeStruct((TOP_K, t), F32),
                   jax.ShapeDtypeStruct((TOP_K, t), I32),
                   jax.ShapeDtypeStruct((N_EXPERTS, LANES), F32)],
        scratch_shapes=[pltpu.VMEM((N_EXPERTS, 1), F32)],
        compiler_params=_params(
            ("arbitrary",),
            [3 * _nbytes((tm, d), F32), _nbytes((tm, d), BF16), _nbytes((N_EXPERTS, d), BF16)],
            [4 * _nbytes((tm, d), F32)]),
        name="ln_router",
    )(x, h, g.reshape(1, d), b.reshape(1, d), wr, rb)
    x_new, xpk, e_idx, gate, rank, cnt = outs
    counts = cnt[:, 0].astype(I32).reshape(epg, ng).T.reshape(N_EXPERTS)
    return x_new, xpk, e_idx, gate, rank, counts


def _weight_copies(w_hbm, layer, ex, stage, wsem):
    rows = stage.shape[0] // WEIGHT_DMA_PARTS
    assert rows * WEIGHT_DMA_PARTS == stage.shape[0] and rows % SUBLANES == 0
    return [pltpu.make_async_copy(w_hbm.at[layer, ex, pl.ds(p * rows, rows)],
                                  stage.at[pl.ds(p * rows, rows)], wsem.at[p])
            for p in range(WEIGHT_DMA_PARTS)]


def _start_all(copies, priority=None):
    for p, cp in enumerate(copies):
        cp.start(priority=p % 2 if priority is None else priority)


def _wait_all(copies):
    for cp in copies:
        cp.wait()


def _wait_cast_refill(copies, refills, refill, stage, wbf):
    rows = stage.shape[0] // len(copies)
    for p, cp in enumerate(copies):
        cp.wait()
        wbf[p * rows:(p + 1) * rows, :] = stage[p * rows:(p + 1) * rows, :].astype(BF16)

        @pl.when(refill)
        def _(p=p):
            refills[p].start(priority=WEIGHT_DMA_PRIORITY)


def _expert_in_kernel(src_ref, crow_ref, cbase_ref, nch_ref, total_ref, xpk_hbm, w_hbm, act_hbm,
                      stage, wbf, ibuf, obuf, wsem, isem, osem, *, layer):
    e = pl.program_id(0)
    last = pl.num_programs(0) - 1
    total = total_ref[0]
    half = xpk_hbm.shape[1]

    def w_copies(ex):
        return _weight_copies(w_hbm, layer, ex, stage, wsem)

    def start_gather(g):
        row0 = crow_ref[g]
        for r in range(EXPERT_CHUNK):
            tok = src_ref[row0 + r]
            pltpu.make_async_copy(xpk_hbm.at[pl.ds(tok, 1)], ibuf.at[g % 2, pl.ds(r, 1)],
                                  isem.at[g % 2]).start(priority=ROW_DMA_PRIORITY)

    def wait_gather(g):
        pltpu.make_async_copy(xpk_hbm.at[pl.ds(0, EXPERT_CHUNK)], ibuf.at[g % 2], isem.at[g % 2]).wait()

    def out_copy(g):
        rows = pl.ds(pl.multiple_of(crow_ref[g], BLOCK_ROWS), EXPERT_CHUNK)
        return pltpu.make_async_copy(obuf.at[g % 2], act_hbm.at[rows], osem.at[g % 2])

    @pl.when(e == 0)
    def _():
        _start_all(w_copies(e), priority=WEIGHT_DMA_PRIORITY)

        @pl.when(total > 0)
        def _():
            start_gather(0)

    _wait_cast_refill(w_copies(e), w_copies(e + 1), e < last, stage, wbf)

    base = cbase_ref[e]

    def body(i, carry):
        g = base + i

        @pl.when(g + 1 < total)
        def _():
            start_gather(g + 1)

        wait_gather(g)
        packed = ibuf[g % 2]
        x_lo = pltpu.bitcast(packed << 16, F32).astype(BF16)
        x_hi = pltpu.bitcast(packed & jnp.uint32(0xFFFF0000), F32).astype(BF16)
        h = (jnp.dot(x_lo, wbf[0:half, :], preferred_element_type=F32)
             + jnp.dot(x_hi, wbf[half:2 * half, :], preferred_element_type=F32))
        h1 = h[:, :D_FF]
        obuf[g % 2] = (h1 * jax.nn.sigmoid(h1) * h[:, D_FF:]).astype(obuf.dtype)

        @pl.when(g >= 1)
        def _():
            out_copy(g - 1).wait()

        out_copy(g).start()
        return carry

    lax.fori_loop(0, nch_ref[e], body, 0)

    @pl.when(e == last)
    def _():
        @pl.when(total > 0)
        def _():
            out_copy(total - 1).wait()

        end = jnp.where(total > 0, crow_ref[jnp.maximum(total - 1, 0)] + EXPERT_CHUNK, 0)
        obuf[0] = jnp.zeros(obuf.shape[1:], obuf.dtype)

        def tail_copy(blk):
            rows = pl.ds(pl.multiple_of(end + blk * BLOCK_ROWS, BLOCK_ROWS), BLOCK_ROWS)
            return pltpu.make_async_copy(obuf.at[0, 0:BLOCK_ROWS], act_hbm.at[rows], osem.at[0])

        n_tail = (act_hbm.shape[0] - end) // BLOCK_ROWS
        lax.fori_loop(0, n_tail, lambda blk, c: (tail_copy(blk).start(), c)[1], 0)
        lax.fori_loop(0, n_tail, lambda blk, c: (tail_copy(blk).wait(), c)[1], 0)


def expert_in(xpk, src, chunk_row, chunk_base, n_chunks, total, moe_w_in, layer):
    rows = src.shape[0]
    half = xpk.shape[1]
    wshape = moe_w_in.shape[-2:]
    resident = [_nbytes(wshape, F32), _nbytes(wshape, BF16),
                2 * _nbytes((EXPERT_CHUNK, half), U32), 2 * _nbytes((EXPERT_CHUNK, D_FF), BF16),
                _nbytes((EXPERT_CHUNK, 2 * half), BF16), 2 * _nbytes((EXPERT_CHUNK, wshape[1]), F32)]
    return pl.pallas_call(
        functools.partial(_expert_in_kernel, layer=layer),
        grid_spec=pltpu.PrefetchScalarGridSpec(
            num_scalar_prefetch=5,
            grid=(N_EXPERTS,),
            in_specs=[pl.BlockSpec(memory_space=pl.ANY), pl.BlockSpec(memory_space=pl.ANY)],
            out_specs=pl.BlockSpec(memory_space=pl.ANY),
            scratch_shapes=[pltpu.VMEM(wshape, F32), pltpu.VMEM(wshape, BF16),
                            pltpu.VMEM((2, EXPERT_CHUNK, half), U32),
                            pltpu.VMEM((2, EXPERT_CHUNK, D_FF), BF16),
                            pltpu.SemaphoreType.DMA((WEIGHT_DMA_PARTS,)),
                            pltpu.SemaphoreType.DMA((2,)),
                            pltpu.SemaphoreType.DMA((2,))]),
        out_shape=jax.ShapeDtypeStruct((rows, D_FF), BF16),
        compiler_params=pltpu.CompilerParams(
            dimension_semantics=("arbitrary",), has_side_effects=True,
            vmem_limit_bytes=int(min(VMEM_CAP, sum(resident) + 4 * 1024 * 1024))),
        name="expert_in",
    )(src, chunk_row, chunk_base, n_chunks, total, xpk, moe_w_in)


def _expert_kernel(pstart_ref, padded_ref, in_hbm, w_hbm, out_hbm,
                   stage, wbf, ibuf, obuf, wsem, isem, osem, *, layer):
    e = pl.program_id(0)

    def w_copies(ex):
        return _weight_copies(w_hbm, layer, ex, stage, wsem)

    @pl.when(e == 0)
    def _():
        _start_all(w_copies(e), priority=WEIGHT_DMA_PRIORITY)

    _wait_cast_refill(w_copies(e), w_copies(e + 1), e + 1 < pl.num_programs(0), stage, wbf)

    start = pstart_ref[e]
    n_chunks = (padded_ref[e] + EXPERT_CHUNK - 1) // EXPERT_CHUNK

    def chunk_rows(i):
        return pl.ds(pl.multiple_of(start + i * EXPERT_CHUNK, BLOCK_ROWS), EXPERT_CHUNK)

    def in_copy(i):
        return pltpu.make_async_copy(in_hbm.at[chunk_rows(i)], ibuf.at[i % 2], isem.at[i % 2])

    def out_copies(i):
        copies = []
        for p in range(EXPERT_CHUNK // BLOCK_ROWS):
            rows = pl.ds(pl.multiple_of(start + i * EXPERT_CHUNK + p * BLOCK_ROWS, BLOCK_ROWS),
                         BLOCK_ROWS)
            copies.append(pltpu.make_async_copy(
                obuf.at[i % 2, pl.ds(p * BLOCK_ROWS, BLOCK_ROWS)], out_hbm.at[rows], osem.at[i % 2]))
        return copies

    @pl.when(n_chunks > 0)
    def _():
        in_copy(0).start()

    def body(i, carry):
        @pl.when(i + 1 < n_chunks)
        def _():
            in_copy(i + 1).start()

        in_copy(i).wait()
        obuf[i % 2] = jnp.dot(ibuf[i % 2], wbf[...], preferred_element_type=F32).astype(obuf.dtype)
        _start_all(out_copies(i))

        @pl.when(i >= 1)
        def _():
            _wait_all(out_copies(i - 1))
        return carry

    lax.fori_loop(0, n_chunks, body, 0)

    @pl.when(n_chunks > 0)
    def _():
        _wait_all(out_copies(n_chunks - 1))

    @pl.when(e + 1 == pl.num_programs(0))
    def _():
        end = start + n_chunks * EXPERT_CHUNK
        obuf[0] = jnp.zeros(obuf.shape[1:], obuf.dtype)

        def tail_copy(blk):
            rows = pl.ds(pl.multiple_of(end + blk * BLOCK_ROWS, BLOCK_ROWS), BLOCK_ROWS)
            return pltpu.make_async_copy(obuf.at[0, 0:BLOCK_ROWS], out_hbm.at[rows], osem.at[0])

        n_tail = (out_hbm.shape[0] - end) // BLOCK_ROWS
        lax.fori_loop(0, n_tail, lambda blk, c: (tail_copy(blk).start(), c)[1], 0)
        lax.fori_loop(0, n_tail, lambda blk, c: (tail_copy(blk).wait(), c)[1], 0)


def _expert_call(rows_in, w, pad_start, padded, layer, *, out_dtype, name):
    rows, k = rows_in.shape
    n_out = w.shape[-1]
    wshape = w.shape[-2:]
    resident = [_nbytes(wshape, F32), _nbytes(wshape, BF16),
                2 * _nbytes((EXPERT_CHUNK, k), rows_in.dtype),
                2 * _nbytes((EXPERT_CHUNK, n_out), out_dtype),
                2 * _nbytes((EXPERT_CHUNK, wshape[1]), F32)]
    return pl.pallas_call(
        functools.partial(_expert_kernel, layer=layer),
        grid_spec=pltpu.PrefetchScalarGridSpec(
            num_scalar_prefetch=2,
            grid=(N_EXPERTS,),
            in_specs=[pl.BlockSpec(memory_space=pl.ANY), pl.BlockSpec(memory_space=pl.ANY)],
            out_specs=pl.BlockSpec(memory_space=pl.ANY),
            scratch_shapes=[pltpu.VMEM(wshape, F32), pltpu.VMEM(wshape, BF16),
                            pltpu.VMEM((2, EXPERT_CHUNK, k), rows_in.dtype),
                            pltpu.VMEM((2, EXPERT_CHUNK, n_out), out_dtype),
                            pltpu.SemaphoreType.DMA((WEIGHT_DMA_PARTS,)),
                            pltpu.SemaphoreType.DMA((2,)),
                            pltpu.SemaphoreType.DMA((2,))]),
        out_shape=jax.ShapeDtypeStruct((rows, n_out), out_dtype),
        compiler_params=pltpu.CompilerParams(
            dimension_semantics=("arbitrary",), has_side_effects=True,
            vmem_limit_bytes=int(min(VMEM_CAP, sum(resident) + 4 * 1024 * 1024))),
        name=name,
    )(pad_start, padded, rows_in, w)


def expert_out(act, pad_start, padded, moe_w_out, layer):
    return _expert_call(act, moe_w_out, pad_start, padded, layer, out_dtype=F32,
                        name="expert_out")


def _combine_ln_kernel(dest_ref, x_ref, gate_ref, g_ref, b_ref, y_hbm, *rest, tm, t, with_bf16):
    if with_bf16:
        xo_ref, xb_ref, ybuf, sem = rest
    else:
        xo_ref, ybuf, sem = rest
        xb_ref = None
    i = pl.program_id(0)
    n = pl.num_programs(0)

    def start_rows(step):
        buf = step % 2

        for r in range(tm):
            for k in range(TOP_K):
                d = dest_ref[k * t + step * tm + r]
                pltpu.make_async_copy(y_hbm.at[pl.ds(d, 1)], ybuf.at[buf, k, pl.ds(r, 1)],
                                      sem.at[buf]).start(priority=(r + k) % 2)

    def wait_rows(step):
        buf = step % 2
        for k in range(TOP_K):
            pltpu.make_async_copy(y_hbm.at[pl.ds(0, tm)], ybuf.at[buf, k], sem.at[buf]).wait()

    @pl.when(i == 0)
    def _():
        start_rows(i)

    @pl.when(i + 1 < n)
    def _():
        start_rows(i + 1)

    wait_rows(i)
    buf = i % 2
    gate = gate_ref[...]
    h = gate[:, 0:1] * ybuf[buf, 0] + gate[:, 1:2] * ybuf[buf, 1]
    y = _layer_norm_rows(DEEPNORM_ALPHA * x_ref[...] + h, g_ref[...], b_ref[...])
    xo_ref[...] = y
    if with_bf16:
        xb_ref[...] = y.astype(BF16)


def combine_ln(x, y_disp, dest_flat, gate_t, g, b, *, tm, with_bf16):
    t, d = x.shape
    assert t % tm == 0
    row = pl.BlockSpec((tm, d), lambda i, dest: (i, 0))
    vec = pl.BlockSpec((1, d), lambda i, dest: (0, 0))
    out_specs = [row, row] if with_bf16 else [row]
    out_shape = [jax.ShapeDtypeStruct((t, d), F32)]
    if with_bf16:
        out_shape.append(jax.ShapeDtypeStruct((t, d), BF16))
    outs = pl.pallas_call(
        functools.partial(_combine_ln_kernel, tm=tm, t=t, with_bf16=with_bf16),
        grid_spec=pltpu.PrefetchScalarGridSpec(
            num_scalar_prefetch=1,
            grid=(t // tm,),
            in_specs=[row, pl.BlockSpec((tm, TOP_K), lambda i, dest: (i, 0)), vec, vec,
                      pl.BlockSpec(memory_space=pl.ANY)],
            out_specs=out_specs,
            scratch_shapes=[pltpu.VMEM((2, TOP_K, tm, d), F32), pltpu.SemaphoreType.DMA((2,))]),
        out_shape=out_shape,
        compiler_params=_params(
            ("arbitrary",),
            [2 * _nbytes((tm, d), F32), _nbytes((tm, d), BF16)],
            [_nbytes((2, TOP_K, tm, d), F32), 4 * _nbytes((tm, d), F32)]),
        name="moe_combine_ln",
    )(dest_flat, x, gate_t, g.reshape(1, d), b.reshape(1, d), y_disp)
    return (outs[0], outs[1]) if with_bf16 else (outs[0], None)


def moe_ffn_ln(x, xpk, e_idx, gate, rank, counts, moe_w_in, moe_w_out, layer, g, b, *,
               with_bf16, tm):
    t, d = x.shape
    tk = t * TOP_K
    padded = ((counts + BLOCK_ROWS - 1) // BLOCK_ROWS * BLOCK_ROWS).astype(I32)
    pad_end = jnp.cumsum(padded).astype(I32)
    pad_start = pad_end - padded
    n_blocks = (tk + N_EXPERTS * (BLOCK_ROWS - 1) + BLOCK_ROWS - 1) // BLOCK_ROWS + 1
    e_flat = e_idx.reshape(tk)
    dest_flat = pad_start[e_flat] + rank.reshape(tk)
    tok_flat = jnp.arange(tk, dtype=I32) % t
    src = jnp.zeros((n_blocks * BLOCK_ROWS,), I32).at[dest_flat].set(tok_flat)
    n_chunks = (padded + EXPERT_CHUNK - 1) // EXPERT_CHUNK
    chunk_end = jnp.cumsum(n_chunks).astype(I32)
    chunk_base = chunk_end - n_chunks
    max_chunks = n_blocks // (EXPERT_CHUNK // BLOCK_ROWS) + N_EXPERTS
    gidx = jnp.arange(max_chunks, dtype=I32)
    owner = jnp.minimum(jnp.sum(gidx[:, None] >= chunk_end[None, :], axis=1), N_EXPERTS - 1)
    chunk_row = (pad_start[owner] + (gidx - chunk_base[owner]) * EXPERT_CHUNK).astype(I32)
    act = expert_in(xpk, src, chunk_row, chunk_base, n_chunks.astype(I32), chunk_end[-1:],
                    moe_w_in, layer)
    y_disp = expert_out(act, pad_start, padded, moe_w_out, layer)
    return combine_ln(x, y_disp, dest_flat, gate.T, g, b, tm=tm, with_bf16=with_bf16)


def _softplus(x):
    return jnp.maximum(x, 0.0) + jnp.log1p(jnp.exp(-jnp.abs(x)))


def _silu(x):
    return x * jax.nn.sigmoid(x)


def _dot_01_f32(m01, v, *, ones_left):
    hi = v.astype(BF16)
    r = v - hi.astype(F32)
    mid = r.astype(BF16)
    lo = (r - mid.astype(F32)).astype(BF16)

    def dot(p):
        lhs, rhs = (m01, p) if ones_left else (p, m01)
        return jnp.dot(lhs, rhs, preferred_element_type=F32)
    return dot(hi) + dot(mid) + dot(lo)


def _ssd_kernel(xs_ref, b_ref, c_ref, gz_ref, dt_ref, dtT_ref,
                dtb_ref, dtbT_ref, alog_ref, alogT_ref, dsk_ref, nw_ref,
                y_ref, state_ref):
    q = SSD_CHUNK

    @pl.when(pl.program_id(1) == 0)
    def _():
        state_ref[...] = jnp.zeros_like(state_ref)

    xs = xs_ref[...]
    bm = b_ref[...]
    cm = c_ref[...]

    dt = _softplus(dt_ref[...] + dtb_ref[...])
    dt_t = _softplus(dtT_ref[...] + dtbT_ref[...])
    a = -jnp.exp(alog_ref[...])
    a_t = -jnp.exp(alogT_ref[...])
    row = lax.broadcasted_iota(I32, (q, q), 0)
    col = lax.broadcasted_iota(I32, (q, q), 1)
    causal = row >= col
    a_cum = _dot_01_f32(jnp.where(causal, 1.0, 0.0).astype(BF16), dt * a, ones_left=True)
    a_cum_t = _dot_01_f32(jnp.where(row <= col, 1.0, 0.0).astype(BF16), dt_t * a_t,
                          ones_left=False)
    a_last = a_cum[q - 1:q, :]
    hrow = lax.broadcasted_iota(I32, (SSD_HPG, SSD_GW), 0)
    hcol = lax.broadcasted_iota(I32, (SSD_HPG, SSD_GW), 1)
    expand = jnp.where(hcol // SSD_HEAD_DIM == hrow, 1.0, 0.0).astype(BF16)
    decay_in = _dot_01_f32(expand, jnp.exp(a_cum), ones_left=False)
    w_state = _dot_01_f32(expand, jnp.exp(a_last - a_cum) * dt, ones_left=False)
    chunk_decay = decay_in[q - 1:q, :]

    cmb = cm.astype(BF16)
    bmb = bm.astype(BF16)
    cb = lax.dot_general(cmb, bmb, (((1,), (1,)), ((), ())), preferred_element_type=F32)
    prev = state_ref[...]
    y_off = jnp.dot(cmb, prev.astype(BF16), preferred_element_type=F32) * decay_in
    xw = (w_state * xs).astype(BF16)
    state_ref[...] = chunk_decay * prev + jnp.dot(bm.T.astype(BF16), xw, preferred_element_type=F32)

    lane = lax.broadcasted_iota(I32, (q, LANES), 1)
    heads_per_tile = LANES // SSD_HEAD_DIM
    ys = []
    for tile in range(SSD_GW // LANES):
        ms = []
        for hh in range(heads_per_tile):
            h = tile * heads_per_tile + hh
            seg = a_cum[:, h:h + 1] - a_cum_t[h:h + 1, :]
            decay = jnp.where(causal, jnp.exp(seg), 0.0)
            ms.append((cb * decay * dt_t[h:h + 1, :]).astype(BF16))
        x_tile = xs[:, tile * LANES:(tile + 1) * LANES]
        rhs = jnp.concatenate(
            [jnp.where(lane // SSD_HEAD_DIM == hh, x_tile, 0.0).astype(BF16)
             for hh in range(heads_per_tile)], axis=0)
        ys.append(jnp.dot(jnp.concatenate(ms, axis=1), rhs, preferred_element_type=F32))
    y = jnp.concatenate(ys, axis=1) + y_off + dsk_ref[...] * xs
    y = y * gz_ref[...]
    y = y * lax.rsqrt(jnp.mean(y * y, axis=-1, keepdims=True) + RMS_EPS) * nw_ref[...]
    y_ref[...] = y.astype(y_ref.dtype)


def ssd_core(gz, xbc, dt_raw, dt_bias, a_log, d_skip, norm_w):
    t = gz.shape[0]
    q, gw, ns, hpg, ng = SSD_CHUNK, SSD_GW, SSD_STATE, SSD_HPG, SSD_GROUPS
    assert t % q == 0
    b_blk = D_INNER // ns
    c_blk = b_blk + ng
    dt_g = dt_raw.reshape(t, ng, hpg).transpose(1, 0, 2)
    dt_gt = dt_raw.T.reshape(ng, hpg, t)
    dtb = dt_bias.astype(F32).reshape(ng, 1, hpg)
    dtb_t = dt_bias.astype(F32).reshape(ng, hpg, 1)
    alog = a_log.astype(F32).reshape(ng, 1, hpg)
    alog_t = a_log.astype(F32).reshape(ng, hpg, 1)
    dsk = jnp.repeat(d_skip.astype(F32), SSD_HEAD_DIM).reshape(1, D_INNER)
    nw = norm_w.astype(F32).reshape(1, D_INNER)

    def cspec(rows, width, blk):
        return pl.BlockSpec((rows, width), lambda g, c: (0, blk(g)))

    in_specs = [
        pl.BlockSpec((q, gw), lambda g, c: (c, g)),
        pl.BlockSpec((q, ns), lambda g, c: (c, b_blk + g)),
        pl.BlockSpec((q, ns), lambda g, c: (c, c_blk + g)),
        pl.BlockSpec((q, gw), lambda g, c: (c, g)),
        pl.BlockSpec((None, q, hpg), lambda g, c: (g, c, 0)),
        pl.BlockSpec((None, hpg, q), lambda g, c: (g, 0, c)),
        pl.BlockSpec((None, 1, hpg), lambda g, c: (g, 0, 0)),
        pl.BlockSpec((None, hpg, 1), lambda g, c: (g, 0, 0)),
        pl.BlockSpec((None, 1, hpg), lambda g, c: (g, 0, 0)),
        pl.BlockSpec((None, hpg, 1), lambda g, c: (g, 0, 0)),
        cspec(1, gw, lambda g: g), cspec(1, gw, lambda g: g),
    ]
    return pl.pallas_call(
        _ssd_kernel,
        grid=(ng, t // q),
        in_specs=in_specs,
        out_specs=pl.BlockSpec((q, gw), lambda g, c: (c, g)),
        out_shape=jax.ShapeDtypeStruct((t, D_INNER), BF16),
        scratch_shapes=[pltpu.VMEM((ns, gw), F32)],
        compiler_params=_params(
            ("arbitrary", "arbitrary"),
            [2 * _nbytes((q, gw), F32), 2 * _nbytes((q, ns), F32), _nbytes((q, gw), BF16)],
            [_nbytes((ns, gw), F32), 16 * _nbytes((q, gw), F32)]),
        name="ssd_core",
    )(xbc, xbc, xbc, gz, dt_g, dt_gt, dtb, dtb_t, alog, alog_t, dsk, nw)


MM_TM = 1024
MM_TN = 512
WIDE_K_TM = 512
EPILOGUE_PARTS = 8
LN_TM = 256
COMBINE_TM = 128


def pool_mixer(xb, w_in, w_group, scale, w_out):
    p = pool_in(xb, w_in, tm=MM_TM, tn=MM_TN)
    mixed = pool_group_matmul(p, w_group, scale, tm=MM_TM)
    return matmul_cols(mixed, w_out, 0, D_MODEL, tm=MM_TM, tn=MM_TN, out_dtype=F32, name="pool_out")


def ssd_mixer(xb, w_in, conv_w, conv_b, dt_bias, a_log, d_skip, norm_w, w_out):
    gz = matmul_cols_act(xb, w_in, 0, D_INNER, tm=MM_TM, tn=MM_TN, parts=EPILOGUE_PARTS,
                         name="ssd_in_z")
    conv = (conv_w.astype(F32), conv_b.astype(F32).reshape(1, SSD_CONV_DIM))
    xbc = matmul_cols_act(xb, w_in, D_INNER, SSD_CONV_DIM, tm=MM_TM, tn=MM_TN, parts=EPILOGUE_PARTS,
                          conv=conv, name="ssd_in_xbc")
    dt_raw = matmul_cols(xb, w_in, D_INNER + SSD_CONV_DIM, SSD_HEADS, tm=MM_TM, tn=SSD_HEADS,
                         out_dtype=F32, name="ssd_in_dt")
    y = ssd_core(gz, xbc, dt_raw, dt_bias, a_log, d_skip, norm_w)
    return matmul_bf16w(y, w_out.astype(BF16), tm=WIDE_K_TM, tn=MM_TN, out_dtype=F32, name="ssd_out")


def kernel(x, pool_w_in, pool_w_group, pool_scale, pool_w_out, ssd_w_in, ssd_conv_w, ssd_conv_b,
           ssd_dt_bias, ssd_a_log, ssd_d, ssd_norm_w, ssd_w_out, moe_w_router, moe_router_bias,
           moe_w_in, moe_w_out, ln_mix_g, ln_mix_b, ln_ffn_g, ln_ffn_b):
    bsz, seq, d = x.shape
    x = x.reshape(bsz * seq, d)
    assert bsz == 1
    xb = x.astype(BF16)
    for i in range(DEPTH):
        j = i // N_MIXERS
        if i % N_MIXERS == 0:
            h = pool_mixer(xb, pool_w_in[j], pool_w_group[j], pool_scale[j], pool_w_out[j])
        else:
            h = ssd_mixer(xb, ssd_w_in[j], ssd_conv_w[j], ssd_conv_b[j], ssd_dt_bias[j],
                          ssd_a_log[j], ssd_d[j], ssd_norm_w[j], ssd_w_out[j])
        x, xpk, e_idx, gate, rank, counts = ln_router(
            x, h, ln_mix_g[i], ln_mix_b[i], moe_w_router, moe_router_bias, tm=LN_TM)
        x, xb = moe_ffn_ln(x, xpk, e_idx, gate, rank, counts, moe_w_in, moe_w_out, i,
                           ln_ffn_g[i], ln_ffn_b[i], with_bf16=(i + 1 < DEPTH), tm=COMBINE_TM)
    return x.reshape(bsz, seq, d)
```

```python
import functools

import jax
import jax.numpy as jnp
from jax import lax
from jax.experimental import pallas as pl
from jax.experimental.pallas import tpu as pltpu

F32 = jnp.float32
BF16 = jnp.bfloat16
I32 = jnp.int32
U32 = jnp.uint32

D_MODEL = 4096
DEPTH = 2
N_MIXERS = 2
DEEPNORM_ALPHA = (2 * DEPTH) ** 0.25
LN_EPS = 1e-5
POOL_WINDOWS = (2, 4, 8, 16)
POOL_GROUPS = len(POOL_WINDOWS)
POOL_GC = D_MODEL // POOL_GROUPS
D_INNER = 2 * D_MODEL
SSD_HEAD_DIM = 64
SSD_HEADS = D_INNER // SSD_HEAD_DIM
SSD_GROUPS = 8
SSD_HPG = SSD_HEADS // SSD_GROUPS
SSD_STATE = 128
SSD_CONV = 4
SSD_CHUNK = 128
SSD_GW = D_INNER // SSD_GROUPS
SSD_CONV_DIM = D_INNER + 2 * SSD_GROUPS * SSD_STATE
SSD_IN_DIM = D_INNER + SSD_CONV_DIM + SSD_HEADS
RMS_EPS = 1e-5
N_EXPERTS = 32
N_EXPERT_GROUPS = 8
EXPERTS_PER_GROUP = N_EXPERTS // N_EXPERT_GROUPS
TOP_K = 2
D_FF = 768
BLOCK_ROWS = 128

LANES = 128
SUBLANES = 8
VMEM_BYTES_V7X = 64 * 1024 * 1024
VMEM_CAP = VMEM_BYTES_V7X * 7 // 8

POOL_HALO = 16
CONV_HALO = SUBLANES
EXPERT_CHUNK = 2 * BLOCK_ROWS
WEIGHT_DMA_PARTS = 8
ROW_DMA_PRIORITY = 0
WEIGHT_DMA_PRIORITY = 1
WEIGHT_PARTS_ON_ROW_QUEUE = 2


def _nbytes(shape, dtype):
    n = 1
    for s in shape:
        n *= s
    return n * jnp.dtype(dtype).itemsize


def _vmem_limit(pipelined, resident):
    est = 2 * sum(pipelined) + sum(resident)
    return int(min(VMEM_CAP, est + max(est // 4, 8 * 1024 * 1024)))


def _params(semantics, pipelined, resident=()):
    return pltpu.CompilerParams(dimension_semantics=semantics,
                                vmem_limit_bytes=_vmem_limit(pipelined, resident))


def _mm_kernel(a_ref, w_ref, o_ref, wbf_ref):
    @pl.when(pl.program_id(1) == 0)
    def _():
        wbf_ref[...] = w_ref[...].astype(BF16)

    o_ref[...] = jnp.dot(a_ref[...], wbf_ref[...], preferred_element_type=F32).astype(o_ref.dtype)


def _mm_scale_kernel(a_ref, w_ref, s_ref, o_ref, wbf_ref):
    @pl.when(pl.program_id(1) == 0)
    def _():
        wbf_ref[...] = w_ref[...].astype(BF16)

    acc = jnp.dot(a_ref[...], wbf_ref[...], preferred_element_type=F32)
    o_ref[...] = (acc * s_ref[...]).astype(o_ref.dtype)


def matmul_cols(a, w, col_off, n_cols, *, tm, tn, out_dtype, name):
    m_rows, k = a.shape
    assert w.shape[0] == k and m_rows % tm == 0 and n_cols % tn == 0 and col_off % tn == 0
    off = col_off // tn
    return pl.pallas_call(
        _mm_kernel,
        grid=(n_cols // tn, m_rows // tm),
        in_specs=[pl.BlockSpec((tm, k), lambda n, m: (m, 0)),
                  pl.BlockSpec((k, tn), lambda n, m: (0, n + off))],
        out_specs=pl.BlockSpec((tm, tn), lambda n, m: (m, n)),
        out_shape=jax.ShapeDtypeStruct((m_rows, n_cols), out_dtype),
        scratch_shapes=[pltpu.VMEM((k, tn), BF16)],
        compiler_params=_params(
            ("arbitrary", "arbitrary"),
            [_nbytes((tm, k), BF16), _nbytes((k, tn), F32), _nbytes((tm, tn), out_dtype)],
            [_nbytes((k, tn), BF16), _nbytes((tm, tn), F32)]),
        name=name,
    )(a, w)


def _mm_silu_kernel(a_ref, w_ref, o_ref, wbf_ref, *, parts):
    @pl.when(pl.program_id(1) == 0)
    def _():
        wbf_ref[...] = w_ref[...].astype(BF16)

    rows = a_ref.shape[0] // parts
    for p in range(parts):
        part = slice(p * rows, (p + 1) * rows)
        o_ref[part, :] = _silu(jnp.dot(a_ref[part, :], wbf_ref[...], preferred_element_type=F32))


def _mm_conv_silu_kernel(a_ref, w_ref, cw_ref, cb_ref, o_ref, wbf_ref, raw_ref, *, parts):
    tm, tn = o_ref.shape

    @pl.when(pl.program_id(1) == 0)
    def _():
        wbf_ref[...] = w_ref[...].astype(BF16)
        raw_ref[0:CONV_HALO, :] = jnp.zeros((CONV_HALO, tn), F32)

    rows = tm // parts
    for p in range(parts):
        lo = CONV_HALO + p * rows
        raw_ref[lo:lo + rows, :] = jnp.dot(a_ref[p * rows:(p + 1) * rows, :], wbf_ref[...],
                                           preferred_element_type=F32)
        for s in range(tn // LANES):
            cols = slice(s * LANES, (s + 1) * LANES)
            acc = cb_ref[:, cols] + cw_ref[SSD_CONV - 1:SSD_CONV, cols] * raw_ref[lo:lo + rows, cols]
            for k in range(SSD_CONV - 1):
                back = SSD_CONV - 1 - k
                acc = acc + cw_ref[k:k + 1, cols] * raw_ref[lo - back:lo - back + rows, cols]
            o_ref[p * rows:(p + 1) * rows, cols] = _silu(acc)
    raw_ref[0:CONV_HALO, :] = raw_ref[tm:tm + CONV_HALO, :]


def matmul_cols_act(a, w, col_off, n_cols, *, tm, tn, parts, conv=None, name):
    m_rows, k = a.shape
    assert w.shape[0] == k and m_rows % tm == 0 and n_cols % tn == 0 and col_off % tn == 0
    assert tm % parts == 0 and (tm // parts) % CONV_HALO == 0
    off = col_off // tn
    in_specs = [pl.BlockSpec((tm, k), lambda n, m: (m, 0)),
                pl.BlockSpec((k, tn), lambda n, m: (0, n + off))]
    scratch = [pltpu.VMEM((k, tn), BF16)]
    operands = [a, w]
    if conv is None:
        body = functools.partial(_mm_silu_kernel, parts=parts)
    else:
        body = functools.partial(_mm_conv_silu_kernel, parts=parts)
        in_specs += [pl.BlockSpec((SSD_CONV, tn), lambda n, m: (0, n)),
                     pl.BlockSpec((1, tn), lambda n, m: (0, n))]
        scratch.append(pltpu.VMEM((CONV_HALO + tm, tn), F32))
        operands += list(conv)
    return pl.pallas_call(
        body,
        grid=(n_cols // tn, m_rows // tm),
        in_specs=in_specs,
        out_specs=pl.BlockSpec((tm, tn), lambda n, m: (m, n)),
        out_shape=jax.ShapeDtypeStruct((m_rows, n_cols), F32),
        scratch_shapes=scratch,
        compiler_params=_params(
            ("arbitrary", "arbitrary"),
            [_nbytes((tm, k), BF16), _nbytes((k, tn), F32), _nbytes((tm, tn), F32)],
            [_nbytes((k, tn), BF16), 2 * _nbytes((tm, tn), F32)]),
        name=name,
    )(*operands)


def _mm_bf16w_kernel(a_ref, w_ref, o_ref):
    o_ref[...] = jnp.dot(a_ref[...], w_ref[...], preferred_element_type=F32).astype(o_ref.dtype)


def matmul_bf16w(a, w, *, tm, tn, out_dtype, name):
    m_rows, k = a.shape
    n_cols = w.shape[1]
    assert w.shape[0] == k and m_rows % tm == 0 and n_cols % tn == 0
    return pl.pallas_call(
        _mm_bf16w_kernel,
        grid=(n_cols // tn, m_rows // tm),
        in_specs=[pl.BlockSpec((tm, k), lambda n, m: (m, 0)),
                  pl.BlockSpec((k, tn), lambda n, m: (0, n))],
        out_specs=pl.BlockSpec((tm, tn), lambda n, m: (m, n)),
        out_shape=jax.ShapeDtypeStruct((m_rows, n_cols), out_dtype),
        compiler_params=_params(
            ("arbitrary", "arbitrary"),
            [_nbytes((tm, k), BF16), _nbytes((k, tn), BF16), _nbytes((tm, tn), out_dtype)],
            [_nbytes((tm, tn), F32)]),
        name=name,
    )(a, w)


def pool_group_matmul(p, w_group, scale, *, tm):
    m_rows = p.shape[0]
    gc = POOL_GC
    assert m_rows % tm == 0
    return pl.pallas_call(
        _mm_scale_kernel,
        grid=(POOL_GROUPS, m_rows // tm),
        in_specs=[pl.BlockSpec((tm, gc), lambda g, m: (m, g)),
                  pl.BlockSpec((None, gc, gc), lambda g, m: (g, 0, 0)),
                  pl.BlockSpec((1, gc), lambda g, m: (0, g))],
        out_specs=pl.BlockSpec((tm, gc), lambda g, m: (m, g)),
        out_shape=jax.ShapeDtypeStruct((m_rows, POOL_GROUPS * gc), BF16),
        scratch_shapes=[pltpu.VMEM((gc, gc), BF16)],
        compiler_params=_params(
            ("arbitrary", "arbitrary"),
            [_nbytes((tm, gc), BF16), _nbytes((gc, gc), F32), _nbytes((tm, gc), BF16)],
            [_nbytes((gc, gc), BF16), _nbytes((tm, gc), F32)]),
        name="pool_group_matmul",
    )(p, w_group, scale.reshape(1, -1))


def _pool_in_kernel(x_ref, w_ref, p_ref, wbf_ref, ubuf_ref, *, tm, tn):
    n = pl.program_id(0)
    m = pl.program_id(1)

    @pl.when(m == 0)
    def _():
        wbf_ref[...] = w_ref[...].astype(BF16)
        ubuf_ref[0:POOL_HALO, :] = jnp.zeros((POOL_HALO, tn), F32)

    u = jnp.dot(x_ref[...], wbf_ref[...], preferred_element_type=F32)
    ubuf_ref[POOL_HALO:POOL_HALO + tm, :] = u
    pos = (m * tm + 1 + lax.broadcasted_iota(I32, (tm, tn), 0)).astype(F32)
    group = (n * tn) // POOL_GC
    for gi, window in enumerate(POOL_WINDOWS):
        @pl.when(group == gi)
        def _(window=window):
            s = ubuf_ref[...]
            span = 1
            while span < window:
                s = s + pltpu.roll(s, span, axis=0)
                span *= 2
            acc = s[POOL_HALO:POOL_HALO + tm, :]
            mean = acc / jnp.minimum(pos, float(window))
            p_ref[...] = (mean - u).astype(p_ref.dtype)

    ubuf_ref[0:POOL_HALO, :] = ubuf_ref[tm:tm + POOL_HALO, :]


def pool_in(xb, w_in, *, tm, tn):
    m_rows, k = xb.shape
    n_cols = w_in.shape[1]
    assert m_rows % tm == 0 and n_cols % tn == 0 and POOL_GC % tn == 0 and tm >= POOL_HALO
    return pl.pallas_call(
        functools.partial(_pool_in_kernel, tm=tm, tn=tn),
        grid=(n_cols // tn, m_rows // tm),
        in_specs=[pl.BlockSpec((tm, k), lambda n, m: (m, 0)),
                  pl.BlockSpec((k, tn), lambda n, m: (0, n))],
        out_specs=pl.BlockSpec((tm, tn), lambda n, m: (m, n)),
        out_shape=jax.ShapeDtypeStruct((m_rows, n_cols), BF16),
        scratch_shapes=[pltpu.VMEM((k, tn), BF16), pltpu.VMEM((tm + POOL_HALO, tn), F32)],
        compiler_params=_params(
            ("arbitrary", "arbitrary"),
            [_nbytes((tm, k), BF16), _nbytes((k, tn), F32), _nbytes((tm, tn), BF16)],
            [_nbytes((k, tn), BF16), 4 * _nbytes((tm + POOL_HALO, tn), F32)]),
        name="pool_in",
    )(xb, w_in)


def _layer_norm_rows(v, g, b):
    mu = jnp.mean(v, axis=-1, keepdims=True)
    vc = v - mu
    var = jnp.mean(vc * vc, axis=-1, keepdims=True)
    return vc * lax.rsqrt(var + LN_EPS) * g + b


def _ln_router_kernel(x_ref, h_ref, g_ref, b_ref, wr_ref, rb_ref,
                      xo_ref, xpk_ref, e_ref, gate_ref, rank_ref, cnt_ref, base_ref, *, tm):
    i = pl.program_id(0)

    @pl.when(i == 0)
    def _():
        base_ref[...] = jnp.zeros_like(base_ref)

    y = _layer_norm_rows(DEEPNORM_ALPHA * x_ref[...] + h_ref[...], g_ref[...], b_ref[...])
    xo_ref[...] = y
    yb = y.astype(BF16)
    half = y.shape[1] // 2
    lo_bits = pltpu.bitcast(yb[:, :half].astype(F32), U32) >> 16
    hi_bits = pltpu.bitcast(yb[:, half:].astype(F32), U32)
    xpk_ref[...] = hi_bits | lo_bits

    ng, epg = N_EXPERT_GROUPS, EXPERTS_PER_GROUP
    logits = lax.dot_general(wr_ref[...], yb, (((1,), (1,)), ((), ())),
                             preferred_element_type=F32)
    scores = jax.nn.sigmoid(logits)
    sel = scores + rb_ref[...]
    sel_j = [sel[ng * j:ng * (j + 1), :] for j in range(epg)]
    sc_j = [scores[ng * j:ng * (j + 1), :] for j in range(epg)]
    gscore = None
    for j1 in range(epg):
        for j2 in range(j1 + 1, epg):
            s = sel_j[j1] + sel_j[j2]
            gscore = s if gscore is None else jnp.maximum(gscore, s)
    gmax = jnp.max(gscore, axis=0, keepdims=True)
    giota = lax.broadcasted_iota(I32, (ng, tm), 0)
    g_idx = jnp.min(jnp.where(gscore == gmax, giota, ng), axis=0, keepdims=True)
    in_g = giota == g_idx
    v = [jnp.sum(jnp.where(in_g, sel_j[j], 0.0), axis=0, keepdims=True) for j in range(epg)]
    s = [jnp.sum(jnp.where(in_g, sc_j[j], 0.0), axis=0, keepdims=True) for j in range(epg)]
    order = []
    for j in range(epg):
        r = jnp.zeros((1, tm), I32)
        for k in range(epg):
            if k == j:
                continue
            beats = (v[k] >= v[j]) if k < j else (v[k] > v[j])
            r = r + beats.astype(I32)
        order.append(r)
    eiota = lax.broadcasted_iota(I32, (N_EXPERTS, tm), 0)
    onehot = jnp.zeros((N_EXPERTS, tm), F32)
    loc, raw, rows = [], [], []
    for slot in range(TOP_K):
        lj = jnp.zeros((1, tm), I32)
        gs = jnp.zeros((1, tm), F32)
        for j in range(epg):
            hit = order[j] == slot
            lj = lj + jnp.where(hit, j, 0)
            gs = gs + jnp.where(hit, s[j], 0.0)
        row = lj * ng + g_idx
        onehot = onehot + (eiota == row).astype(F32)
        loc.append(lj)
        raw.append(gs)
        rows.append(row)
    denom = raw[0] + raw[1]
    srow = lax.broadcasted_iota(I32, (tm, tm), 0)
    scol = lax.broadcasted_iota(I32, (tm, tm), 1)
    before = (srow < scol).astype(BF16)
    prefix = jnp.dot(onehot.astype(BF16), before, preferred_element_type=F32) + base_ref[...]
    for slot in range(TOP_K):
        e_ref[slot:slot + 1, :] = g_idx * epg + loc[slot]
        gate_ref[slot:slot + 1, :] = raw[slot] / denom
        rk = jnp.sum(jnp.where(eiota == rows[slot], prefix, 0.0), axis=0, keepdims=True)
        rank_ref[slot:slot + 1, :] = rk.astype(I32)
    base_ref[...] = base_ref[...] + jnp.sum(onehot, axis=1, keepdims=True)
    cnt_ref[...] = jnp.broadcast_to(base_ref[...], cnt_ref.shape)


def ln_router(x, h, g, b, w_router, router_bias, *, tm):
    t, d = x.shape
    assert t % tm == 0
    ng, epg = N_EXPERT_GROUPS, EXPERTS_PER_GROUP
    wr = w_router.T.reshape(ng, epg, d).transpose(1, 0, 2).reshape(N_EXPERTS, d).astype(BF16)
    rb = router_bias.astype(F32).reshape(ng, epg).T.reshape(N_EXPERTS, 1)
    row = pl.BlockSpec((tm, d), lambda i: (i, 0))
    vec = pl.BlockSpec((1, d), lambda i: (0, 0))
    tok = pl.BlockSpec((TOP_K, tm), lambda i: (0, i))
    outs = pl.pallas_call(
        functools.partial(_ln_router_kernel, tm=tm),
        grid=(t // tm,),
        in_specs=[row, row, vec, vec,
                  pl.BlockSpec((N_EXPERTS, d), lambda i: (0, 0)),
                  pl.BlockSpec((N_EXPERTS, 1), lambda i: (0, 0))],
        out_specs=[row, pl.BlockSpec((tm, d // 2), lambda i: (i, 0)), tok, tok, tok,
                   pl.BlockSpec((N_EXPERTS, LANES), lambda i: (0, 0))],
        out_shape=[jax.ShapeDtypeStruct((t, d), F32), jax.ShapeDtypeStruct((t, d // 2), U32),
                   jax.ShapeDtypeStruct((TOP_K, t), I32), jax.ShapeDtypeStruct((TOP_K, t), F32),
                   jax.ShapeDtypeStruct((TOP_K, t), I32),
                   jax.ShapeDtypeStruct((N_EXPERTS, LANES), F32)],
        scratch_shapes=[pltpu.VMEM((N_EXPERTS, 1), F32)],
        compiler_params=_params(
            ("arbitrary",),
            [3 * _nbytes((tm, d), F32), _nbytes((tm, d), BF16), _nbytes((N_EXPERTS, d), BF16)],
            [4 * _nbytes((tm, d), F32)]),
        name="ln_router",
    )(x, h, g.reshape(1, d), b.reshape(1, d), wr, rb)
    x_new, xpk, e_idx, gate, rank, cnt = outs
    counts = cnt[:, 0].astype(I32).reshape(epg, ng).T.reshape(N_EXPERTS)
    return x_new, xpk, e_idx, gate, rank, counts


def _weight_copies(w_hbm, layer, ex, stage, wsem):
    rows = stage.shape[0] // WEIGHT_DMA_PARTS
    assert rows * WEIGHT_DMA_PARTS == stage.shape[0] and rows % SUBLANES == 0
    return [pltpu.make_async_copy(w_hbm.at[layer, ex, pl.ds(p * rows, rows)],
                                  stage.at[pl.ds(p * rows, rows)], wsem.at[p])
            for p in range(WEIGHT_DMA_PARTS)]


def _weight_priority(p):
    return ROW_DMA_PRIORITY if p < WEIGHT_PARTS_ON_ROW_QUEUE else WEIGHT_DMA_PRIORITY


def _start_weights(copies):
    for p, cp in enumerate(copies):
        cp.start(priority=_weight_priority(p))


def _start_all(copies):
    for p, cp in enumerate(copies):
        cp.start(priority=p % 2)


def _wait_all(copies):
    for cp in copies:
        cp.wait()


def _wait_cast_refill(copies, refills, refill, stage, wbf):
    rows = stage.shape[0] // len(copies)
    for p, cp in enumerate(copies):
        cp.wait()
        wbf[p * rows:(p + 1) * rows, :] = stage[p * rows:(p + 1) * rows, :].astype(BF16)

        @pl.when(refill)
        def _(p=p):
            refills[p].start(priority=_weight_priority(p))


def _expert_in_kernel(src_ref, crow_ref, cbase_ref, nch_ref, total_ref, xpk_hbm, w_hbm, act_hbm,
                      stage, wbf, ibuf, obuf, wsem, isem, osem, *, layer):
    e = pl.program_id(0)
    last = pl.num_programs(0) - 1
    total = total_ref[0]
    half = xpk_hbm.shape[1]

    def w_copies(ex):
        return _weight_copies(w_hbm, layer, ex, stage, wsem)

    def start_gather(g):
        row0 = crow_ref[g]
        for r in range(EXPERT_CHUNK):
            tok = src_ref[row0 + r]
            pltpu.make_async_copy(xpk_hbm.at[pl.ds(tok, 1)], ibuf.at[g % 2, pl.ds(r, 1)],
                                  isem.at[g % 2]).start(priority=ROW_DMA_PRIORITY)

    def wait_gather(g):
        pltpu.make_async_copy(xpk_hbm.at[pl.ds(0, EXPERT_CHUNK)], ibuf.at[g % 2], isem.at[g % 2]).wait()

    def out_copy(g):
        rows = pl.ds(pl.multiple_of(crow_ref[g], BLOCK_ROWS), EXPERT_CHUNK)
        return pltpu.make_async_copy(obuf.at[g % 2], act_hbm.at[rows], osem.at[g % 2])

    @pl.when(e == 0)
    def _():
        _start_weights(w_copies(e))

        @pl.when(total > 0)
        def _():
            start_gather(0)

    _wait_cast_refill(w_copies(e), w_copies(e + 1), e < last, stage, wbf)

    base = cbase_ref[e]

    def body(i, carry):
        g = base + i

        @pl.when(g + 1 < total)
        def _():
            start_gather(g + 1)

        wait_gather(g)
        packed = ibuf[g % 2]
        x_lo = pltpu.bitcast(packed << 16, F32).astype(BF16)
        x_hi = pltpu.bitcast(packed & jnp.uint32(0xFFFF0000), F32).astype(BF16)
        h = (jnp.dot(x_lo, wbf[0:half, :], preferred_element_type=F32)
             + jnp.dot(x_hi, wbf[half:2 * half, :], preferred_element_type=F32))
        h1 = h[:, :D_FF]
        obuf[g % 2] = (h1 * jax.nn.sigmoid(h1) * h[:, D_FF:]).astype(obuf.dtype)

        @pl.when(g >= 1)
        def _():
            out_copy(g - 1).wait()

        out_copy(g).start()
        return carry

    lax.fori_loop(0, nch_ref[e], body, 0)

    @pl.when(e == last)
    def _():
        @pl.when(total > 0)
        def _():
            out_copy(total - 1).wait()

        end = jnp.where(total > 0, crow_ref[jnp.maximum(total - 1, 0)] + EXPERT_CHUNK, 0)
        obuf[0] = jnp.zeros(obuf.shape[1:], obuf.dtype)

        def tail_copy(blk):
            rows = pl.ds(pl.multiple_of(end + blk * BLOCK_ROWS, BLOCK_ROWS), BLOCK_ROWS)
            return pltpu.make_async_copy(obuf.at[0, 0:BLOCK_ROWS], act_hbm.at[rows], osem.at[0])

        n_tail = (act_hbm.shape[0] - end) // BLOCK_ROWS
        lax.fori_loop(0, n_tail, lambda blk, c: (tail_copy(blk).start(), c)[1], 0)
        lax.fori_loop(0, n_tail, lambda blk, c: (tail_copy(blk).wait(), c)[1], 0)


def expert_in(xpk, src, chunk_row, chunk_base, n_chunks, total, moe_w_in, layer):
    rows = src.shape[0]
    half = xpk.shape[1]
    wshape = moe_w_in.shape[-2:]
    resident = [_nbytes(wshape, F32), _nbytes(wshape, BF16),
                2 * _nbytes((EXPERT_CHUNK, half), U32), 2 * _nbytes((EXPERT_CHUNK, D_FF), BF16),
                _nbytes((EXPERT_CHUNK, 2 * half), BF16), 2 * _nbytes((EXPERT_CHUNK, wshape[1]), F32)]
    return pl.pallas_call(
        functools.partial(_expert_in_kernel, layer=layer),
        grid_spec=pltpu.PrefetchScalarGridSpec(
            num_scalar_prefetch=5,
            grid=(N_EXPERTS,),
            in_specs=[pl.BlockSpec(memory_space=pl.ANY), pl.BlockSpec(memory_space=pl.ANY)],
            out_specs=pl.BlockSpec(memory_space=pl.ANY),
            scratch_shapes=[pltpu.VMEM(wshape, F32), pltpu.VMEM(wshape, BF16),
                            pltpu.VMEM((2, EXPERT_CHUNK, half), U32),
                            pltpu.VMEM((2, EXPERT_CHUNK, D_FF), BF16),
                            pltpu.SemaphoreType.DMA((WEIGHT_DMA_PARTS,)),
                            pltpu.SemaphoreType.DMA((2,)),
                            pltpu.SemaphoreType.DMA((2,))]),
        out_shape=jax.ShapeDtypeStruct((rows, D_FF), BF16),
        compiler_params=pltpu.CompilerParams(
            dimension_semantics=("arbitrary",), has_side_effects=True,
            vmem_limit_bytes=int(min(VMEM_CAP, sum(resident) + 4 * 1024 * 1024))),
        name="expert_in",
    )(src, chunk_row, chunk_base, n_chunks, total, xpk, moe_w_in)


def _expert_kernel(pstart_ref, padded_ref, in_hbm, w_hbm, out_hbm,
                   stage, wbf, ibuf, obuf, wsem, isem, osem, *, layer):
    e = pl.program_id(0)

    def w_copies(ex):
        return _weight_copies(w_hbm, layer, ex, stage, wsem)

    @pl.when(e == 0)
    def _():
        _start_weights(w_copies(e))

    _wait_cast_refill(w_copies(e), w_copies(e + 1), e + 1 < pl.num_programs(0), stage, wbf)

    start = pstart_ref[e]
    n_chunks = (padded_ref[e] + EXPERT_CHUNK - 1) // EXPERT_CHUNK

    def chunk_rows(i):
        return pl.ds(pl.multiple_of(start + i * EXPERT_CHUNK, BLOCK_ROWS), EXPERT_CHUNK)

    def in_copy(i):
        return pltpu.make_async_copy(in_hbm.at[chunk_rows(i)], ibuf.at[i % 2], isem.at[i % 2])

    def out_copies(i):
        copies = []
        for p in range(EXPERT_CHUNK // BLOCK_ROWS):
            rows = pl.ds(pl.multiple_of(start + i * EXPERT_CHUNK + p * BLOCK_ROWS, BLOCK_ROWS),
                         BLOCK_ROWS)
            copies.append(pltpu.make_async_copy(
                obuf.at[i % 2, pl.ds(p * BLOCK_ROWS, BLOCK_ROWS)], out_hbm.at[rows], osem.at[i % 2]))
        return copies

    @pl.when(n_chunks > 0)
    def _():
        in_copy(0).start()

    def body(i, carry):
        @pl.when(i + 1 < n_chunks)
        def _():
            in_copy(i + 1).start()

        in_copy(i).wait()
        obuf[i % 2] = jnp.dot(ibuf[i % 2], wbf[...], preferred_element_type=F32).astype(obuf.dtype)
        _start_all(out_copies(i))

        @pl.when(i >= 1)
        def _():
            _wait_all(out_copies(i - 1))
        return carry

    lax.fori_loop(0, n_chunks, body, 0)

    @pl.when(n_chunks > 0)
    def _():
        _wait_all(out_copies(n_chunks - 1))

    @pl.when(e + 1 == pl.num_programs(0))
    def _():
        end = start + n_chunks * EXPERT_CHUNK
        obuf[0] = jnp.zeros(obuf.shape[1:], obuf.dtype)

        def tail_copy(blk):
            rows = pl.ds(pl.multiple_of(end + blk * BLOCK_ROWS, BLOCK_ROWS), BLOCK_ROWS)
            return pltpu.make_async_copy(obuf.at[0, 0:BLOCK_ROWS], out_hbm.at[rows], osem.at[0])

        n_tail = (out_hbm.shape[0] - end) // BLOCK_ROWS
        lax.fori_loop(0, n_tail, lambda blk, c: (tail_copy(blk).start(), c)[1], 0)
        lax.fori_loop(0, n_tail, lambda blk, c: (tail_copy(blk).wait(), c)[1], 0)


def _expert_call(rows_in, w, pad_start, padded, layer, *, out_dtype, name):
    rows, k = rows_in.shape
    n_out = w.shape[-1]
    wshape = w.shape[-2:]
    resident = [_nbytes(wshape, F32), _nbytes(wshape, BF16),
                2 * _nbytes((EXPERT_CHUNK, k), rows_in.dtype),
                2 * _nbytes((EXPERT_CHUNK, n_out), out_dtype),
                2 * _nbytes((EXPERT_CHUNK, wshape[1]), F32)]
    return pl.pallas_call(
        functools.partial(_expert_kernel, layer=layer),
        grid_spec=pltpu.PrefetchScalarGridSpec(
            num_scalar_prefetch=2,
            grid=(N_EXPERTS,),
            in_specs=[pl.BlockSpec(memory_space=pl.ANY), pl.BlockSpec(memory_space=pl.ANY)],
            out_specs=pl.BlockSpec(memory_space=pl.ANY),
            scratch_shapes=[pltpu.VMEM(wshape, F32), pltpu.VMEM(wshape, BF16),
                            pltpu.VMEM((2, EXPERT_CHUNK, k), rows_in.dtype),
                            pltpu.VMEM((2, EXPERT_CHUNK, n_out), out_dtype),
                            pltpu.SemaphoreType.DMA((WEIGHT_DMA_PARTS,)),
                            pltpu.SemaphoreType.DMA((2,)),
                            pltpu.SemaphoreType.DMA((2,))]),
        out_shape=jax.ShapeDtypeStruct((rows, n_out), out_dtype),
        compiler_params=pltpu.CompilerParams(
            dimension_semantics=("arbitrary",), has_side_effects=True,
            vmem_limit_bytes=int(min(VMEM_CAP, sum(resident) + 4 * 1024 * 1024))),
        name=name,
    )(pad_start, padded, rows_in, w)


def expert_out(act, pad_start, padded, moe_w_out, layer):
    return _expert_call(act, moe_w_out, pad_start, padded, layer, out_dtype=F32,
                        name="expert_out")


def _combine_ln_kernel(dest_ref, x_ref, gate_ref, g_ref, b_ref, y_hbm, *rest, tm, t, with_bf16):
    if with_bf16:
        xo_ref, xb_ref, ybuf, sem = rest
    else:
        xo_ref, ybuf, sem = rest
        xb_ref = None
    i = pl.program_id(0)
    n = pl.num_programs(0)

    def start_rows(step):
        buf = step % 2

        for r in range(tm):
            for k in range(TOP_K):
                d = dest_ref[k * t + step * tm + r]
                pltpu.make_async_copy(y_hbm.at[pl.ds(d, 1)], ybuf.at[buf, k, pl.ds(r, 1)],
                                      sem.at[buf]).start(priority=(r + k) % 2)

    def wait_rows(step):
        buf = step % 2
        for k in range(TOP_K):
            pltpu.make_async_copy(y_hbm.at[pl.ds(0, tm)], ybuf.at[buf, k], sem.at[buf]).wait()

    @pl.when(i == 0)
    def _():
        start_rows(i)

    @pl.when(i + 1 < n)
    def _():
        start_rows(i + 1)

    wait_rows(i)
    buf = i % 2
    gate = gate_ref[...]
    h = gate[:, 0:1] * ybuf[buf, 0] + gate[:, 1:2] * ybuf[buf, 1]
    y = _layer_norm_rows(DEEPNORM_ALPHA * x_ref[...] + h, g_ref[...], b_ref[...])
    xo_ref[...] = y
    if with_bf16:
        xb_ref[...] = y.astype(BF16)


def combine_ln(x, y_disp, dest_flat, gate_t, g, b, *, tm, with_bf16):
    t, d = x.shape
    assert t % tm == 0
    row = pl.BlockSpec((tm, d), lambda i, dest: (i, 0))
    vec = pl.BlockSpec((1, d), lambda i, dest: (0, 0))
    out_specs = [row, row] if with_bf16 else [row]
    out_shape = [jax.ShapeDtypeStruct((t, d), F32)]
    if with_bf16:
        out_shape.append(jax.ShapeDtypeStruct((t, d), BF16))
    outs = pl.pallas_call(
        functools.partial(_combine_ln_kernel, tm=tm, t=t, with_bf16=with_bf16),
        grid_spec=pltpu.PrefetchScalarGridSpec(
            num_scalar_prefetch=1,
            grid=(t // tm,),
            in_specs=[row, pl.BlockSpec((tm, TOP_K), lambda i, dest: (i, 0)), vec, vec,
                      pl.BlockSpec(memory_space=pl.ANY)],
            out_specs=out_specs,
            scratch_shapes=[pltpu.VMEM((2, TOP_K, tm, d), F32), pltpu.SemaphoreType.DMA((2,))]),
        out_shape=out_shape,
        compiler_params=_params(
            ("arbitrary",),
            [2 * _nbytes((tm, d), F32), _nbytes((tm, d), BF16)],
            [_nbytes((2, TOP_K, tm, d), F32), 4 * _nbytes((tm, d), F32)]),
        name="moe_combine_ln",
    )(dest_flat, x, gate_t, g.reshape(1, d), b.reshape(1, d), y_disp)
    return (outs[0], outs[1]) if with_bf16 else (outs[0], None)


def moe_ffn_ln(x, xpk, e_idx, gate, rank, counts, moe_w_in, moe_w_out, layer, g, b, *,
               with_bf16, tm):
    t, d = x.shape
    tk = t * TOP_K
    padded = ((counts + BLOCK_ROWS - 1) // BLOCK_ROWS * BLOCK_ROWS).astype(I32)
    pad_end = jnp.cumsum(padded).astype(I32)
    pad_start = pad_end - padded
    n_blocks = (tk + N_EXPERTS * (BLOCK_ROWS - 1) + BLOCK_ROWS - 1) // BLOCK_ROWS + 1
    e_flat = e_idx.reshape(tk)
    dest_flat = pad_start[e_flat] + rank.reshape(tk)
    tok_flat = jnp.arange(tk, dtype=I32) % t
    src = jnp.zeros((n_blocks * BLOCK_ROWS,), I32).at[dest_flat].set(tok_flat)
    n_chunks = (padded + EXPERT_CHUNK - 1) // EXPERT_CHUNK
    chunk_end = jnp.cumsum(n_chunks).astype(I32)
    chunk_base = chunk_end - n_chunks
    max_chunks = n_blocks // (EXPERT_CHUNK // BLOCK_ROWS) + N_EXPERTS
    gidx = jnp.arange(max_chunks, dtype=I32)
    owner = jnp.minimum(jnp.sum(gidx[:, None] >= chunk_end[None, :], axis=1), N_EXPERTS - 1)
    chunk_row = (pad_start[owner] + (gidx - chunk_base[owner]) * EXPERT_CHUNK).astype(I32)
    act = expert_in(xpk, src, chunk_row, chunk_base, n_chunks.astype(I32), chunk_end[-1:],
                    moe_w_in, layer)
    y_disp = expert_out(act, pad_start, padded, moe_w_out, layer)
    return combine_ln(x, y_disp, dest_flat, gate.T, g, b, tm=tm, with_bf16=with_bf16)


def _softplus(x):
    return jnp.maximum(x, 0.0) + jnp.log1p(jnp.exp(-jnp.abs(x)))


def _silu(x):
    return x * jax.nn.sigmoid(x)


def _dot_01_f32(m01, v, *, ones_left):
    hi = v.astype(BF16)
    r = v - hi.astype(F32)
    mid = r.astype(BF16)
    lo = (r - mid.astype(F32)).astype(BF16)

    def dot(p):
        lhs, rhs = (m01, p) if ones_left else (p, m01)
        return jnp.dot(lhs, rhs, preferred_element_type=F32)
    return dot(hi) + dot(mid) + dot(lo)


def _ssd_kernel(xs_ref, b_ref, c_ref, gz_ref, dt_ref, dtT_ref,
                dtb_ref, dtbT_ref, alog_ref, alogT_ref, dsk_ref, nw_ref,
                y_ref, state_ref):
    q = SSD_CHUNK

    @pl.when(pl.program_id(1) == 0)
    def _():
        state_ref[...] = jnp.zeros_like(state_ref)

    xs = xs_ref[...]
    bm = b_ref[...]
    cm = c_ref[...]

    dt = _softplus(dt_ref[...] + dtb_ref[...])
    dt_t = _softplus(dtT_ref[...] + dtbT_ref[...])
    a = -jnp.exp(alog_ref[...])
    a_t = -jnp.exp(alogT_ref[...])
    row = lax.broadcasted_iota(I32, (q, q), 0)
    col = lax.broadcasted_iota(I32, (q, q), 1)
    causal = row >= col
    a_cum = _dot_01_f32(jnp.where(causal, 1.0, 0.0).astype(BF16), dt * a, ones_left=True)
    a_cum_t = _dot_01_f32(jnp.where(row <= col, 1.0, 0.0).astype(BF16), dt_t * a_t,
                          ones_left=False)
    a_last = a_cum[q - 1:q, :]
    hrow = lax.broadcasted_iota(I32, (SSD_HPG, SSD_GW), 0)
    hcol = lax.broadcasted_iota(I32, (SSD_HPG, SSD_GW), 1)
    expand = jnp.where(hcol // SSD_HEAD_DIM == hrow, 1.0, 0.0).astype(BF16)
    decay_in = _dot_01_f32(expand, jnp.exp(a_cum), ones_left=False)
    w_state = _dot_01_f32(expand, jnp.exp(a_last - a_cum) * dt, ones_left=False)
    chunk_decay = decay_in[q - 1:q, :]

    cmb = cm.astype(BF16)
    bmb = bm.astype(BF16)
    cb = lax.dot_general(cmb, bmb, (((1,), (1,)), ((), ())), preferred_element_type=F32)
    prev = state_ref[...]
    y_off = jnp.dot(cmb, prev.astype(BF16), preferred_element_type=F32) * decay_in
    xw = (w_state * xs).astype(BF16)
    state_ref[...] = chunk_decay * prev + jnp.dot(bm.T.astype(BF16), xw, preferred_element_type=F32)

    lane = lax.broadcasted_iota(I32, (q, LANES), 1)
    heads_per_tile = LANES // SSD_HEAD_DIM
    ys = []
    for tile in range(SSD_GW // LANES):
        ms = []
        for hh in range(heads_per_tile):
            h = tile * heads_per_tile + hh
            seg = a_cum[:, h:h + 1] - a_cum_t[h:h + 1, :]
            decay = jnp.where(causal, jnp.exp(seg), 0.0)
            ms.append((cb * decay * dt_t[h:h + 1, :]).astype(BF16))
        x_tile = xs[:, tile * LANES:(tile + 1) * LANES]
        rhs = jnp.concatenate(
            [jnp.where(lane // SSD_HEAD_DIM == hh, x_tile, 0.0).astype(BF16)
             for hh in range(heads_per_tile)], axis=0)
        ys.append(jnp.dot(jnp.concatenate(ms, axis=1), rhs, preferred_element_type=F32))
    y = jnp.concatenate(ys, axis=1) + y_off + dsk_ref[...] * xs
    y = y * gz_ref[...]
    y = y * lax.rsqrt(jnp.mean(y * y, axis=-1, keepdims=True) + RMS_EPS) * nw_ref[...]
    y_ref[...] = y.astype(y_ref.dtype)


def ssd_core(gz, xbc, dt_raw, dt_bias, a_log, d_skip, norm_w):
    t = gz.shape[0]
    q, gw, ns, hpg, ng = SSD_CHUNK, SSD_GW, SSD_STATE, SSD_HPG, SSD_GROUPS
    assert t % q == 0
    b_blk = D_INNER // ns
    c_blk = b_blk + ng
    dt_g = dt_raw.reshape(t, ng, hpg).transpose(1, 0, 2)
    dt_gt = dt_raw.T.reshape(ng, hpg, t)
    dtb = dt_bias.astype(F32).reshape(ng, 1, hpg)
    dtb_t = dt_bias.astype(F32).reshape(ng, hpg, 1)
    alog = a_log.astype(F32).reshape(ng, 1, hpg)
    alog_t = a_log.astype(F32).reshape(ng, hpg, 1)
    dsk = jnp.repeat(d_skip.astype(F32), SSD_HEAD_DIM).reshape(1, D_INNER)
    nw = norm_w.astype(F32).reshape(1, D_INNER)

    def cspec(rows, width, blk):
        return pl.BlockSpec((rows, width), lambda g, c: (0, blk(g)))

    in_specs = [
        pl.BlockSpec((q, gw), lambda g, c: (c, g)),
        pl.BlockSpec((q, ns), lambda g, c: (c, b_blk + g)),
        pl.BlockSpec((q, ns), lambda g, c: (c, c_blk + g)),
        pl.BlockSpec((q, gw), lambda g, c: (c, g)),
        pl.BlockSpec((None, q, hpg), lambda g, c: (g, c, 0)),
        pl.BlockSpec((None, hpg, q), lambda g, c: (g, 0, c)),
        pl.BlockSpec((None, 1, hpg), lambda g, c: (g, 0, 0)),
        pl.BlockSpec((None, hpg, 1), lambda g, c: (g, 0, 0)),
        pl.BlockSpec((None, 1, hpg), lambda g, c: (g, 0, 0)),
        pl.BlockSpec((None, hpg, 1), lambda g, c: (g, 0, 0)),
        cspec(1, gw, lambda g: g), cspec(1, gw, lambda g: g),
    ]
    return pl.pallas_call(
        _ssd_kernel,
        grid=(ng, t // q),
        in_specs=in_specs,
        out_specs=pl.BlockSpec((q, gw), lambda g, c: (c, g)),
        out_shape=jax.ShapeDtypeStruct((t, D_INNER), BF16),
        scratch_shapes=[pltpu.VMEM((ns, gw), F32)],
        compiler_params=_params(
            ("arbitrary", "arbitrary"),
            [2 * _nbytes((q, gw), F32), 2 * _nbytes((q, ns), F32), _nbytes((q, gw), BF16)],
            [_nbytes((ns, gw), F32), 16 * _nbytes((q, gw), F32)]),
        name="ssd_core",
    )(xbc, xbc, xbc, gz, dt_g, dt_gt, dtb, dtb_t, alog, alog_t, dsk, nw)


MM_TM = 1024
MM_TN = 512
WIDE_K_TM = 512
EPILOGUE_PARTS = 8
LN_TM = 256
COMBINE_TM = 128


def pool_mixer(xb, w_in, w_group, scale, w_out):
    p = pool_in(xb, w_in, tm=MM_TM, tn=MM_TN)
    mixed = pool_group_matmul(p, w_group, scale, tm=MM_TM)
    return matmul_cols(mixed, w_out, 0, D_MODEL, tm=MM_TM, tn=MM_TN, out_dtype=F32, name="pool_out")


def ssd_mixer(xb, w_in, conv_w, conv_b, dt_bias, a_log, d_skip, norm_w, w_out):
    gz = matmul_cols_act(xb, w_in, 0, D_INNER, tm=MM_TM, tn=MM_TN, parts=EPILOGUE_PARTS,
                         name="ssd_in_z")
    conv = (conv_w.astype(F32), conv_b.astype(F32).reshape(1, SSD_CONV_DIM))
    xbc = matmul_cols_act(xb, w_in, D_INNER, SSD_CONV_DIM, tm=MM_TM, tn=MM_TN, parts=EPILOGUE_PARTS,
                          conv=conv, name="ssd_in_xbc")
    dt_raw = matmul_cols(xb, w_in, D_INNER + SSD_CONV_DIM, SSD_HEADS, tm=MM_TM, tn=SSD_HEADS,
                         out_dtype=F32, name="ssd_in_dt")
    y = ssd_core(gz, xbc, dt_raw, dt_bias, a_log, d_skip, norm_w)
    return matmul_bf16w(y, w_out.astype(BF16), tm=WIDE_K_TM, tn=MM_TN, out_dtype=F32, name="ssd_out")


def kernel(x, pool_w_in, pool_w_group, pool_scale, pool_w_out, ssd_w_in, ssd_conv_w, ssd_conv_b,
           ssd_dt_bias, ssd_a_log, ssd_d, ssd_norm_w, ssd_w_out, moe_w_router, moe_router_bias,
           moe_w_in, moe_w_out, ln_mix_g, ln_mix_b, ln_ffn_g, ln_ffn_b):
    bsz, seq, d = x.shape
    x = x.reshape(bsz * seq, d)
    assert bsz == 1
    xb = x.astype(BF16)
    for i in range(DEPTH):
        j = i // N_MIXERS
        if i % N_MIXERS == 0:
            h = pool_mixer(xb, pool_w_in[j], pool_w_group[j], pool_scale[j], pool_w_out[j])
        else:
            h = ssd_mixer(xb, ssd_w_in[j], ssd_conv_w[j], ssd_conv_b[j], ssd_dt_bias[j],
                          ssd_a_log[j], ssd_d[j], ssd_norm_w[j], ssd_w_out[j])
        x, xpk, e_idx, gate, rank, counts = ln_router(
            x, h, ln_mix_g[i], ln_mix_b[i], moe_w_router, moe_router_bias, tm=LN_TM)
        x, xb = moe_ffn_ln(x, xpk, e_idx, gate, rank, counts, moe_w_in, moe_w_out, i,
                           ln_ffn_g[i], ln_ffn_b[i], with_bf16=(i + 1 < DEPTH), tm=COMBINE_TM)
    return x.reshape(bsz, seq, d)
```

```python
import functools

import jax
import jax.numpy as jnp
from jax import lax
from jax.experimental import pallas as pl
from jax.experimental.pallas import tpu as pltpu

F32 = jnp.float32
BF16 = jnp.bfloat16
I32 = jnp.int32
U32 = jnp.uint32

D_MODEL = 4096
DEPTH = 2
N_MIXERS = 2
DEEPNORM_ALPHA = (2 * DEPTH) ** 0.25
LN_EPS = 1e-5
POOL_WINDOWS = (2, 4, 8, 16)
POOL_GROUPS = len(POOL_WINDOWS)
POOL_GC = D_MODEL // POOL_GROUPS
D_INNER = 2 * D_MODEL
SSD_HEAD_DIM = 64
SSD_HEADS = D_INNER // SSD_HEAD_DIM
SSD_GROUPS = 8
SSD_HPG = SSD_HEADS // SSD_GROUPS
SSD_STATE = 128
SSD_CONV = 4
SSD_CHUNK = 128
SSD_GW = D_INNER // SSD_GROUPS
SSD_CONV_DIM = D_INNER + 2 * SSD_GROUPS * SSD_STATE
SSD_IN_DIM = D_INNER + SSD_CONV_DIM + SSD_HEADS
RMS_EPS = 1e-5
N_EXPERTS = 32
N_EXPERT_GROUPS = 8
EXPERTS_PER_GROUP = N_EXPERTS // N_EXPERT_GROUPS
TOP_K = 2
D_FF = 768
BLOCK_ROWS = 128

LANES = 128
SUBLANES = 8
VMEM_BYTES_V7X = 64 * 1024 * 1024
VMEM_CAP = VMEM_BYTES_V7X * 7 // 8

POOL_HALO = 16
CONV_HALO = SUBLANES
EXPERT_CHUNK = 2 * BLOCK_ROWS
WEIGHT_DMA_PARTS = 8
ROW_DMA_PRIORITY = 0
WEIGHT_DMA_PRIORITY = 1


def _nbytes(shape, dtype):
    n = 1
    for s in shape:
        n *= s
    return n * jnp.dtype(dtype).itemsize


def _vmem_limit(pipelined, resident):
    est = 2 * sum(pipelined) + sum(resident)
    return int(min(VMEM_CAP, est + max(est // 4, 8 * 1024 * 1024)))


def _params(semantics, pipelined, resident=()):
    return pltpu.CompilerParams(dimension_semantics=semantics,
                                vmem_limit_bytes=_vmem_limit(pipelined, resident))


def _mm_kernel(a_ref, w_ref, o_ref, wbf_ref):
    @pl.when(pl.program_id(1) == 0)
    def _():
        wbf_ref[...] = w_ref[...].astype(BF16)

    o_ref[...] = jnp.dot(a_ref[...], wbf_ref[...], preferred_element_type=F32).astype(o_ref.dtype)


def _mm_scale_kernel(a_ref, w_ref, s_ref, o_ref, wbf_ref):
    @pl.when(pl.program_id(1) == 0)
    def _():
        wbf_ref[...] = w_ref[...].astype(BF16)

    acc = jnp.dot(a_ref[...], wbf_ref[...], preferred_element_type=F32)
    o_ref[...] = (acc * s_ref[...]).astype(o_ref.dtype)


def matmul_cols(a, w, col_off, n_cols, *, tm, tn, out_dtype, name):
    m_rows, k = a.shape
    assert w.shape[0] == k and m_rows % tm == 0 and n_cols % tn == 0 and col_off % tn == 0
    off = col_off // tn
    return pl.pallas_call(
        _mm_kernel,
        grid=(n_cols // tn, m_rows // tm),
        in_specs=[pl.BlockSpec((tm, k), lambda n, m: (m, 0)),
                  pl.BlockSpec((k, tn), lambda n, m: (0, n + off))],
        out_specs=pl.BlockSpec((tm, tn), lambda n, m: (m, n)),
        out_shape=jax.ShapeDtypeStruct((m_rows, n_cols), out_dtype),
        scratch_shapes=[pltpu.VMEM((k, tn), BF16)],
        compiler_params=_params(
            ("arbitrary", "arbitrary"),
            [_nbytes((tm, k), BF16), _nbytes((k, tn), F32), _nbytes((tm, tn), out_dtype)],
            [_nbytes((k, tn), BF16), _nbytes((tm, tn), F32)]),
        name=name,
    )(a, w)


def _mm_silu_kernel(a_ref, w_ref, o_ref, wbf_ref, *, parts):
    @pl.when(pl.program_id(1) == 0)
    def _():
        wbf_ref[...] = w_ref[...].astype(BF16)

    rows = a_ref.shape[0] // parts
    for p in range(parts):
        part = slice(p * rows, (p + 1) * rows)
        o_ref[part, :] = _silu(jnp.dot(a_ref[part, :], wbf_ref[...], preferred_element_type=F32))


def _mm_conv_silu_kernel(a_ref, w_ref, cw_ref, cb_ref, o_ref, wbf_ref, raw_ref, *, parts):
    tm, tn = o_ref.shape

    @pl.when(pl.program_id(1) == 0)
    def _():
        wbf_ref[...] = w_ref[...].astype(BF16)
        raw_ref[0:CONV_HALO, :] = jnp.zeros((CONV_HALO, tn), F32)

    rows = tm // parts
    for p in range(parts):
        lo = CONV_HALO + p * rows
        raw_ref[lo:lo + rows, :] = jnp.dot(a_ref[p * rows:(p + 1) * rows, :], wbf_ref[...],
                                           preferred_element_type=F32)
        for s in range(tn // LANES):
            cols = slice(s * LANES, (s + 1) * LANES)
            acc = cb_ref[:, cols] + cw_ref[SSD_CONV - 1:SSD_CONV, cols] * raw_ref[lo:lo + rows, cols]
            for k in range(SSD_CONV - 1):
                back = SSD_CONV - 1 - k
                acc = acc + cw_ref[k:k + 1, cols] * raw_ref[lo - back:lo - back + rows, cols]
            o_ref[p * rows:(p + 1) * rows, cols] = _silu(acc)
    raw_ref[0:CONV_HALO, :] = raw_ref[tm:tm + CONV_HALO, :]


def matmul_cols_act(a, w, col_off, n_cols, *, tm, tn, parts, conv=None, name):
    m_rows, k = a.shape
    assert w.shape[0] == k and m_rows % tm == 0 and n_cols % tn == 0 and col_off % tn == 0
    assert tm % parts == 0 and (tm // parts) % CONV_HALO == 0
    off = col_off // tn
    in_specs = [pl.BlockSpec((tm, k), lambda n, m: (m, 0)),
                pl.BlockSpec((k, tn), lambda n, m: (0, n + off))]
    scratch = [pltpu.VMEM((k, tn), BF16)]
    operands = [a, w]
    if conv is None:
        body = functools.partial(_mm_silu_kernel, parts=parts)
    else:
        body = functools.partial(_mm_conv_silu_kernel, parts=parts)
        in_specs += [pl.BlockSpec((SSD_CONV, tn), lambda n, m: (0, n)),
                     pl.BlockSpec((1, tn), lambda n, m: (0, n))]
        scratch.append(pltpu.VMEM((CONV_HALO + tm, tn), F32))
        operands += list(conv)
    return pl.pallas_call(
        body,
        grid=(n_cols // tn, m_rows // tm),
        in_specs=in_specs,
        out_specs=pl.BlockSpec((tm, tn), lambda n, m: (m, n)),
        out_shape=jax.ShapeDtypeStruct((m_rows, n_cols), F32),
        scratch_shapes=scratch,
        compiler_params=_params(
            ("arbitrary", "arbitrary"),
            [_nbytes((tm, k), BF16), _nbytes((k, tn), F32), _nbytes((tm, tn), F32)],
            [_nbytes((k, tn), BF16), 2 * _nbytes((tm, tn), F32)]),
        name=name,
    )(*operands)


def _mm_bf16w_kernel(a_ref, w_ref, o_ref):
    o_ref[...] = jnp.dot(a_ref[...], w_ref[...], preferred_element_type=F32).astype(o_ref.dtype)


def matmul_bf16w(a, w, *, tm, tn, out_dtype, name):
    m_rows, k = a.shape
    n_cols = w.shape[1]
    assert w.shape[0] == k and m_rows % tm == 0 and n_cols % tn == 0
    return pl.pallas_call(
        _mm_bf16w_kernel,
        grid=(n_cols // tn, m_rows // tm),
        in_specs=[pl.BlockSpec((tm, k), lambda n, m: (m, 0)),
                  pl.BlockSpec((k, tn), lambda n, m: (0, n))],
        out_specs=pl.BlockSpec((tm, tn), lambda n, m: (m, n)),
        out_shape=jax.ShapeDtypeStruct((m_rows, n_cols), out_dtype),
        compiler_params=_params(
            ("arbitrary", "arbitrary"),
            [_nbytes((tm, k), BF16), _nbytes((k, tn), BF16), _nbytes((tm, tn), out_dtype)],
            [_nbytes((tm, tn), F32)]),
        name=name,
    )(a, w)


def pool_group_matmul(p, w_group, scale, *, tm):
    m_rows = p.shape[0]
    gc = POOL_GC
    assert m_rows % tm == 0
    return pl.pallas_call(
        _mm_scale_kernel,
        grid=(POOL_GROUPS, m_rows // tm),
        in_specs=[pl.BlockSpec((tm, gc), lambda g, m: (m, g)),
                  pl.BlockSpec((None, gc, gc), lambda g, m: (g, 0, 0)),
                  pl.BlockSpec((1, gc), lambda g, m: (0, g))],
        out_specs=pl.BlockSpec((tm, gc), lambda g, m: (m, g)),
        out_shape=jax.ShapeDtypeStruct((m_rows, POOL_GROUPS * gc), BF16),
        scratch_shapes=[pltpu.VMEM((gc, gc), BF16)],
        compiler_params=_params(
            ("arbitrary", "arbitrary"),
            [_nbytes((tm, gc), BF16), _nbytes((gc, gc), F32), _nbytes((tm, gc), BF16)],
            [_nbytes((gc, gc), BF16), _nbytes((tm, gc), F32)]),
        name="pool_group_matmul",
    )(p, w_group, scale.reshape(1, -1))


def _pool_in_kernel(x_ref, w_ref, p_ref, wbf_ref, ubuf_ref, *, tm, tn):
    n = pl.program_id(0)
    m = pl.program_id(1)

    @pl.when(m == 0)
    def _():
        wbf_ref[...] = w_ref[...].astype(BF16)
        ubuf_ref[0:POOL_HALO, :] = jnp.zeros((POOL_HALO, tn), F32)

    u = jnp.dot(x_ref[...], wbf_ref[...], preferred_element_type=F32)
    ubuf_ref[POOL_HALO:POOL_HALO + tm, :] = u
    pos = (m * tm + 1 + lax.broadcasted_iota(I32, (tm, tn), 0)).astype(F32)
    group = (n * tn) // POOL_GC
    for gi, window in enumerate(POOL_WINDOWS):
        @pl.when(group == gi)
        def _(window=window):
            s = ubuf_ref[...]
            span = 1
            while span < window:
                s = s + pltpu.roll(s, span, axis=0)
                span *= 2
            acc = s[POOL_HALO:POOL_HALO + tm, :]
            mean = acc / jnp.minimum(pos, float(window))
            p_ref[...] = (mean - u).astype(p_ref.dtype)

    ubuf_ref[0:POOL_HALO, :] = ubuf_ref[tm:tm + POOL_HALO, :]


def pool_in(xb, w_in, *, tm, tn):
    m_rows, k = xb.shape
    n_cols = w_in.shape[1]
    assert m_rows % tm == 0 and n_cols % tn == 0 and POOL_GC % tn == 0 and tm >= POOL_HALO
    return pl.pallas_call(
        functools.partial(_pool_in_kernel, tm=tm, tn=tn),
        grid=(n_cols // tn, m_rows // tm),
        in_specs=[pl.BlockSpec((tm, k), lambda n, m: (m, 0)),
                  pl.BlockSpec((k, tn), lambda n, m: (0, n))],
        out_specs=pl.BlockSpec((tm, tn), lambda n, m: (m, n)),
        out_shape=jax.ShapeDtypeStruct((m_rows, n_cols), BF16),
        scratch_shapes=[pltpu.VMEM((k, tn), BF16), pltpu.VMEM((tm + POOL_HALO, tn), F32)],
        compiler_params=_params(
            ("arbitrary", "arbitrary"),
            [_nbytes((tm, k), BF16), _nbytes((k, tn), F32), _nbytes((tm, tn), BF16)],
            [_nbytes((k, tn), BF16), 4 * _nbytes((tm + POOL_HALO, tn), F32)]),
        name="pool_in",
    )(xb, w_in)


def _layer_norm_rows(v, g, b):
    mu = jnp.mean(v, axis=-1, keepdims=True)
    vc = v - mu
    var = jnp.mean(vc * vc, axis=-1, keepdims=True)
    return vc * lax.rsqrt(var + LN_EPS) * g + b


def _ln_router_kernel(x_ref, h_ref, g_ref, b_ref, wr_ref, rb_ref,
                      xo_ref, xpk_ref, e_ref, gate_ref, rank_ref, cnt_ref, base_ref, *, tm):
    i = pl.program_id(0)

    @pl.when(i == 0)
    def _():
        base_ref[...] = jnp.zeros_like(base_ref)

    y = _layer_norm_rows(DEEPNORM_ALPHA * x_ref[...] + h_ref[...], g_ref[...], b_ref[...])
    xo_ref[...] = y
    yb = y.astype(BF16)
    half = y.shape[1] // 2
    lo_bits = pltpu.bitcast(yb[:, :half].astype(F32), U32) >> 16
    hi_bits = pltpu.bitcast(yb[:, half:].astype(F32), U32)
    xpk_ref[...] = hi_bits | lo_bits

    ng, epg = N_EXPERT_GROUPS, EXPERTS_PER_GROUP
    logits = lax.dot_general(wr_ref[...], yb, (((1,), (1,)), ((), ())),
                             preferred_element_type=F32)
    scores = jax.nn.sigmoid(logits)
    sel = scores + rb_ref[...]
    sel_j = [sel[ng * j:ng * (j + 1), :] for j in range(epg)]
    sc_j = [scores[ng * j:ng * (j + 1), :] for j in range(epg)]
    gscore = None
    for j1 in range(epg):
        for j2 in range(j1 + 1, epg):
            s = sel_j[j1] + sel_j[j2]
            gscore = s if gscore is None else jnp.maximum(gscore, s)
    gmax = jnp.max(gscore, axis=0, keepdims=True)
    giota = lax.broadcasted_iota(I32, (ng, tm), 0)
    g_idx = jnp.min(jnp.where(gscore == gmax, giota, ng), axis=0, keepdims=True)
    in_g = giota == g_idx
    v = [jnp.sum(jnp.where(in_g, sel_j[j], 0.0), axis=0, keepdims=True) for j in range(epg)]
    s = [jnp.sum(jnp.where(in_g, sc_j[j], 0.0), axis=0, keepdims=True) for j in range(epg)]
    order = []
    for j in range(epg):
        r = jnp.zeros((1, tm), I32)
        for k in range(epg):
            if k == j:
                continue
            beats = (v[k] >= v[j]) if k < j else (v[k] > v[j])
            r = r + beats.astype(I32)
        order.append(r)
    eiota = lax.broadcasted_iota(I32, (N_EXPERTS, tm), 0)
    onehot = jnp.zeros((N_EXPERTS, tm), F32)
    loc, raw, rows = [], [], []
    for slot in range(TOP_K):
        lj = jnp.zeros((1, tm), I32)
        gs = jnp.zeros((1, tm), F32)
        for j in range(epg):
            hit = order[j] == slot
            lj = lj + jnp.where(hit, j, 0)
            gs = gs + jnp.where(hit, s[j], 0.0)
        row = lj * ng + g_idx
        onehot = onehot + (eiota == row).astype(F32)
        loc.append(lj)
        raw.append(gs)
        rows.append(row)
    denom = raw[0] + raw[1]
    srow = lax.broadcasted_iota(I32, (tm, tm), 0)
    scol = lax.broadcasted_iota(I32, (tm, tm), 1)
    before = (srow < scol).astype(BF16)
    prefix = jnp.dot(onehot.astype(BF16), before, preferred_element_type=F32) + base_ref[...]
    for slot in range(TOP_K):
        e_ref[slot:slot + 1, :] = g_idx * epg + loc[slot]
        gate_ref[slot:slot + 1, :] = raw[slot] / denom
        rk = jnp.sum(jnp.where(eiota == rows[slot], prefix, 0.0), axis=0, keepdims=True)
        rank_ref[slot:slot + 1, :] = rk.astype(I32)
    base_ref[...] = base_ref[...] + jnp.sum(onehot, axis=1, keepdims=True)
    cnt_ref[...] = jnp.broadcast_to(base_ref[...], cnt_ref.shape)


def ln_router(x, h, g, b, w_router, router_bias, *, tm):
    t, d = x.shape
    assert t % tm == 0
    ng, epg = N_EXPERT_GROUPS, EXPERTS_PER_GROUP
    wr = w_router.T.reshape(ng, epg, d).transpose(1, 0, 2).reshape(N_EXPERTS, d).astype(BF16)
    rb = router_bias.astype(F32).reshape(ng, epg).T.reshape(N_EXPERTS, 1)
    row = pl.BlockSpec((tm, d), lambda i: (i, 0))
    vec = pl.BlockSpec((1, d), lambda i: (0, 0))
    tok = pl.BlockSpec((TOP_K, tm), lambda i: (0, i))
    outs = pl.pallas_call(
        functools.partial(_ln_router_kernel, tm=tm),
        grid=(t // tm,),
        in_specs=[row, row, vec, vec,
                  pl.BlockSpec((N_EXPERTS, d), lambda i: (0, 0)),
                  pl.BlockSpec((N_EXPERTS, 1), lambda i: (0, 0))],
        out_specs=[row, pl.BlockSpec((tm, d // 2), lambda i: (i, 0)), tok, tok, tok,
                   pl.BlockSpec((N_EXPERTS, LANES), lambda i: (0, 0))],
        out_shape=[jax.ShapeDtypeStruct((t, d), F32), jax.ShapeDtypeStruct((t, d // 2), U32),
                   jax.ShapeDtypeStruct((TOP_K, t), I32), jax.ShapeDtypeStruct((TOP_K, t), F32),
                   jax.ShapeDtypeStruct((TOP_K, t), I32),
                   jax.ShapeDtypeStruct((N_EXPERTS, LANES), F32)],
        scratch_shapes=[pltpu.VMEM((N_EXPERTS, 1), F32)],
        compiler_params=_params(
            ("arbitrary",),
            [3 * _nbytes((tm, d), F32), _nbytes((tm, d), BF16), _nbytes((N_EXPERTS, d), BF16)],
            [4 * _nbytes((tm, d), F32)]),
        name="ln_router",
    )(x, h, g.reshape(1, d), b.reshape(1, d), wr, rb)
    x_new, xpk, e_idx, gate, rank, cnt = outs
    counts = cnt[:, 0].astype(I32).reshape(epg, ng).T.reshape(N_EXPERTS)
    return x_new, xpk, e_idx, gate, rank, counts


def _weight_copies(w_hbm, layer, ex, stage, wsem):
    rows = stage.shape[0] // WEIGHT_DMA_PARTS
    assert rows * WEIGHT_DMA_PARTS == stage.shape[0] and rows % SUBLANES == 0
    return [pltpu.make_async_copy(w_hbm.at[layer, ex, pl.ds(p * rows, rows)],
                                  stage.at[pl.ds(p * rows, rows)], wsem.at[p])
            for p in range(WEIGHT_DMA_PARTS)]


def _start_all(copies, priority=None):
    for p, cp in enumerate(copies):
        cp.start(priority=p % 2 if priority is None else priority)


def _wait_all(copies):
    for cp in copies:
        cp.wait()


def _wait_cast_refill(copies, refills, refill, stage, wbf):
    rows = stage.shape[0] // len(copies)
    for p, cp in enumerate(copies):
        cp.wait()
        wbf[p * rows:(p + 1) * rows, :] = stage[p * rows:(p + 1) * rows, :].astype(BF16)

        @pl.when(refill)
        def _(p=p):
            refills[p].start(priority=WEIGHT_DMA_PRIORITY)


def _expert_in_kernel(src_ref, crow_ref, cbase_ref, nch_ref, total_ref, xpk_hbm, w_hbm, act_hbm,
                      stage, wbf, ibuf, obuf, wsem, isem, osem, *, layer):
    e = pl.program_id(0)
    last = pl.num_programs(0) - 1
    total = total_ref[0]
    half = xpk_hbm.shape[1]

    def w_copies(ex):
        return _weight_copies(w_hbm, layer, ex, stage, wsem)

    def start_gather(g):
        row0 = crow_ref[g]
        for r in range(EXPERT_CHUNK):
            tok = src_ref[row0 + r]
            pltpu.make_async_copy(xpk_hbm.at[pl.ds(tok, 1)], ibuf.at[g % 2, pl.ds(r, 1)],
                                  isem.at[g % 2]).start(priority=ROW_DMA_PRIORITY)

    def wait_gather(g):
        pltpu.make_async_copy(xpk_hbm.at[pl.ds(0, EXPERT_CHUNK)], ibuf.at[g % 2], isem.at[g % 2]).wait()

    def out_copy(g):
        rows = pl.ds(pl.multiple_of(crow_ref[g], BLOCK_ROWS), EXPERT_CHUNK)
        return pltpu.make_async_copy(obuf.at[g % 2], act_hbm.at[rows], osem.at[g % 2])

    @pl.when(e == 0)
    def _():
        _start_all(w_copies(e), priority=WEIGHT_DMA_PRIORITY)

        @pl.when(total > 0)
        def _():
            start_gather(0)

    _wait_cast_refill(w_copies(e), w_copies(e + 1), e < last, stage, wbf)

    base = cbase_ref[e]

    def body(i, carry):
        g = base + i

        @pl.when(g + 1 < total)
        def _():
            start_gather(g + 1)

        wait_gather(g)
        packed = ibuf[g % 2]
        x_lo = pltpu.bitcast(packed << 16, F32).astype(BF16)
        x_hi = pltpu.bitcast(packed & jnp.uint32(0xFFFF0000), F32).astype(BF16)
        h = (jnp.dot(x_lo, wbf[0:half, :], preferred_element_type=F32)
             + jnp.dot(x_hi, wbf[half:2 * half, :], preferred_element_type=F32))
        h1 = h[:, :D_FF]
        obuf[g % 2] = (h1 * jax.nn.sigmoid(h1) * h[:, D_FF:]).astype(obuf.dtype)

        @pl.when(g >= 1)
        def _():
            out_copy(g - 1).wait()

        out_copy(g).start()
        return carry

    lax.fori_loop(0, nch_ref[e], body, 0)

    @pl.when(e == last)
    def _():
        @pl.when(total > 0)
        def _():
            out_copy(total - 1).wait()

        end = jnp.where(total > 0, crow_ref[jnp.maximum(total - 1, 0)] + EXPERT_CHUNK, 0)
        obuf[0] = jnp.zeros(obuf.shape[1:], obuf.dtype)

        def tail_copy(blk):
            rows = pl.ds(pl.multiple_of(end + blk * BLOCK_ROWS, BLOCK_ROWS), BLOCK_ROWS)
            return pltpu.make_async_copy(obuf.at[0, 0:BLOCK_ROWS], act_hbm.at[rows], osem.at[0])

        n_tail = (act_hbm.shape[0] - end) // BLOCK_ROWS
        lax.fori_loop(0, n_tail, lambda blk, c: (tail_copy(blk).start(), c)[1], 0)
        lax.fori_loop(0, n_tail, lambda blk, c: (tail_copy(blk).wait(), c)[1], 0)


def expert_in(xpk, src, chunk_row, chunk_base, n_chunks, total, moe_w_in, layer):
    rows = src.shape[0]
    half = xpk.shape[1]
    wshape = moe_w_in.shape[-2:]
    resident = [_nbytes(wshape, F32), _nbytes(wshape, BF16),
                2 * _nbytes((EXPERT_CHUNK, half), U32), 2 * _nbytes((EXPERT_CHUNK, D_FF), BF16),
                _nbytes((EXPERT_CHUNK, 2 * half), BF16), 2 * _nbytes((EXPERT_CHUNK, wshape[1]), F32)]
    return pl.pallas_call(
        functools.partial(_expert_in_kernel, layer=layer),
        grid_spec=pltpu.PrefetchScalarGridSpec(
            num_scalar_prefetch=5,
            grid=(N_EXPERTS,),
            in_specs=[pl.BlockSpec(memory_space=pl.ANY), pl.BlockSpec(memory_space=pl.ANY)],
            out_specs=pl.BlockSpec(memory_space=pl.ANY),
            scratch_shapes=[pltpu.VMEM(wshape, F32), pltpu.VMEM(wshape, BF16),
                            pltpu.VMEM((2, EXPERT_CHUNK, half), U32),
                            pltpu.VMEM((2, EXPERT_CHUNK, D_FF), BF16),
                            pltpu.SemaphoreType.DMA((WEIGHT_DMA_PARTS,)),
                            pltpu.SemaphoreType.DMA((2,)),
                            pltpu.SemaphoreType.DMA((2,))]),
        out_shape=jax.ShapeDtypeStruct((rows, D_FF), BF16),
        compiler_params=pltpu.CompilerParams(
            dimension_semantics=("arbitrary",), has_side_effects=True,
            vmem_limit_bytes=int(min(VMEM_CAP, sum(resident) + 4 * 1024 * 1024))),
        name="expert_in",
    )(src, chunk_row, chunk_base, n_chunks, total, xpk, moe_w_in)


def _expert_kernel(pstart_ref, padded_ref, in_hbm, w_hbm, out_hbm,
                   stage, wbf, ibuf, obuf, wsem, isem, osem, *, layer):
    e = pl.program_id(0)

    def w_copies(ex):
        return _weight_copies(w_hbm, layer, ex, stage, wsem)

    @pl.when(e == 0)
    def _():
        _start_all(w_copies(e), priority=WEIGHT_DMA_PRIORITY)

    _wait_cast_refill(w_copies(e), w_copies(e + 1), e + 1 < pl.num_programs(0), stage, wbf)

    start = pstart_ref[e]
    n_chunks = (padded_ref[e] + EXPERT_CHUNK - 1) // EXPERT_CHUNK

    def chunk_rows(i):
        return pl.ds(pl.multiple_of(start + i * EXPERT_CHUNK, BLOCK_ROWS), EXPERT_CHUNK)

    def in_copy(i):
        return pltpu.make_async_copy(in_hbm.at[chunk_rows(i)], ibuf.at[i % 2], isem.at[i % 2])

    def out_copies(i):
        copies = []
        for p in range(EXPERT_CHUNK // BLOCK_ROWS):
            rows = pl.ds(pl.multiple_of(start + i * EXPERT_CHUNK + p * BLOCK_ROWS, BLOCK_ROWS),
                         BLOCK_ROWS)
            copies.append(pltpu.make_async_copy(
                obuf.at[i % 2, pl.ds(p * BLOCK_ROWS, BLOCK_ROWS)], out_hbm.at[rows], osem.at[i % 2]))
        return copies

    @pl.when(n_chunks > 0)
    def _():
        in_copy(0).start()

    def body(i, carry):
        @pl.when(i + 1 < n_chunks)
        def _():
            in_copy(i + 1).start()

        in_copy(i).wait()
        obuf[i % 2] = jnp.dot(ibuf[i % 2], wbf[...], preferred_element_type=F32).astype(obuf.dtype)
        _start_all(out_copies(i))

        @pl.when(i >= 1)
        def _():
            _wait_all(out_copies(i - 1))
        return carry

    lax.fori_loop(0, n_chunks, body, 0)

    @pl.when(n_chunks > 0)
    def _():
        _wait_all(out_copies(n_chunks - 1))

    @pl.when(e + 1 == pl.num_programs(0))
    def _():
        end = start + n_chunks * EXPERT_CHUNK
        obuf[0] = jnp.zeros(obuf.shape[1:], obuf.dtype)

        def tail_copy(blk):
            rows = pl.ds(pl.multiple_of(end + blk * BLOCK_ROWS, BLOCK_ROWS), BLOCK_ROWS)
            return pltpu.make_async_copy(obuf.at[0, 0:BLOCK_ROWS], out_hbm.at[rows], osem.at[0])

        n_tail = (out_hbm.shape[0] - end) // BLOCK_ROWS
        lax.fori_loop(0, n_tail, lambda blk, c: (tail_copy(blk).start(), c)[1], 0)
        lax.fori_loop(0, n_tail, lambda blk, c: (tail_copy(blk).wait(), c)[1], 0)


def _expert_call(rows_in, w, pad_start, padded, layer, *, out_dtype, name):
    rows, k = rows_in.shape
    n_out = w.shape[-1]
    wshape = w.shape[-2:]
    resident = [_nbytes(wshape, F32), _nbytes(wshape, BF16),
                2 * _nbytes((EXPERT_CHUNK, k), rows_in.dtype),
                2 * _nbytes((EXPERT_CHUNK, n_out), out_dtype),
                2 * _nbytes((EXPERT_CHUNK, wshape[1]), F32)]
    return pl.pallas_call(
        functools.partial(_expert_kernel, layer=layer),
        grid_spec=pltpu.PrefetchScalarGridSpec(
            num_scalar_prefetch=2,
            grid=(N_EXPERTS,),
            in_specs=[pl.BlockSpec(memory_space=pl.ANY), pl.BlockSpec(memory_space=pl.ANY)],
            out_specs=pl.BlockSpec(memory_space=pl.ANY),
            scratch_shapes=[pltpu.VMEM(wshape, F32), pltpu.VMEM(wshape, BF16),
                            pltpu.VMEM((2, EXPERT_CHUNK, k), rows_in.dtype),
                            pltpu.VMEM((2, EXPERT_CHUNK, n_out), out_dtype),
                            pltpu.SemaphoreType.DMA((WEIGHT_DMA_PARTS,)),
                            pltpu.SemaphoreType.DMA((2,)),
                            pltpu.SemaphoreType.DMA((2,))]),
        out_shape=jax.ShapeDtypeStruct((rows, n_out), out_dtype),
        compiler_params=pltpu.CompilerParams(
            dimension_semantics=("arbitrary",), has_side_effects=True,
            vmem_limit_bytes=int(min(VMEM_CAP, sum(resident) + 4 * 1024 * 1024))),
        name=name,
    )(pad_start, padded, rows_in, w)


def expert_out(act, pad_start, padded, moe_w_out, layer):
    return _expert_call(act, moe_w_out, pad_start, padded, layer, out_dtype=F32,
                        name="expert_out")


def _combine_ln_kernel(dest_ref, x_ref, gate_ref, g_ref, b_ref, y_hbm, *rest, tm, t, with_bf16):
    if with_bf16:
        xo_ref, xb_ref, ybuf, sem = rest
    else:
        xo_ref, ybuf, sem = rest
        xb_ref = None
    i = pl.program_id(0)
    n = pl.num_programs(0)

    def start_rows(step):
        buf = step % 2

        for r in range(tm):
            for k in range(TOP_K):
                d = dest_ref[k * t + step * tm + r]
                pltpu.make_async_copy(y_hbm.at[pl.ds(d, 1)], ybuf.at[buf, k, pl.ds(r, 1)],
                                      sem.at[buf]).start(priority=(r + k) % 2)

    def wait_rows(step):
        buf = step % 2
        for k in range(TOP_K):
            pltpu.make_async_copy(y_hbm.at[pl.ds(0, tm)], ybuf.at[buf, k], sem.at[buf]).wait()

    @pl.when(i == 0)
    def _():
        start_rows(i)

    @pl.when(i + 1 < n)
    def _():
        start_rows(i + 1)

    wait_rows(i)
    buf = i % 2
    gate = gate_ref[...]
    h = gate[:, 0:1] * ybuf[buf, 0] + gate[:, 1:2] * ybuf[buf, 1]
    y = _layer_norm_rows(DEEPNORM_ALPHA * x_ref[...] + h, g_ref[...], b_ref[...])
    xo_ref[...] = y
    if with_bf16:
        xb_ref[...] = y.astype(BF16)


def combine_ln(x, y_disp, dest_flat, gate_t, g, b, *, tm, with_bf16):
    t, d = x.shape
    assert t % tm == 0
    row = pl.BlockSpec((tm, d), lambda i, dest: (i, 0))
    vec = pl.BlockSpec((1, d), lambda i, dest: (0, 0))
    out_specs = [row, row] if with_bf16 else [row]
    out_shape = [jax.ShapeDtypeStruct((t, d), F32)]
    if with_bf16:
        out_shape.append(jax.ShapeDtypeStruct((t, d), BF16))
    outs = pl.pallas_call(
        functools.partial(_combine_ln_kernel, tm=tm, t=t, with_bf16=with_bf16),
        grid_spec=pltpu.PrefetchScalarGridSpec(
            num_scalar_prefetch=1,
            grid=(t // tm,),
            in_specs=[row, pl.BlockSpec((tm, TOP_K), lambda i, dest: (i, 0)), vec, vec,
                      pl.BlockSpec(memory_space=pl.ANY)],
            out_specs=out_specs,
            scratch_shapes=[pltpu.VMEM((2, TOP_K, tm, d), F32), pltpu.SemaphoreType.DMA((2,))]),
        out_shape=out_shape,
        compiler_params=_params(
            ("arbitrary",),
            [2 * _nbytes((tm, d), F32), _nbytes((tm, d), BF16)],
            [_nbytes((2, TOP_K, tm, d), F32), 4 * _nbytes((tm, d), F32)]),
        name="moe_combine_ln",
    )(dest_flat, x, gate_t, g.reshape(1, d), b.reshape(1, d), y_disp)
    return (outs[0], outs[1]) if with_bf16 else (outs[0], None)


def moe_ffn_ln(x, xpk, e_idx, gate, rank, counts, moe_w_in, moe_w_out, layer, g, b, *,
               with_bf16, tm):
    t, d = x.shape
    tk = t * TOP_K
    padded = ((counts + BLOCK_ROWS - 1) // BLOCK_ROWS * BLOCK_ROWS).astype(I32)
    pad_end = jnp.cumsum(padded).astype(I32)
    pad_start = pad_end - padded
    n_blocks = (tk + N_EXPERTS * (BLOCK_ROWS - 1) + BLOCK_ROWS - 1) // BLOCK_ROWS + 1
    e_flat = e_idx.reshape(tk)
    dest_flat = pad_start[e_flat] + rank.reshape(tk)
    tok_flat = jnp.arange(tk, dtype=I32) % t
    src = jnp.zeros((n_blocks * BLOCK_ROWS,), I32).at[dest_flat].set(tok_flat)
    n_chunks = (padded + EXPERT_CHUNK - 1) // EXPERT_CHUNK
    chunk_end = jnp.cumsum(n_chunks).astype(I32)
    chunk_base = chunk_end - n_chunks
    max_chunks = n_blocks // (EXPERT_CHUNK // BLOCK_ROWS) + N_EXPERTS
    gidx = jnp.arange(max_chunks, dtype=I32)
    owner = jnp.minimum(jnp.sum(gidx[:, None] >= chunk_end[None, :], axis=1), N_EXPERTS - 1)
    chunk_row = (pad_start[owner] + (gidx - chunk_base[owner]) * EXPERT_CHUNK).astype(I32)
    act = expert_in(xpk, src, chunk_row, chunk_base, n_chunks.astype(I32), chunk_end[-1:],
                    moe_w_in, layer)
    y_disp = expert_out(act, pad_start, padded, moe_w_out, layer)
    return combine_ln(x, y_disp, dest_flat, gate.T, g, b, tm=tm, with_bf16=with_bf16)


def _softplus(x):
    return jnp.maximum(x, 0.0) + jnp.log1p(jnp.exp(-jnp.abs(x)))


def _silu(x):
    return x * jax.nn.sigmoid(x)


def _dot_01_f32(m01, v, *, ones_left):
    hi = v.astype(BF16)
    r = v - hi.astype(F32)
    mid = r.astype(BF16)
    lo = (r - mid.astype(F32)).astype(BF16)

    def dot(p):
        lhs, rhs = (m01, p) if ones_left else (p, m01)
        return jnp.dot(lhs, rhs, preferred_element_type=F32)
    return dot(hi) + dot(mid) + dot(lo)


def _ssd_kernel(xs_ref, b_ref, c_ref, gz_ref, dt_ref, dtT_ref,
                dtb_ref, dtbT_ref, alog_ref, alogT_ref, dsk_ref, nw_ref,
                y_ref, state_ref):
    q = SSD_CHUNK

    @pl.when(pl.program_id(1) == 0)
    def _():
        state_ref[...] = jnp.zeros_like(state_ref)

    xs = xs_ref[...]
    bm = b_ref[...]
    cm = c_ref[...]

    dt = _softplus(dt_ref[...] + dtb_ref[...])
    dt_t = _softplus(dtT_ref[...] + dtbT_ref[...])
    a = -jnp.exp(alog_ref[...])
    a_t = -jnp.exp(alogT_ref[...])
    row = lax.broadcasted_iota(I32, (q, q), 0)
    col = lax.broadcasted_iota(I32, (q, q), 1)
    causal = row >= col
    a_cum = _dot_01_f32(jnp.where(causal, 1.0, 0.0).astype(BF16), dt * a, ones_left=True)
    a_cum_t = _dot_01_f32(jnp.where(row <= col, 1.0, 0.0).astype(BF16), dt_t * a_t,
                          ones_left=False)
    a_last = a_cum[q - 1:q, :]
    hrow = lax.broadcasted_iota(I32, (SSD_HPG, SSD_GW), 0)
    hcol = lax.broadcasted_iota(I32, (SSD_HPG, SSD_GW), 1)
    expand = jnp.where(hcol // SSD_HEAD_DIM == hrow, 1.0, 0.0).astype(BF16)
    decay_in = _dot_01_f32(expand, jnp.exp(a_cum), ones_left=False)
    w_state = _dot_01_f32(expand, jnp.exp(a_last - a_cum) * dt, ones_left=False)
    chunk_decay = decay_in[q - 1:q, :]

    cmb = cm.astype(BF16)
    bmb = bm.astype(BF16)
    cb = lax.dot_general(cmb, bmb, (((1,), (1,)), ((), ())), preferred_element_type=F32)
    prev = state_ref[...]
    y_off = jnp.dot(cmb, prev.astype(BF16), preferred_element_type=F32) * decay_in
    xw = (w_state * xs).astype(BF16)
    state_ref[...] = chunk_decay * prev + jnp.dot(bm.T.astype(BF16), xw, preferred_element_type=F32)

    lane = lax.broadcasted_iota(I32, (q, LANES), 1)
    heads_per_tile = LANES // SSD_HEAD_DIM
    ys = []
    for tile in range(SSD_GW // LANES):
        ms = []
        for hh in range(heads_per_tile):
            h = tile * heads_per_tile + hh
            seg = a_cum[:, h:h + 1] - a_cum_t[h:h + 1, :]
            decay = jnp.where(causal, jnp.exp(seg), 0.0)
            ms.append((cb * decay * dt_t[h:h + 1, :]).astype(BF16))
        x_tile = xs[:, tile * LANES:(tile + 1) * LANES]
        rhs = jnp.concatenate(
            [jnp.where(lane // SSD_HEAD_DIM == hh, x_tile, 0.0).astype(BF16)
             for hh in range(heads_per_tile)], axis=0)
        ys.append(jnp.dot(jnp.concatenate(ms, axis=1), rhs, preferred_element_type=F32))
    y = jnp.concatenate(ys, axis=1) + y_off + dsk_ref[...] * xs
    y = y * gz_ref[...]
    y = y * lax.rsqrt(jnp.mean(y * y, axis=-1, keepdims=True) + RMS_EPS) * nw_ref[...]
    y_ref[...] = y.astype(y_ref.dtype)


def ssd_core(gz, xbc, dt_raw, dt_bias, a_log, d_skip, norm_w):
    t = gz.shape[0]
    q, gw, ns, hpg, ng = SSD_CHUNK, SSD_GW, SSD_STATE, SSD_HPG, SSD_GROUPS
    assert t % q == 0
    b_blk = D_INNER // ns
    c_blk = b_blk + ng
    dt_g = dt_raw.reshape(t, ng, hpg).transpose(1, 0, 2)
    dt_gt = dt_raw.T.reshape(ng, hpg, t)
    dtb = dt_bias.astype(F32).reshape(ng, 1, hpg)
    dtb_t = dt_bias.astype(F32).reshape(ng, hpg, 1)
    alog = a_log.astype(F32).reshape(ng, 1, hpg)
    alog_t = a_log.astype(F32).reshape(ng, hpg, 1)
    dsk = jnp.repeat(d_skip.astype(F32), SSD_HEAD_DIM).reshape(1, D_INNER)
    nw = norm_w.astype(F32).reshape(1, D_INNER)

    def cspec(rows, width, blk):
        return pl.BlockSpec((rows, width), lambda g, c: (0, blk(g)))

    in_specs = [
        pl.BlockSpec((q, gw), lambda g, c: (c, g)),
        pl.BlockSpec((q, ns), lambda g, c: (c, b_blk + g)),
        pl.BlockSpec((q, ns), lambda g, c: (c, c_blk + g)),
        pl.BlockSpec((q, gw), lambda g, c: (c, g)),
        pl.BlockSpec((None, q, hpg), lambda g, c: (g, c, 0)),
        pl.BlockSpec((None, hpg, q), lambda g, c: (g, 0, c)),
        pl.BlockSpec((None, 1, hpg), lambda g, c: (g, 0, 0)),
        pl.BlockSpec((None, hpg, 1), lambda g, c: (g, 0, 0)),
        pl.BlockSpec((None, 1, hpg), lambda g, c: (g, 0, 0)),
        pl.BlockSpec((None, hpg, 1), lambda g, c: (g, 0, 0)),
        cspec(1, gw, lambda g: g), cspec(1, gw, lambda g: g),
    ]
    return pl.pallas_call(
        _ssd_kernel,
        grid=(ng, t // q),
        in_specs=in_specs,
        out_specs=pl.BlockSpec((q, gw), lambda g, c: (c, g)),
        out_shape=jax.ShapeDtypeStruct((t, D_INNER), BF16),
        scratch_shapes=[pltpu.VMEM((ns, gw), F32)],
        compiler_params=_params(
            ("arbitrary", "arbitrary"),
            [2 * _nbytes((q, gw), F32), 2 * _nbytes((q, ns), F32), _nbytes((q, gw), BF16)],
            [_nbytes((ns, gw), F32), 16 * _nbytes((q, gw), F32)]),
        name="ssd_core",
    )(xbc, xbc, xbc, gz, dt_g, dt_gt, dtb, dtb_t, alog, alog_t, dsk, nw)


MM_TM = 1024
MM_TN = 512
WIDE_K_TM = 512
EPILOGUE_PARTS = 8
LN_TM = 256
COMBINE_TM = 128


def pool_mixer(xb, w_in, w_group, scale, w_out):
    p = pool_in(xb, w_in, tm=MM_TM, tn=MM_TN)
    mixed = pool_group_matmul(p, w_group, scale, tm=MM_TM)
    return matmul_cols(mixed, w_out, 0, D_MODEL, tm=MM_TM, tn=MM_TN, out_dtype=F32, name="pool_out")


def ssd_mixer(xb, w_in, conv_w, conv_b, dt_bias, a_log, d_skip, norm_w, w_out):
    gz = matmul_cols_act(xb, w_in, 0, D_INNER, tm=MM_TM, tn=MM_TN, parts=EPILOGUE_PARTS,
                         name="ssd_in_z")
    conv = (conv_w.astype(F32), conv_b.astype(F32).reshape(1, SSD_CONV_DIM))
    xbc = matmul_cols_act(xb, w_in, D_INNER, SSD_CONV_DIM, tm=MM_TM, tn=MM_TN, parts=EPILOGUE_PARTS,
                          conv=conv, name="ssd_in_xbc")
    dt_raw = matmul_cols(xb, w_in, D_INNER + SSD_CONV_DIM, SSD_HEADS, tm=MM_TM, tn=SSD_HEADS,
                         out_dtype=F32, name="ssd_in_dt")
    y = ssd_core(gz, xbc, dt_raw, dt_bias, a_log, d_skip, norm_w)
    return matmul_bf16w(y, w_out.astype(BF16), tm=WIDE_K_TM, tn=MM_TN, out_dtype=F32, name="ssd_out")


def kernel(x, pool_w_in, pool_w_group, pool_scale, pool_w_out, ssd_w_in, ssd_conv_w, ssd_conv_b,
           ssd_dt_bias, ssd_a_log, ssd_d, ssd_norm_w, ssd_w_out, moe_w_router, moe_router_bias,
           moe_w_in, moe_w_out, ln_mix_g, ln_mix_b, ln_ffn_g, ln_ffn_b):
    bsz, seq, d = x.shape
    x = x.reshape(bsz * seq, d)
    assert bsz == 1
    xb = x.astype(BF16)
    for i in range(DEPTH):
        j = i // N_MIXERS
        if i % N_MIXERS == 0:
            h = pool_mixer(xb, pool_w_in[j], pool_w_group[j], pool_scale[j], pool_w_out[j])
        else:
            h = ssd_mixer(xb, ssd_w_in[j], ssd_conv_w[j], ssd_conv_b[j], ssd_dt_bias[j],
                          ssd_a_log[j], ssd_d[j], ssd_norm_w[j], ssd_w_out[j])
        x, xpk, e_idx, gate, rank, counts = ln_router(
            x, h, ln_mix_g[i], ln_mix_b[i], moe_w_router, moe_router_bias, tm=LN_TM)
        x, xb = moe_ffn_ln(x, xpk, e_idx, gate, rank, counts, moe_w_in, moe_w_out, i,
                           ln_ffn_g[i], ln_ffn_b[i], with_bf16=(i + 1 < DEPTH), tm=COMBINE_TM)
    return x.reshape(bsz, seq, d)
```
